```python
import math
import jax
import jax.numpy as jnp
from jax import lax
import numpy as np

D_MODEL = 1024
BATCH = 16
SEQ = 256
DEPTH = 4
DEC_BATCH = 2
DEC_SEQ = 2048
PAST_LEN = 256

GRID_W = 64
GROUP_W = D_MODEL // 2
D_MIX = 4 * GROUP_W
CHUNK = 128
Q_BLOCK = 128
D_CONV = 5
ROPE_THETA = 10000.0
ROPE_DIM = 64
EPS = 1e-6

SSM_HEADS = 8
SSM_HEAD_DIM = GROUP_W // SSM_HEADS
SSM_GROUPS = 2
SSM_STATE = 64
SSM_CONV_CH = GROUP_W + 2 * SSM_GROUPS * SSM_STATE
A_COLS = GROUP_W + SSM_CONV_CH + 2 * SSM_HEADS
DIFF_HEADS = 4
DIFF_QK_DIM = 64
DIFF_V_DIM = GROUP_W // DIFF_HEADS
B_QK = DIFF_HEADS * 2 * DIFF_QK_DIM
B_COLS = 2 * B_QK + GROUP_W
MLSTM_HEADS = 4
MLSTM_HEAD_DIM = GROUP_W // MLSTM_HEADS
C_COLS = 4 * GROUP_W + 4 * MLSTM_HEADS
GQA_HEADS = 8
GQA_KV_HEADS = 2
GQA_HEAD_DIM = GROUP_W // GQA_HEADS
GQA_GROUP = GQA_HEADS // GQA_KV_HEADS
D_COLS = GROUP_W + 2 * GQA_KV_HEADS * GQA_HEAD_DIM

IN_COLS = A_COLS + B_COLS + C_COLS + D_COLS
D_FF = -(-8 * D_MODEL // (3 * 256)) * 256

kernel_name = 'hybrid_diffusion_prefix_trunk_step'


def rms_norm(x):
    xf = x.astype(jnp.float32)
    return (xf * lax.rsqrt(jnp.mean(xf * xf, axis=-1, keepdims=True) + EPS)).astype(x.dtype)


def grid_rope(n_tok, head_dim):
    rows = n_tok // GRID_W
    r, c = jnp.meshgrid(jnp.arange(rows, dtype=jnp.float32), jnp.arange(GRID_W, dtype=jnp.float32), indexing='ij')
    nf = head_dim // 4
    freqs = ROPE_THETA ** (-jnp.arange(nf, dtype=jnp.float32) / nf)
    ang = jnp.stack([r.reshape(-1)[:, None] * freqs, c.reshape(-1)[:, None] * freqs], axis=1)
    return jnp.cos(ang), jnp.sin(ang)


def apply_rope(x, rope):
    cos, sin = rope
    b, L, h, d = x.shape
    xs = x.reshape(b, L, h, 2, 2, d // 4)
    cs = cos[None, :, None].astype(x.dtype)
    sn = sin[None, :, None].astype(x.dtype)
    x1, x2 = xs[..., 0, :], xs[..., 1, :]
    return jnp.stack([x1 * cs - x2 * sn, x2 * cs + x1 * sn], axis=-2).reshape(b, L, h, d)


def dwconv(x, w, bias):
    L = x.shape[1]
    pad = D_CONV // 2
    xp = jnp.pad(x, ((0, 0), (pad, pad), (0, 0)))
    out = bias
    for tap in range(D_CONV):
        out = out + w[tap] * xp[:, tap:tap + L]
    return out


def flip(t):
    return jnp.flip(t, axis=1)


def sweep_q_blocks(fn, q):
    b, L = q.shape[:2]
    nb = L // Q_BLOCK
    qb = jnp.moveaxis(q.reshape(b, nb, Q_BLOCK, *q.shape[2:]), 1, 0)
    out = lax.map(fn, qb)
    return jnp.moveaxis(out, 0, 1).reshape(b, L, *out.shape[3:])


def ssd_scan(x, dt, a_neg, bm, cm, h0):
    b, L, H, P = x.shape
    nc = L // CHUNK
    rep = H // bm.shape[2]
    ch = lambda t: t.reshape(b, nc, CHUNK, *t.shape[2:])
    xc, dtc = ch(x), ch(dt)
    bc, cc = ch(jnp.repeat(bm, rep, axis=2)), ch(jnp.repeat(cm, rep, axis=2))
    cum = jnp.cumsum(dtc * a_neg, axis=2)
    causal = jnp.tril(jnp.ones((CHUNK, CHUNK), dtype=bool))[:, :, None]
    decay = jnp.exp(jnp.where(causal, cum[:, :, :, None] - cum[:, :, None], -jnp.inf))
    xdt = xc * dtc[..., None]
    scores = jnp.einsum('bcihn,bcjhn->bcijh', cc, bc) * decay
    y_diag = jnp.einsum('bcijh,bcjhp->bcihp', scores, xdt)
    last = cum[:, :, -1]
    w_end = jnp.exp(last[:, :, None] - cum)
    chunk_state = jnp.einsum('bcjh,bcjhn,bcjhp->bchpn', w_end, bc, xdt)

    def step(h, inp):
        st, dec = inp
        return jnp.exp(dec)[..., None, None] * h + st, h

    h_fin, h_prev = lax.scan(step, h0.astype(jnp.float32), (jnp.moveaxis(chunk_state, 1, 0), jnp.moveaxis(last, 1, 0)))
    h_prev = jnp.moveaxis(h_prev, 0, 1)
    y_off = jnp.einsum('bcihn,bchpn->bcihp', cc, h_prev) * jnp.exp(cum)[..., None]
    return (y_diag + y_off).reshape(b, L, H, P), h_fin


def mlstm_scan(q, k, v, li, lf, state):
    b, L, H, dh = q.shape
    nc = L // CHUNK
    ch = lambda t: t.reshape(b, nc, CHUNK, *t.shape[2:])
    qc, kc, vc, lic, lfc = ch(q), ch(k), ch(v), ch(li), ch(lf)
    cum = jnp.cumsum(lfc, axis=2)
    causal = jnp.tril(jnp.ones((CHUNK, CHUNK), dtype=bool))[:, :, None]
    dmat = jnp.where(causal, cum[:, :, :, None] - cum[:, :, None] + lic[:, :, None], -jnp.inf)
    last = cum[:, :, -1]
    g_end = last[:, :, None] - cum + lic
    m_loc = jnp.max(g_end, axis=2)
    w_end = jnp.exp(g_end - m_loc[:, :, None])
    c_loc = jnp.einsum('bcjh,bcjhd,bcjhe->bchde', w_end, kc, vc)
    n_loc = jnp.einsum('bcjh,bcjhd->bchd', w_end, kc)

    def step(carry, inp):
        c_p, n_p, m_p = carry
        c_l, n_l, m_l, a_l = inp
        m_new = jnp.maximum(a_l + m_p, m_l)
        s_p = jnp.exp(a_l + m_p - m_new)
        s_l = jnp.exp(m_l - m_new)
        new = (s_p[..., None, None] * c_p + s_l[..., None, None] * c_l, s_p[..., None] * n_p + s_l[..., None] * n_l, m_new)
        return new, carry

    init = tuple(s.astype(jnp.float32) for s in state)
    final, prev = lax.scan(step, init, tuple(jnp.moveaxis(t, 1, 0) for t in (c_loc, n_loc, m_loc, last)))
    c_p, n_p, m_p = (jnp.moveaxis(t, 0, 1) for t in prev)
    inter = cum + m_p[:, :, None]
    m_t = jnp.maximum(inter, jnp.max(dmat, axis=3))
    w_intra = jnp.exp(dmat - m_t[:, :, :, None])
    w_inter = jnp.exp(inter - m_t)
    s = jnp.einsum('bcihd,bcjhd->bcijh', qc, kc) * w_intra
    num = jnp.einsum('bcijh,bcjhe->bcihe', s, vc) + w_inter[..., None] * jnp.einsum('bcihd,bchde->bcihe', qc, c_p)
    den = jnp.sum(s, axis=3) + w_inter * jnp.einsum('bcihd,bchd->bcih', qc, n_p)
    h = num / jnp.maximum(jnp.abs(den), jnp.exp(-m_t))[..., None]
    return h.reshape(b, L, H, dh), final


def mixer_ssd(u, conv_w, conv_b, a_log, dt_bias, d_skip, norm_g, h0):
    b, L, _ = u.shape
    z, xbc, dt_raw = jnp.split(u, [GROUP_W, GROUP_W + SSM_CONV_CH], axis=-1)
    xbc = jax.nn.silu(dwconv(xbc, conv_w, conv_b))
    xs, bs, cs = jnp.split(xbc, [GROUP_W, GROUP_W + SSM_GROUPS * SSM_STATE], axis=-1)
    x = xs.reshape(b, L, SSM_HEADS, SSM_HEAD_DIM)
    bm = bs.reshape(b, L, SSM_GROUPS, SSM_STATE)
    cm = cs.reshape(b, L, SSM_GROUPS, SSM_STATE)
    dt = jax.nn.softplus((dt_raw.reshape(b, L, 2, SSM_HEADS) + dt_bias).astype(jnp.float32))
    a = -jnp.exp(a_log.astype(jnp.float32))
    y_f, h_f = ssd_scan(x, dt[:, :, 0], a[0], bm, cm, h0[:, 0])
    y_b, h_b = ssd_scan(flip(x), flip(dt[:, :, 1]), a[1], flip(bm), flip(cm), h0[:, 1])
    y = y_f + flip(y_b) + d_skip[:, None] * x
    y = y.reshape(b, L, GROUP_W).astype(u.dtype) * jax.nn.silu(z)
    return rms_norm(y) * norm_g, jnp.stack([h_f, h_b], axis=1).astype(u.dtype)


def mixer_diff(u, lq1, lk1, lq2, lk2, lam_init, rope, ctx):
    b, L, _ = u.shape
    q, k, v = jnp.split(u, [B_QK, 2 * B_QK], axis=-1)
    q = q.reshape(b, L, DIFF_HEADS, 2, DIFF_QK_DIM)
    k = k.reshape(b, L, DIFF_HEADS, 2, DIFF_QK_DIM)
    v = v.reshape(b, L, DIFF_HEADS, DIFF_V_DIM)
    new_ctx = (k, v)
    if rope is not None:
        q = apply_rope(q.reshape(b, L, 2 * DIFF_HEADS, DIFF_QK_DIM), rope).reshape(q.shape)
        k = apply_rope(k.reshape(b, L, 2 * DIFF_HEADS, DIFF_QK_DIM), rope).reshape(k.shape)
    if ctx is not None:
        k = jnp.concatenate([ctx[0].astype(k.dtype), k], axis=1)
        v = jnp.concatenate([ctx[1].astype(v.dtype), v], axis=1)
    lam = (jnp.exp(jnp.sum(lq1 * lk1)) - jnp.exp(jnp.sum(lq2 * lk2)) + lam_init).astype(jnp.float32)
    scale = DIFF_QK_DIM ** -0.5

    def block(qb):
        s = jnp.einsum('bqhrd,bkhrd->bhrqk', qb, k).astype(jnp.float32) * scale
        p = jax.nn.softmax(s, axis=-1)
        att = p[:, :, 0] - lam * p[:, :, 1]
        return jnp.einsum('bhqk,bkhe->bqhe', att.astype(v.dtype), v)

    o = rms_norm(sweep_q_blocks(block, q)) * (1.0 - lam_init)
    return o.reshape(b, L, GROUP_W), new_ctx


def mixer_mlstm(u, conv_w, conv_b, gate_b, norm_g, state):
    b, L, _ = u.shape
    qk, v, o, gates = jnp.split(u, [2 * GROUP_W, 3 * GROUP_W, 4 * GROUP_W], axis=-1)
    qk = jax.nn.silu(dwconv(qk, conv_w, conv_b))
    q, k = jnp.split(qk, 2, axis=-1)
    shp = (b, L, MLSTM_HEADS, MLSTM_HEAD_DIM)
    q = q.reshape(shp) * MLSTM_HEAD_DIM ** -0.5
    k = k.reshape(shp)
    v = v.reshape(shp)
    g = (gates.reshape(b, L, 2, 2, MLSTM_HEADS) + gate_b).astype(jnp.float32)
    li = g[:, :, :, 0]
    lf = jax.nn.log_sigmoid(g[:, :, :, 1])
    c0, n0, m0 = state
    h_f, (cf, nf, mf) = mlstm_scan(q, k, v, li[:, :, 0], lf[:, :, 0], (c0[:, 0], n0[:, 0], m0[:, 0]))
    h_b, (cb, nb, mb) = mlstm_scan(flip(q), flip(k), flip(v), flip(li[:, :, 1]), flip(lf[:, :, 1]), (c0[:, 1], n0[:, 1], m0[:, 1]))
    h = rms_norm((h_f + flip(h_b)).astype(u.dtype)).reshape(b, L, GROUP_W) * norm_g
    h = jax.nn.sigmoid(o) * h
    new_state = (jnp.stack([cf, cb], axis=1).astype(u.dtype), jnp.stack([nf, nb], axis=1).astype(u.dtype), jnp.stack([mf, mb], axis=1).astype(u.dtype))
    return h, new_state


def mixer_gqa(u, qn_g, kn_g, rope, ctx):
    b, L, _ = u.shape
    kvw = GQA_KV_HEADS * GQA_HEAD_DIM
    q, k, v = jnp.split(u, [GROUP_W, GROUP_W + kvw], axis=-1)
    q = rms_norm(q.reshape(b, L, GQA_HEADS, GQA_HEAD_DIM)) * qn_g
    k = rms_norm(k.reshape(b, L, GQA_KV_HEADS, GQA_HEAD_DIM)) * kn_g
    v = v.reshape(b, L, GQA_KV_HEADS, GQA_HEAD_DIM)
    new_ctx = (k, v)
    if rope is not None:
        q = apply_rope(q, rope)
        k = apply_rope(k, rope)
    if ctx is not None:
        k = jnp.concatenate([ctx[0].astype(k.dtype), k], axis=1)
        v = jnp.concatenate([ctx[1].astype(v.dtype), v], axis=1)
    q = q.reshape(b, L, GQA_KV_HEADS, GQA_GROUP, GQA_HEAD_DIM)
    scale = GQA_HEAD_DIM ** -0.5

    def block(qb):
        s = jnp.einsum('bqhgd,bkhd->bhgqk', qb, k).astype(jnp.float32) * scale
        p = jax.nn.softmax(s, axis=-1)
        return jnp.einsum('bhgqk,bkhd->bqhgd', p.astype(v.dtype), v)

    o = sweep_q_blocks(block, q)
    return o.reshape(b, L, GROUP_W), new_ctx


def trunk_layer(x, mod, p, l, rope, ctx):
    b = x.shape[0]
    sh1, sc1, g1, sh2, sc2, g2 = jnp.split(mod, 6, axis=-1)
    h = (rms_norm(x) * p['norm1'][l]) * (1.0 + sc1) + sh1
    u = h @ p['w_in'][l]
    u_a, u_b, u_c, u_d = jnp.split(u, [A_COLS, A_COLS + B_COLS, A_COLS + B_COLS + C_COLS], axis=-1)
    if ctx is None:
        ssm0 = jnp.zeros((b, 2, SSM_HEADS, SSM_HEAD_DIM, SSM_STATE), x.dtype)
        mst0 = (jnp.zeros((b, 2, MLSTM_HEADS, MLSTM_HEAD_DIM, MLSTM_HEAD_DIM), x.dtype),
                jnp.zeros((b, 2, MLSTM_HEADS, MLSTM_HEAD_DIM), x.dtype),
                jnp.zeros((b, 2, MLSTM_HEADS), x.dtype))
        diff_ctx = None
        gqa_ctx = None
    else:
        ssm0 = ctx['ssm']
        mst0 = (ctx['mlstm_c'], ctx['mlstm_n'], ctx['mlstm_m'])
        diff_ctx = (ctx['diff_k'], ctx['diff_v'])
        gqa_ctx = (ctx['gqa_k'], ctx['gqa_v'])
    lam_init = 0.8 - 0.6 * math.exp(-0.3 * l)
    y_a, ssm_t = mixer_ssd(u_a, p['conv_ssd_w'][l], p['conv_ssd_b'][l], p['ssd_a_log'][l], p['ssd_dt_bias'][l], p['ssd_d'][l], p['ssd_norm'][l], ssm0)
    y_b, diff_kv = mixer_diff(u_b, p['diff_lq1'][l], p['diff_lk1'][l], p['diff_lq2'][l], p['diff_lk2'][l], lam_init, rope, diff_ctx)
    y_c, mst_t = mixer_mlstm(u_c, p['conv_mlstm_w'][l], p['conv_mlstm_b'][l], p['mlstm_gate_b'][l], p['mlstm_norm'][l], mst0)
    y_d, gqa_kv = mixer_gqa(u_d, p['gqa_q_norm'][l], p['gqa_k_norm'][l], rope, gqa_ctx)
    y = jnp.concatenate([y_a, y_b, y_c, y_d], axis=-1) @ p['w_out'][l]
    x = x + g1 * y
    h = (rms_norm(x) * p['norm2'][l]) * (1.0 + sc2) + sh2
    gate, up = jnp.split(h @ p['w_ffn_in'][l], 2, axis=-1)
    x = x + g2 * ((jax.nn.silu(gate) * up) @ p['w_ffn_out'][l])
    return x, (diff_kv[0], diff_kv[1], gqa_kv[0], gqa_kv[1], ssm_t, mst_t[0], mst_t[1], mst_t[2])


def setup_inputs(seed: int = 0) -> dict:
    key = jax.random.key(seed)
    keys = iter(jax.random.split(key, 48))

    def nrm(shape, scale=1.0):
        return scale * jax.random.normal(next(keys), shape, jnp.float32)

    def unif(shape, lo, hi):
        return jax.random.uniform(next(keys), shape, jnp.float32, lo, hi)

    def gain(shape):
        return 1.0 + nrm(shape, 0.05)

    dt0 = jnp.exp(unif((DEPTH, 2, SSM_HEADS), math.log(1e-3), math.log(1e-1)))
    gate_b = jnp.stack([nrm((DEPTH, 2, MLSTM_HEADS), 0.1), unif((DEPTH, 2, MLSTM_HEADS), 3.0, 6.0)], axis=2)
    return {
        'x_prompt': nrm((BATCH, SEQ, D_MODEL)),
        'x_sample': nrm((DEC_BATCH, DEC_SEQ, D_MODEL)),
        'c': nrm((DEC_BATCH, D_MODEL)),
        'cache_diff_k': nrm((DEC_BATCH, DEPTH, PAST_LEN, DIFF_HEADS, 2, DIFF_QK_DIM)),
        'cache_diff_v': nrm((DEC_BATCH, DEPTH, PAST_LEN, DIFF_HEADS, DIFF_V_DIM)),
        'cache_gqa_k': nrm((DEC_BATCH, DEPTH, PAST_LEN, GQA_KV_HEADS, GQA_HEAD_DIM)),
        'cache_gqa_v': nrm((DEC_BATCH, DEPTH, PAST_LEN, GQA_KV_HEADS, GQA_HEAD_DIM)),
        'state_ssm': nrm((DEC_BATCH, DEPTH, 2, SSM_HEADS, SSM_HEAD_DIM, SSM_STATE), 0.5),
        'state_mlstm_c': nrm((DEC_BATCH, DEPTH, 2, MLSTM_HEADS, MLSTM_HEAD_DIM, MLSTM_HEAD_DIM), 0.3),
        'state_mlstm_n': nrm((DEC_BATCH, DEPTH, 2, MLSTM_HEADS, MLSTM_HEAD_DIM), 0.3),
        'state_mlstm_m': nrm((DEC_BATCH, DEPTH, 2, MLSTM_HEADS)),
        'c_ctx': nrm((D_MODEL,)),
        'w_ada': nrm((DEPTH, D_MODEL, 6 * D_MODEL), 0.5 * D_MODEL ** -0.5),
        'b_ada': nrm((DEPTH, 6 * D_MODEL), 0.02),
        'norm1': gain((DEPTH, D_MODEL)),
        'norm2': gain((DEPTH, D_MODEL)),
        'w_in': nrm((DEPTH, D_MODEL, IN_COLS), D_MODEL ** -0.5),
        'w_out': nrm((DEPTH, D_MIX, D_MODEL), D_MIX ** -0.5),
        'conv_ssd_w': nrm((DEPTH, D_CONV, SSM_CONV_CH), D_CONV ** -0.5),
        'conv_ssd_b': nrm((DEPTH, SSM_CONV_CH), 0.02),
        'ssd_a_log': jnp.log(unif((DEPTH, 2, SSM_HEADS), 1.0, 16.0)),
        'ssd_dt_bias': dt0 + jnp.log(-jnp.expm1(-dt0)),
        'ssd_d': gain((DEPTH, SSM_HEADS)),
        'ssd_norm': gain((DEPTH, GROUP_W)),
        'diff_lq1': nrm((DEPTH, DIFF_QK_DIM), 0.1),
        'diff_lk1': nrm((DEPTH, DIFF_QK_DIM), 0.1),
        'diff_lq2': nrm((DEPTH, DIFF_QK_DIM), 0.1),
        'diff_lk2': nrm((DEPTH, DIFF_QK_DIM), 0.1),
        'conv_mlstm_w': nrm((DEPTH, D_CONV, 2 * GROUP_W), D_CONV ** -0.5),
        'conv_mlstm_b': nrm((DEPTH, 2 * GROUP_W), 0.02),
        'mlstm_gate_b': gate_b,
        'mlstm_norm': gain((DEPTH, GROUP_W)),
        'gqa_q_norm': gain((DEPTH, GQA_HEAD_DIM)),
        'gqa_k_norm': gain((DEPTH, GQA_HEAD_DIM)),
        'w_ffn_in': nrm((DEPTH, D_MODEL, 2 * D_FF), D_MODEL ** -0.5),
        'w_ffn_out': nrm((DEPTH, D_FF, D_MODEL), D_FF ** -0.5),
        'norm_f': gain((D_MODEL,)),
    }


def reference(x_prompt, x_sample, c, cache_diff_k, cache_diff_v, cache_gqa_k, cache_gqa_v, state_ssm, state_mlstm_c, state_mlstm_n, state_mlstm_m, c_ctx, w_ada, b_ada, norm1, norm2, w_in, w_out, conv_ssd_w, conv_ssd_b, ssd_a_log, ssd_dt_bias, ssd_d, ssd_norm, diff_lq1, diff_lk1, diff_lq2, diff_lk2, conv_mlstm_w, conv_mlstm_b, mlstm_gate_b, mlstm_norm, gqa_q_norm, gqa_k_norm, w_ffn_in, w_ffn_out, norm_f):
    p = dict(norm1=norm1, norm2=norm2, w_in=w_in, w_out=w_out, conv_ssd_w=conv_ssd_w, conv_ssd_b=conv_ssd_b,
             ssd_a_log=ssd_a_log, ssd_dt_bias=ssd_dt_bias, ssd_d=ssd_d, ssd_norm=ssd_norm,
             diff_lq1=diff_lq1, diff_lk1=diff_lk1, diff_lq2=diff_lq2, diff_lk2=diff_lk2,
             conv_mlstm_w=conv_mlstm_w, conv_mlstm_b=conv_mlstm_b, mlstm_gate_b=mlstm_gate_b, mlstm_norm=mlstm_norm,
             gqa_q_norm=gqa_q_norm, gqa_k_norm=gqa_k_norm, w_ffn_in=w_ffn_in, w_ffn_out=w_ffn_out)

    x = x_prompt
    per_layer = []
    for l in range(DEPTH):
        mod = (jax.nn.silu(c_ctx) @ w_ada[l] + b_ada[l])[None, None, :]
        x, ctx_l = trunk_layer(x, mod, p, l, None, None)
        per_layer.append(ctx_l)
    y_prompt = rms_norm(x) * norm_f
    new_diff_k = jnp.stack([s[0] for s in per_layer], axis=1)
    new_diff_v = jnp.stack([s[1] for s in per_layer], axis=1)
    new_gqa_k = jnp.stack([s[2] for s in per_layer], axis=1)
    new_gqa_v = jnp.stack([s[3] for s in per_layer], axis=1)
    new_ssm = jnp.stack([s[4] for s in per_layer], axis=1)
    new_mlstm_c = jnp.stack([s[5] for s in per_layer], axis=1)
    new_mlstm_n = jnp.stack([s[6] for s in per_layer], axis=1)
    new_mlstm_m = jnp.stack([s[7] for s in per_layer], axis=1)

    rope = grid_rope(x_sample.shape[1], ROPE_DIM)
    x = x_sample
    for l in range(DEPTH):
        mod = (jax.nn.silu(c) @ w_ada[l] + b_ada[l])[:, None, :]
        ctx = dict(diff_k=cache_diff_k[:, l], diff_v=cache_diff_v[:, l], gqa_k=cache_gqa_k[:, l], gqa_v=cache_gqa_v[:, l],
                   ssm=state_ssm[:, l], mlstm_c=state_mlstm_c[:, l], mlstm_n=state_mlstm_n[:, l], mlstm_m=state_mlstm_m[:, l])
        x, _ = trunk_layer(x, mod, p, l, rope, ctx)
    y_sample = rms_norm(x) * norm_f
    return (y_prompt, y_sample, new_diff_k, new_diff_v, new_gqa_k, new_gqa_v, new_ssm, new_mlstm_c, new_mlstm_n, new_mlstm_m)
```

```python
import functools
import math

import numpy as np
import jax
import jax.numpy as jnp
from jax import lax
from jax.experimental import pallas as pl
from jax.experimental.pallas import tpu as pltpu

F32 = jnp.float32
BF16 = jnp.bfloat16

D_MODEL = 1024
BATCH = 16
SEQ = 256
DEPTH = 4
DEC_BATCH = 2
DEC_SEQ = 2048
PAST_LEN = 256
GRID_W = 64
GROUP_W = 512
D_MIX = 2048
CHUNK = 128
Q_BLOCK = 128
D_CONV = 5
ROPE_THETA = 10000.0
EPS = 1e-6
SSM_HEADS = 8
SSM_HEAD_DIM = 64
SSM_STATE = 64
SSM_CONV_CH = 768
DIFF_HEADS = 4
MLSTM_HEADS = 4
MLSTM_HEAD_DIM = 128
GQA_KV_HEADS = 2
GQA_HEAD_DIM = 64
IN_COLS = 5664
D_FF = 2816

N_CTX = BATCH * SEQ
N_DEC = DEC_BATCH * DEC_SEQ
N_TOK = N_CTX + N_DEC

U_COLS = 6144
C_QK, C_V, C_O = 0, 1024, 1536
A_Z = 2048
B_Q, B_K, B_V = 2560, 3072, 3584
D_Q = 4096
A_XBC = 4608
D_KV = 5376
A_DT = 5632
C_G = 5760

VMEM_LIMIT_BYTES = 56 * 1024 * 1024


def _in_col_permutation():
    idx = np.full((U_COLS,), IN_COLS, np.int32)
    a0, b0, c0, d0 = 0, 1296, 2832, 4896

    def put(dst, src, n):
        idx[dst:dst + n] = np.arange(src, src + n)

    put(A_Z, a0, 512)
    put(A_XBC, a0 + 512, 768)
    put(A_DT, a0 + 1280, 16)
    put(B_Q, b0, 512)
    put(B_K, b0 + 512, 512)
    put(B_V, b0 + 1024, 512)
    put(C_QK, c0, 1024)
    put(C_V, c0 + 1024, 512)
    put(C_O, c0 + 1536, 512)
    put(C_G, c0 + 2048, 16)
    put(D_Q, d0, 512)
    for kv in range(GQA_KV_HEADS):
        put(D_KV + kv * 128, d0 + 512 + kv * 64, 64)
        put(D_KV + kv * 128 + 64, d0 + 640 + kv * 64, 64)
    return idx


_IN_PERM = _in_col_permutation()


def _cparams(sem):
    return pltpu.CompilerParams(dimension_semantics=sem, vmem_limit_bytes=VMEM_LIMIT_BYTES)


def _bdot(a, b):
    return jnp.dot(a.astype(BF16), b.astype(BF16), preferred_element_type=F32)


def _bdot_t(a, b):
    return lax.dot_general(a.astype(BF16), b.astype(BF16), (((1,), (1,)), ((), ())),
                           preferred_element_type=F32)


def _split3(v):
    hi = v.astype(BF16)
    r = v - hi.astype(F32)
    mid = r.astype(BF16)
    lo = (r - mid.astype(F32)).astype(BF16)
    return hi, mid, lo


def _xdot_l(e, v):
    hi, mid, lo = _split3(v)
    return (jnp.dot(e, hi, preferred_element_type=F32) + jnp.dot(e, mid, preferred_element_type=F32)
            + jnp.dot(e, lo, preferred_element_type=F32))


def _xdot_r(v, e):
    hi, mid, lo = _split3(v)
    return (jnp.dot(hi, e, preferred_element_type=F32) + jnp.dot(mid, e, preferred_element_type=F32)
            + jnp.dot(lo, e, preferred_element_type=F32))


def _sigmoid(x):
    return 1.0 / (1.0 + jnp.exp(-x))


def _silu(x):
    return x * _sigmoid(x)


def _softplus(x):
    return jnp.maximum(x, 0.0) + jnp.log1p(jnp.exp(-jnp.abs(x)))


def _log_sigmoid(x):
    return jnp.minimum(x, 0.0) - jnp.log1p(jnp.exp(-jnp.abs(x)))


def _rms(x):
    return x * lax.rsqrt(jnp.mean(x * x, axis=-1, keepdims=True) + EPS)


def _iota(shape, dim):
    return lax.broadcasted_iota(jnp.int32, shape, dim)


def _rope(x, cos, sin_signed):
    lane = _iota(x.shape, 1)
    first = (lane % 32) < 16
    xr = jnp.where(first, pltpu.roll(x, 112, 1), pltpu.roll(x, 16, 1))
    return x * cos + xr * sin_signed


def _mod_index(tile_rows):
    n_ctx_tiles = N_CTX // tile_rows
    per_sample = DEC_SEQ // tile_rows

    def f(i):
        return jnp.where(i < n_ctx_tiles, 0, 1 + (i - n_ctx_tiles) // per_sample)

    return f


def _mod_kernel(c_ref, w_ref, b_ref, o_ref):
    c = c_ref[...]
    s = _silu(c)
    s_hi = s.astype(BF16)
    s_lo = (s - s_hi.astype(F32)).astype(BF16)
    w = w_ref[...]
    w_hi = w.astype(BF16)
    w_lo = (w - w_hi.astype(F32)).astype(BF16)
    acc = jnp.dot(s_hi, w_hi, preferred_element_type=F32)
    acc = acc + jnp.dot(s_hi, w_lo, preferred_element_type=F32)
    acc = acc + jnp.dot(s_lo, w_hi, preferred_element_type=F32)
    o_ref[...] = acc + b_ref[...]


def _modulation(cvec, w_ada, b_ada):
    tn = 1536
    n = 6 * D_MODEL
    return pl.pallas_call(
        _mod_kernel,
        grid=(DEPTH, n // tn),
        in_specs=[
            pl.BlockSpec((8, D_MODEL), lambda l, j: (0, 0)),
            pl.BlockSpec((None, D_MODEL, tn), lambda l, j: (l, 0, j)),
            pl.BlockSpec((None, 1, tn), lambda l, j: (l, 0, j)),
        ],
        out_specs=pl.BlockSpec((None, 8, tn), lambda l, j: (l, 0, j)),
        out_shape=jax.ShapeDtypeStruct((DEPTH, 8, n), F32),
        compiler_params=_cparams(("arbitrary", "arbitrary")),
        name="adaln_mod",
    )(cvec, w_ada, b_ada.reshape(DEPTH, 1, n))


ROW_SUB = 256


def _norm_mod_to(h_ref, x_ref, gain_ref, shift, scale, rows):
    def body(s, carry):
        r = pl.multiple_of(s * ROW_SUB, ROW_SUB)
        x = x_ref[pl.ds(r, ROW_SUB), :]
        h = (_rms(x) * gain_ref[...]) * (1.0 + scale) + shift
        h_ref[pl.ds(r, ROW_SUB), :] = h.astype(BF16)
        return carry

    lax.fori_loop(0, rows // ROW_SUB, body, 0)


def _inproj_kernel(x_ref, mod_ref, gain_ref, w_ref, u_ref, h_ref, *, tm):
    @pl.when(pl.program_id(1) == 0)
    def _():
        shift = mod_ref[:, 0:D_MODEL]
        scale = mod_ref[:, D_MODEL:2 * D_MODEL]
        _norm_mod_to(h_ref, x_ref, gain_ref, shift, scale, tm)

    u_ref[...] = jnp.dot(h_ref[...], w_ref[...], preferred_element_type=F32)


def _inproj(x, mod_l, gain, w):
    tm, tn = 1024, 1536
    midx = _mod_index(tm)
    return pl.pallas_call(
        functools.partial(_inproj_kernel, tm=tm),
        grid=(N_TOK // tm, U_COLS // tn),
        in_specs=[
            pl.BlockSpec((tm, D_MODEL), lambda i, j: (i, 0)),
            pl.BlockSpec((None, 1, 6 * D_MODEL), lambda i, j: (midx(i), 0, 0)),
            pl.BlockSpec((1, D_MODEL), lambda i, j: (0, 0)),
            pl.BlockSpec((D_MODEL, tn), lambda i, j: (0, j)),
        ],
        out_specs=pl.BlockSpec((tm, tn), lambda i, j: (i, j)),
        out_shape=jax.ShapeDtypeStruct((N_TOK, U_COLS), F32),
        scratch_shapes=[pltpu.VMEM((tm, D_MODEL), BF16)],
        compiler_params=_cparams(("arbitrary", "arbitrary")),
        name="in_proj",
    )(x, mod_l, gain, w)


def _outproj_kernel(*refs, n_ctx_tiles):
    ctx_refs = refs[0:4]
    dec_refs = refs[4:8]
    x_ref, mod_ref, w_ref, o_ref = refs[8:12]
    i = pl.program_id(0)

    def compute(ys):
        acc = jnp.dot(ys[0][...], w_ref[0:512, :], preferred_element_type=F32)
        for g in range(1, 4):
            acc = acc + jnp.dot(ys[g][...], w_ref[g * 512:(g + 1) * 512, :], preferred_element_type=F32)
        gate = mod_ref[:, 2 * D_MODEL:3 * D_MODEL]
        o_ref[...] = x_ref[...] + gate * acc

    @pl.when(i < n_ctx_tiles)
    def _():
        compute(ctx_refs)

    @pl.when(i >= n_ctx_tiles)
    def _():
        compute(dec_refs)


def _outproj(ys_ctx, ys_dec, x, mod_l, w):
    tm = 512
    nct = N_CTX // tm
    midx = _mod_index(tm)
    ctx_spec = pl.BlockSpec((tm, GROUP_W), lambda i: (jnp.minimum(i, nct - 1), 0))
    dec_spec = pl.BlockSpec((tm, GROUP_W), lambda i: (jnp.maximum(i - nct, 0), 0))
    return pl.pallas_call(
        functools.partial(_outproj_kernel, n_ctx_tiles=nct),
        grid=(N_TOK // tm,),
        in_specs=[ctx_spec] * 4 + [dec_spec] * 4 + [
            pl.BlockSpec((tm, D_MODEL), lambda i: (i, 0)),
            pl.BlockSpec((None, 1, 6 * D_MODEL), lambda i: (midx(i), 0, 0)),
            pl.BlockSpec((D_MIX, D_MODEL), lambda i: (0, 0)),
        ],
        out_specs=pl.BlockSpec((tm, D_MODEL), lambda i: (i, 0)),
        out_shape=jax.ShapeDtypeStruct((N_TOK, D_MODEL), F32),
        compiler_params=_cparams(("arbitrary",)),
        name="out_proj",
    )(*ys_ctx, *ys_dec, x, mod_l, w)


def _ffn_kernel(x_ref, mod_ref, gain_ref, wg_ref, wu_ref, wo_ref, o_ref, h_ref, acc_ref, *, tm, n_f):
    j = pl.program_id(1)

    @pl.when(j == 0)
    def _():
        shift = mod_ref[:, 3 * D_MODEL:4 * D_MODEL]
        scale = mod_ref[:, 4 * D_MODEL:5 * D_MODEL]
        _norm_mod_to(h_ref, x_ref, gain_ref, shift, scale, tm)

    h = h_ref[...]
    gate = jnp.dot(h, wg_ref[...], preferred_element_type=F32)
    up = jnp.dot(h, wu_ref[...], preferred_element_type=F32)
    act = (_silu(gate) * up).astype(BF16)
    part = jnp.dot(act, wo_ref[...], preferred_element_type=F32)

    @pl.when(j == 0)
    def _():
        acc_ref[...] = part

    @pl.when(j > 0)
    def _():
        acc_ref[...] += part

    @pl.when(j == n_f - 1)
    def _():
        g2 = mod_ref[:, 5 * D_MODEL:6 * D_MODEL]
        o_ref[...] = x_ref[...] + g2 * acc_ref[...]


def _ffn(x, mod_l, gain, w_in, w_out):
    tm, tf = 1024, 256
    n_f = D_FF // tf
    midx = _mod_index(tm)
    return pl.pallas_call(
        functools.partial(_ffn_kernel, tm=tm, n_f=n_f),
        grid=(N_TOK // tm, n_f),
        in_specs=[
            pl.BlockSpec((tm, D_MODEL), lambda i, j: (i, 0)),
            pl.BlockSpec((None, 1, 6 * D_MODEL), lambda i, j: (midx(i), 0, 0)),
            pl.BlockSpec((1, D_MODEL), lambda i, j: (0, 0)),
            pl.BlockSpec((D_MODEL, tf), lambda i, j: (0, j)),
            pl.BlockSpec((D_MODEL, tf), lambda i, j: (0, j + n_f)),
            pl.BlockSpec((tf, D_MODEL), lambda i, j: (j, 0)),
        ],
        out_specs=pl.BlockSpec((tm, D_MODEL), lambda i, j: (i, 0)),
        out_shape=jax.ShapeDtypeStruct((N_TOK, D_MODEL), F32),
        scratch_shapes=[pltpu.VMEM((tm, D_MODEL), BF16), pltpu.VMEM((tm, D_MODEL), F32)],
        compiler_params=_cparams(("arbitrary", "arbitrary")),
        name="ffn",
    )(x, mod_l, gain, w_in, w_in, w_out)


def _final_norm_kernel(x_ref, g_ref, o_ref):
    o_ref[...] = _rms(x_ref[...]) * g_ref[...]


def _final_norm(x, gain, row0, rows):
    tm = 512
    off = row0 // tm
    return pl.pallas_call(
        _final_norm_kernel,
        grid=(rows // tm,),
        in_specs=[pl.BlockSpec((tm, D_MODEL), lambda i: (i + off, 0)),
                  pl.BlockSpec((1, D_MODEL), lambda i: (0, 0))],
        out_specs=pl.BlockSpec((tm, D_MODEL), lambda i: (i, 0)),
        out_shape=jax.ShapeDtypeStruct((rows, D_MODEL), F32),
        compiler_params=_cparams(("arbitrary",)),
        name="final_norm",
    )(x, gain)


def _causal_masks():
    i = _iota((CHUNK, CHUNK), 0)
    j = _iota((CHUNK, CHUNK), 1)
    return j <= i, j >= i


def _conv_silu_chunk(pad_ref, w_ref, b_ref, r):
    win = pad_ref[pl.ds(r, CHUNK + 16), :]
    acc = b_ref[...] + w_ref[0:1, :] * win[6:6 + CHUNK, :]
    for tap in range(1, D_CONV):
        acc = acc + w_ref[tap:tap + 1, :] * win[6 + tap:6 + tap + CHUNK, :]
    return _silu(acc)


def _fill_padded(pad_ref, src_ref, seq, width):
    zeros = jnp.zeros((8, width), F32)
    pad_ref[0:8, :] = zeros
    pad_ref[seq + 8:seq + 16, :] = zeros

    def body(c, carry):
        r = pl.multiple_of(c * CHUNK, CHUNK)
        pad_ref[pl.ds(r + 8, CHUNK), :] = src_ref[pl.ds(r, CHUNK), :]
        return carry

    lax.fori_loop(0, seq // CHUNK, body, 0)


def _ssd_kernel(*refs, seq, has_state):
    (z_ref, xbc_ref, dt_ref, cw_ref, cb_ref, alog_ref, dtb_ref, dsk_ref, ng_ref) = refs[0:9]
    if has_state:
        h0_ref, y_ref = refs[9:11]
        hfin_ref = None
        pad_ref, xc_ref, yacc_ref, st_ref = refs[11:15]
    else:
        h0_ref = None
        y_ref, hfin_ref = refs[9:11]
        pad_ref, xc_ref, yacc_ref, st_ref = refs[11:15]
    nc = seq // CHUNK

    _fill_padded(pad_ref, xbc_ref, seq, SSM_CONV_CH)

    def conv_body(c, carry):
        r = pl.multiple_of(c * CHUNK, CHUNK)
        xc_ref[pl.ds(r, CHUNK), :] = _conv_silu_chunk(pad_ref, cw_ref, cb_ref, r)
        return carry

    lax.fori_loop(0, nc, conv_body, 0)

    if has_state:
        st_ref[...] = h0_ref[...]
    else:
        st_ref[...] = jnp.zeros(st_ref.shape, F32)

    mask_f, mask_b = _causal_masks()
    a_neg = -jnp.exp(alog_ref[...])

    def chunk(d, r):
        mask = mask_f if d == 0 else mask_b
        tmat = mask.astype(BF16)
        xc = xc_ref[pl.ds(r, CHUNK), :]
        dt = _softplus(dt_ref[pl.ds(r, CHUNK), :] + dtb_ref[...])
        ld = dt * a_neg
        cum_c = _xdot_l(tmat, ld)
        cum_r = cum_c.T
        last = cum_c[CHUNK - 1:CHUNK, :] if d == 0 else cum_c[0:1, :]
        e_last = jnp.exp(last)
        e_cum = jnp.exp(cum_c)
        w_end = jnp.exp(last - cum_c)
        gmats = []
        for g in range(2):
            bm = xc[:, 512 + g * 64:512 + (g + 1) * 64]
            cm = xc[:, 640 + g * 64:640 + (g + 1) * 64]
            gmats.append((bm, cm, _bdot_t(cm, bm)))
        ys = []
        for h in range(SSM_HEADS):
            col = d * SSM_HEADS + h
            bm, cm, gmat = gmats[h // 4]
            cc = cum_c[:, col:col + 1]
            cr = cum_r[col:col + 1, :]
            decay = jnp.exp(jnp.where(mask, cc - cr, -jnp.inf))
            xdt = xc[:, h * 64:(h + 1) * 64] * dt[:, col:col + 1]
            hprev = st_ref[d, h]
            y = _bdot(gmat * decay, xdt) + _bdot_t(cm, hprev) * e_cum[:, col:col + 1]
            cs = _bdot((xdt * w_end[:, col:col + 1]).T, bm)
            st_ref[d, h] = e_last[:, col:col + 1] * hprev + cs
            ys.append(y)
        return jnp.concatenate(ys, axis=1), xc[:, 0:GROUP_W]

    def fwd_body(c, carry):
        r = pl.multiple_of(c * CHUNK, CHUNK)
        y, x = chunk(0, r)
        yacc_ref[pl.ds(r, CHUNK), :] = y + dsk_ref[...] * x
        return carry

    lax.fori_loop(0, nc, fwd_body, 0)

    def bwd_body(s, carry):
        r = pl.multiple_of((nc - 1 - s) * CHUNK, CHUNK)
        y, _ = chunk(1, r)
        y = (yacc_ref[pl.ds(r, CHUNK), :] + y) * _silu(z_ref[pl.ds(r, CHUNK), :])
        y_ref[pl.ds(r, CHUNK), :] = (_rms(y) * ng_ref[...]).astype(BF16)
        return carry

    lax.fori_loop(0, nc, bwd_body, 0)

    if hfin_ref is not None:
        hfin_ref[...] = st_ref[...]


def _mixer_ssd(u, row0, nb, seq, conv_w, conv_b, a_log, dt_bias, d_skip, norm_g, h0):
    has_state = h0 is not None
    roff = row0 // seq
    vec = lambda n: pl.BlockSpec((1, n), lambda b: (0, 0))
    in_specs = [
        pl.BlockSpec((seq, GROUP_W), lambda b: (b + roff, A_Z // GROUP_W)),
        pl.BlockSpec((seq, SSM_CONV_CH), lambda b: (b + roff, A_XBC // SSM_CONV_CH)),
        pl.BlockSpec((seq, 128), lambda b: (b + roff, A_DT // 128)),
        pl.BlockSpec((8, SSM_CONV_CH), lambda b: (0, 0)),
        vec(SSM_CONV_CH), vec(128), vec(128), vec(GROUP_W), vec(GROUP_W),
    ]
    args = [u, u, u, conv_w, conv_b, a_log, dt_bias, d_skip, norm_g]
    st_block = (None, 2, SSM_HEADS, SSM_HEAD_DIM, SSM_STATE)
    y_spec = pl.BlockSpec((seq, GROUP_W), lambda b: (b, 0))
    y_shape = jax.ShapeDtypeStruct((nb * seq, GROUP_W), BF16)
    if has_state:
        in_specs.append(pl.BlockSpec(st_block, lambda b: (b, 0, 0, 0, 0)))
        args.append(h0)
        out_specs, out_shape = y_spec, y_shape
    else:
        out_specs = [y_spec, pl.BlockSpec(st_block, lambda b: (b, 0, 0, 0, 0))]
        out_shape = [y_shape, jax.ShapeDtypeStruct((nb, 2, SSM_HEADS, SSM_HEAD_DIM, SSM_STATE), F32)]
    return pl.pallas_call(
        functools.partial(_ssd_kernel, seq=seq, has_state=has_state),
        grid=(nb,),
        in_specs=in_specs,
        out_specs=out_specs,
        out_shape=out_shape,
        scratch_shapes=[
            pltpu.VMEM((seq + 16, SSM_CONV_CH), F32),
            pltpu.VMEM((seq, SSM_CONV_CH), F32),
            pltpu.VMEM((seq, GROUP_W), F32),
            pltpu.VMEM((2, SSM_HEADS, SSM_HEAD_DIM, SSM_STATE), F32),
        ],
        compiler_params=_cparams(("arbitrary",)),
        name="mixer_ssd_dec" if has_state else "mixer_ssd_ctx",
    )(*args)


def _mlstm_kernel(*refs, seq, has_state):
    (q_ref, k_ref, v_ref, o_ref, g_ref, cwq_ref, cwk_ref, cbq_ref, cbk_ref, gb_ref, ng_ref) = refs[0:11]
    if has_state:
        c0_ref, n0_ref, m0_ref, y_ref = refs[11:15]
        outs = None
        scratch = refs[15:]
    else:
        y_ref = refs[11]
        outs = refs[12:15]
        scratch = refs[15:]
    qpad_ref, kpad_ref, qc_ref, kc_ref, hacc_ref, cst_ref, nst_ref, mst_ref = scratch
    nc = seq // CHUNK
    head = pl.program_id(1)

    _fill_padded(qpad_ref, q_ref, seq, MLSTM_HEAD_DIM)
    _fill_padded(kpad_ref, k_ref, seq, MLSTM_HEAD_DIM)

    def conv_body(c, carry):
        r = pl.multiple_of(c * CHUNK, CHUNK)
        qc_ref[pl.ds(r, CHUNK), :] = _conv_silu_chunk(qpad_ref, cwq_ref, cbq_ref, r) * (MLSTM_HEAD_DIM ** -0.5)
        kc_ref[pl.ds(r, CHUNK), :] = _conv_silu_chunk(kpad_ref, cwk_ref, cbk_ref, r)
        return carry

    lax.fori_loop(0, nc, conv_body, 0)

    if has_state:
        cst_ref[...] = c0_ref[...]
        nst_ref[...] = n0_ref[...]
        mst_ref[...] = m0_ref[...]
    else:
        cst_ref[...] = jnp.zeros(cst_ref.shape, F32)
        nst_ref[...] = jnp.zeros(nst_ref.shape, F32)
        mst_ref[...] = jnp.zeros(mst_ref.shape, F32)

    mask_f, mask_b = _causal_masks()
    row_id = _iota((CHUNK, CHUNK), 0)

    def chunk(d, r):
        mask = mask_f if d == 0 else mask_b
        tmat = mask.astype(BF16)
        sel_i = (row_id == d * 8 + head).astype(BF16)
        sel_f = (row_id == d * 8 + 4 + head).astype(BF16)
        gates = g_ref[pl.ds(r, CHUNK), :] + gb_ref[...]
        li_c = _xdot_r(gates, sel_i)
        lf_c = _log_sigmoid(_xdot_r(gates, sel_f))
        cum_c = _xdot_l(tmat, lf_c)
        cum_r = cum_c.T
        li_r = li_c.T
        last = cum_c[CHUNK - 1:CHUNK, :] if d == 0 else cum_c[0:1, :]
        dmat = jnp.where(mask, cum_c - cum_r + li_r, -jnp.inf)
        g_end_r = last - cum_r[0:1, :] + li_r[0:1, :]
        m_loc = jnp.max(g_end_r, axis=1, keepdims=True)
        w_end_c = jnp.exp(last - cum_c + li_c - m_loc)
        q = qc_ref[pl.ds(r, CHUNK), :]
        k = kc_ref[pl.ds(r, CHUNK), :]
        v = v_ref[pl.ds(r, CHUNK), :]
        kw = k * w_end_c
        c_loc = _bdot(kw.T, v)
        n_loc = jnp.sum(kw, axis=0, keepdims=True)
        c_p = cst_ref[d]
        n_p = nst_ref[d]
        m_p = mst_ref[d]
        inter = cum_c[:, 0:1] + m_p[:, 0:1]
        m_t = jnp.maximum(inter, jnp.max(dmat, axis=1, keepdims=True))
        w_intra = jnp.exp(dmat - m_t)
        w_inter = jnp.exp(inter - m_t)
        s = _bdot_t(q, k) * w_intra
        num = _bdot(s, v) + w_inter * _bdot(q, c_p)
        den = jnp.sum(s, axis=1, keepdims=True) + w_inter * jnp.sum(q * n_p, axis=1, keepdims=True)
        hh = num / jnp.maximum(jnp.abs(den), jnp.exp(-m_t))
        m_new = jnp.maximum(last + m_p, m_loc)
        s_p = jnp.exp(last + m_p - m_new)
        s_l = jnp.exp(m_loc - m_new)
        cst_ref[d] = s_p[:, 0:1] * c_p + s_l[:, 0:1] * c_loc
        nst_ref[d] = s_p * n_p + s_l * n_loc
        mst_ref[d] = m_new
        return hh

    def fwd_body(c, carry):
        r = pl.multiple_of(c * CHUNK, CHUNK)
        hacc_ref[pl.ds(r, CHUNK), :] = chunk(0, r)
        return carry

    lax.fori_loop(0, nc, fwd_body, 0)

    def bwd_body(s, carry):
        r = pl.multiple_of((nc - 1 - s) * CHUNK, CHUNK)
        hsum = hacc_ref[pl.ds(r, CHUNK), :] + chunk(1, r)
        y = _sigmoid(o_ref[pl.ds(r, CHUNK), :]) * (_rms(hsum) * ng_ref[...])
        y_ref[pl.ds(r, CHUNK), :] = y.astype(BF16)
        return carry

    lax.fori_loop(0, nc, bwd_body, 0)

    if outs is not None:
        outs[0][...] = cst_ref[...]
        outs[1][...] = nst_ref[...]
        outs[2][...] = mst_ref[...]


def _mixer_mlstm(u, row0, nb, seq, conv_w, conv_b, gate_b, norm_g, state):
    has_state = state is not None
    roff = row0 // seq
    hd = MLSTM_HEAD_DIM
    col = lambda base: (lambda b, h: (b + roff, base // hd + h))
    in_specs = [
        pl.BlockSpec((seq, hd), col(C_QK)),
        pl.BlockSpec((seq, hd), col(C_QK + GROUP_W)),
        pl.BlockSpec((seq, hd), col(C_V)),
        pl.BlockSpec((seq, hd), col(C_O)),
        pl.BlockSpec((seq, 128), lambda b, h: (b + roff, C_G // 128)),
        pl.BlockSpec((8, hd), lambda b, h: (0, h)),
        pl.BlockSpec((8, hd), lambda b, h: (0, MLSTM_HEADS + h)),
        pl.BlockSpec((1, hd), lambda b, h: (0, h)),
        pl.BlockSpec((1, hd), lambda b, h: (0, MLSTM_HEADS + h)),
        pl.BlockSpec((1, 128), lambda b, h: (0, 0)),
        pl.BlockSpec((1, hd), lambda b, h: (0, h)),
    ]
    args = [u, u, u, u, u, conv_w, conv_w, conv_b, conv_b, gate_b, norm_g]
    c_spec = pl.BlockSpec((None, 2, None, hd, hd), lambda b, h: (b, 0, h, 0, 0))
    n_spec = pl.BlockSpec((None, 2, None, 1, hd), lambda b, h: (b, 0, h, 0, 0))
    y_spec = pl.BlockSpec((seq, hd), lambda b, h: (b, h))
    y_shape = jax.ShapeDtypeStruct((nb * seq, GROUP_W), BF16)
    if has_state:
        in_specs += [c_spec, n_spec, n_spec]
        args += list(state)
        out_specs, out_shape = y_spec, y_shape
    else:
        out_specs = [y_spec, c_spec, n_spec, n_spec]
        out_shape = [y_shape,
                     jax.ShapeDtypeStruct((nb, 2, MLSTM_HEADS, hd, hd), F32),
                     jax.ShapeDtypeStruct((nb, 2, MLSTM_HEADS, 1, hd), F32),
                     jax.ShapeDtypeStruct((nb, 2, MLSTM_HEADS, 1, hd), F32)]
    return pl.pallas_call(
        functools.partial(_mlstm_kernel, seq=seq, has_state=has_state),
        grid=(nb, MLSTM_HEADS),
        in_specs=in_specs,
        out_specs=out_specs,
        out_shape=out_shape,
        scratch_shapes=[
            pltpu.VMEM((seq + 16, hd), F32), pltpu.VMEM((seq + 16, hd), F32),
            pltpu.VMEM((seq, hd), F32), pltpu.VMEM((seq, hd), F32),
            pltpu.VMEM((seq, hd), F32),
            pltpu.VMEM((2, hd, hd), F32), pltpu.VMEM((2, 1, hd), F32), pltpu.VMEM((2, 1, hd), F32),
        ],
        compiler_params=_cparams(("arbitrary", "arbitrary")),
        name="mixer_mlstm_dec" if has_state else "mixer_mlstm_ctx",
    )(*args)


KV_SUB = 256


def _diff_kernel(*refs, seq, has_ctx, lam_init):
    if has_ctx:
        (q_ref, k_ref, v_ref, lam_ref, ck_ref, cv_ref, cosq_ref, sinq_ref, cosk_ref, sink_ref,
         o_ref, k1_ref, k2_ref, vf_ref) = refs
        past = PAST_LEN
    else:
        q_ref, k_ref, v_ref, lam_ref, o_ref, k1_ref, k2_ref, vf_ref = refs
        past = 0

    @pl.when(pl.program_id(2) == 0)
    def _():
        if has_ctx:
            ck = ck_ref[...]
            k1_ref[0:past, :] = ck[:, 0:64].astype(BF16)
            k2_ref[0:past, :] = ck[:, 64:128].astype(BF16)
            vf_ref[0:past, :] = cv_ref[...].astype(BF16)

        def body(s, carry):
            r = pl.multiple_of(s * KV_SUB, KV_SUB)
            kk = k_ref[pl.ds(r, KV_SUB), :]
            if has_ctx:
                kk = _rope(kk, cosk_ref[pl.ds(r, KV_SUB), :], sink_ref[pl.ds(r, KV_SUB), :])
            k1_ref[pl.ds(past + r, KV_SUB), :] = kk[:, 0:64].astype(BF16)
            k2_ref[pl.ds(past + r, KV_SUB), :] = kk[:, 64:128].astype(BF16)
            vf_ref[pl.ds(past + r, KV_SUB), :] = v_ref[pl.ds(r, KV_SUB), :].astype(BF16)
            return carry

        lax.fori_loop(0, seq // KV_SUB, body, 0)

    lp = lam_ref[...]
    lam = (jnp.exp(jnp.sum(lp[0:1, :] * lp[1:2, :], axis=1, keepdims=True))
           - jnp.exp(jnp.sum(lp[2:3, :] * lp[3:4, :], axis=1, keepdims=True)) + lam_init)

    q = q_ref[...]
    if has_ctx:
        q = _rope(q, cosq_ref[...], sinq_ref[...])
    q = q * (64 ** -0.5)

    def probs(qm, kref):
        s = _bdot_t(qm, kref[...])
        e = jnp.exp(s - jnp.max(s, axis=1, keepdims=True))
        return e / jnp.sum(e, axis=1, keepdims=True)

    att = probs(q[:, 0:64], k1_ref) - lam * probs(q[:, 64:128], k2_ref)
    o = _bdot(att, vf_ref[...])
    o_ref[...] = (_rms(o) * (1.0 - lam_init)).astype(BF16)


def _mixer_diff(u, row0, nb, seq, lam_rows, lam_init, ctx):
    has_ctx = ctx is not None
    roff = row0 // seq
    qoff = row0 // Q_BLOCK
    nq = seq // Q_BLOCK
    in_specs = [
        pl.BlockSpec((Q_BLOCK, 128), lambda b, h, i: (qoff + b * nq + i, B_Q // 128 + h)),
        pl.BlockSpec((seq, 128), lambda b, h, i: (b + roff, B_K // 128 + h)),
        pl.BlockSpec((seq, 128), lambda b, h, i: (b + roff, B_V // 128 + h)),
        pl.BlockSpec((8, 128), lambda b, h, i: (0, 0)),
    ]
    args = [u, u, u, lam_rows]
    lk = seq
    if has_ctx:
        ck, cv, cos, sin = ctx
        lk = seq + PAST_LEN
        in_specs += [
            pl.BlockSpec((None, PAST_LEN, 128), lambda b, h, i: (b, 0, h)),
            pl.BlockSpec((None, PAST_LEN, 128), lambda b, h, i: (b, 0, h)),
            pl.BlockSpec((Q_BLOCK, 128), lambda b, h, i: (i, 0)),
            pl.BlockSpec((Q_BLOCK, 128), lambda b, h, i: (i, 0)),
            pl.BlockSpec((seq, 128), lambda b, h, i: (0, 0)),
            pl.BlockSpec((seq, 128), lambda b, h, i: (0, 0)),
        ]
        args += [ck, cv, cos, sin, cos, sin]
    return pl.pallas_call(
        functools.partial(_diff_kernel, seq=seq, has_ctx=has_ctx, lam_init=lam_init),
        grid=(nb, DIFF_HEADS, nq),
        in_specs=in_specs,
        out_specs=pl.BlockSpec((Q_BLOCK, 128), lambda b, h, i: (b * nq + i, h)),
        out_shape=jax.ShapeDtypeStruct((nb * seq, GROUP_W), BF16),
        scratch_shapes=[pltpu.VMEM((lk, 64), BF16), pltpu.VMEM((lk, 64), BF16), pltpu.VMEM((lk, 128), BF16)],
        compiler_params=_cparams(("arbitrary", "arbitrary", "arbitrary")),
        name="mixer_diff_dec" if has_ctx else "mixer_diff_ctx",
    )(*args)


def _seg_rms_pair(x, gain):
    low = _iota(x.shape, 1) < 64
    sq = x * x
    ms_lo = jnp.sum(jnp.where(low, sq, 0.0), axis=1, keepdims=True) * (1.0 / 64)
    ms_hi = jnp.sum(jnp.where(low, 0.0, sq), axis=1, keepdims=True) * (1.0 / 64)
    return x * lax.rsqrt(jnp.where(low, ms_lo, ms_hi) + EPS) * gain


def _gqa_kernel(*refs, seq, has_ctx):
    if has_ctx:
        (q_ref, kv_ref, qg_ref, kg_ref, ck_ref, cv_ref, cosq_ref, sinq_ref, cosk_ref, sink_ref,
         o_ref, kf_ref, vf_ref) = refs
        kn_ref = None
        past = PAST_LEN
    else:
        q_ref, kv_ref, qg_ref, kg_ref, o_ref, kn_ref, kf_ref, vf_ref = refs
        past = 0

    @pl.when(pl.program_id(2) == 0)
    def _():
        if has_ctx:
            kf_ref[0:past, :] = ck_ref[...].astype(BF16)
            vf_ref[0:past, :] = cv_ref[...].astype(BF16)

        def body(s, carry):
            r = pl.multiple_of(s * KV_SUB, KV_SUB)
            blk = kv_ref[pl.ds(r, KV_SUB), :]
            kn = _seg_rms_pair(blk, kg_ref[...])
            if has_ctx:
                kn = _rope(kn, cosk_ref[pl.ds(r, KV_SUB), :], sink_ref[pl.ds(r, KV_SUB), :])
            else:
                kn_ref[pl.ds(r, KV_SUB), :] = kn[:, 0:64]
            kf_ref[pl.ds(past + r, KV_SUB), :] = kn[:, 0:64].astype(BF16)
            vf_ref[pl.ds(past + r, KV_SUB), :] = blk[:, 64:128].astype(BF16)
            return carry

        lax.fori_loop(0, seq // KV_SUB, body, 0)

    kf = kf_ref[...]
    vf = vf_ref[...]
    outs = []
    for half in range(2):
        qh = _seg_rms_pair(q_ref[:, half * 128:(half + 1) * 128], qg_ref[...])
        if has_ctx:
            qh = _rope(qh, cosq_ref[...], sinq_ref[...])
        qh = qh * (64 ** -0.5)
        for g in range(2):
            s = _bdot_t(qh[:, g * 64:(g + 1) * 64], kf)
            e = jnp.exp(s - jnp.max(s, axis=1, keepdims=True))
            p = e / jnp.sum(e, axis=1, keepdims=True)
            outs.append(_bdot(p, vf))
    o_ref[...] = jnp.concatenate(outs, axis=1).astype(BF16)


def _mixer_gqa(u, row0, nb, seq, q_gain, k_gain, ctx):
    has_ctx = ctx is not None
    roff = row0 // seq
    qoff = row0 // Q_BLOCK
    nq = seq // Q_BLOCK
    in_specs = [
        pl.BlockSpec((Q_BLOCK, 256), lambda b, h, i: (qoff + b * nq + i, D_Q // 256 + h)),
        pl.BlockSpec((seq, 128), lambda b, h, i: (b + roff, D_KV // 128 + h)),
        pl.BlockSpec((1, 128), lambda b, h, i: (0, 0)),
        pl.BlockSpec((1, 128), lambda b, h, i: (0, 0)),
    ]
    args = [u, u, q_gain, k_gain]
    o_spec = pl.BlockSpec((Q_BLOCK, 256), lambda b, h, i: (b * nq + i, h))
    o_shape = jax.ShapeDtypeStruct((nb * seq, GROUP_W), BF16)
    lk = seq
    if has_ctx:
        ck, cv, cos, sin = ctx
        lk = seq + PAST_LEN
        in_specs += [
            pl.BlockSpec((None, None, PAST_LEN, 64), lambda b, h, i: (b, h, 0, 0)),
            pl.BlockSpec((None, None, PAST_LEN, 64), lambda b, h, i: (b, h, 0, 0)),
            pl.BlockSpec((Q_BLOCK, 128), lambda b, h, i: (i, 0)),
            pl.BlockSpec((Q_BLOCK, 128), lambda b, h, i: (i, 0)),
            pl.BlockSpec((seq, 128), lambda b, h, i: (0, 0)),
            pl.BlockSpec((seq, 128), lambda b, h, i: (0, 0)),
        ]
        args += [ck, cv, cos, sin, cos, sin]
        out_specs, out_shape = o_spec, o_shape
    else:
        out_specs = [o_spec, pl.BlockSpec((None, None, seq, 64), lambda b, h, i: (b, h, 0, 0))]
        out_shape = [o_shape, jax.ShapeDtypeStruct((nb, GQA_KV_HEADS, seq, 64), F32)]
    return pl.pallas_call(
        functools.partial(_gqa_kernel, seq=seq, has_ctx=has_ctx),
        grid=(nb, GQA_KV_HEADS, nq),
        in_specs=in_specs,
        out_specs=out_specs,
        out_shape=out_shape,
        scratch_shapes=[pltpu.VMEM((lk, 64), BF16), pltpu.VMEM((lk, 64), BF16)],
        compiler_params=_cparams(("arbitrary", "arbitrary", "arbitrary")),
        name="mixer_gqa_dec" if has_ctx else "mixer_gqa_ctx",
    )(*args)


def _rope_tables(n_tok):
    rows = n_tok // GRID_W
    r, c = jnp.meshgrid(jnp.arange(rows, dtype=F32), jnp.arange(GRID_W, dtype=F32), indexing='ij')
    nf = 16
    freqs = ROPE_THETA ** (-jnp.arange(nf, dtype=F32) / nf)
    ang = jnp.stack([r.reshape(-1)[:, None] * freqs, c.reshape(-1)[:, None] * freqs], axis=1)
    cos, sin = jnp.cos(ang), jnp.sin(ang)
    cos64 = jnp.concatenate([cos, cos], axis=2).reshape(n_tok, 64)
    sin64 = jnp.concatenate([-sin, sin], axis=2).reshape(n_tok, 64)
    return jnp.tile(cos64, (1, 2)), jnp.tile(sin64, (1, 2))


def _pad_lanes(v, n=128):
    v = v.reshape(1, -1)
    return jnp.pad(v, ((0, 0), (0, n - v.shape[1])))


def kernel(x_prompt, x_sample, c, cache_diff_k, cache_diff_v, cache_gqa_k, cache_gqa_v, state_ssm, state_mlstm_c, state_mlstm_n, state_mlstm_m, c_ctx, w_ada, b_ada, norm1, norm2, w_in, w_out, conv_ssd_w, conv_ssd_b, ssd_a_log, ssd_dt_bias, ssd_d, ssd_norm, diff_lq1, diff_lk1, diff_lq2, diff_lk2, conv_mlstm_w, conv_mlstm_b, mlstm_gate_b, mlstm_norm, gqa_q_norm, gqa_k_norm, w_ffn_in, w_ffn_out, norm_f):
    w_in_p = jnp.take(jnp.pad(w_in.astype(BF16), ((0, 0), (0, 0), (0, 1))), _IN_PERM, axis=2)
    w_out_b = w_out.astype(BF16)
    w_ffn_in_b = w_ffn_in.astype(BF16)
    w_ffn_out_b = w_ffn_out.astype(BF16)
    cvec = jnp.concatenate([c_ctx[None, :], c, jnp.zeros((5, D_MODEL), F32)], axis=0)
    cos_t, sin_t = _rope_tables(DEC_SEQ)
    pad_taps = lambda w: jnp.pad(w, ((0, 8 - D_CONV), (0, 0)))

    mod = _modulation(cvec, w_ada, b_ada).reshape(DEPTH, 8, 1, 6 * D_MODEL)
    x = jnp.concatenate([x_prompt.reshape(N_CTX, D_MODEL), x_sample.reshape(N_DEC, D_MODEL)], axis=0)

    per_layer = []
    for l in range(DEPTH):
        mod_l = mod[l]
        u = _inproj(x, mod_l, norm1[l].reshape(1, D_MODEL), w_in_p[l])

        ssd_args = (pad_taps(conv_ssd_w[l]), conv_ssd_b[l].reshape(1, -1), _pad_lanes(ssd_a_log[l]),
                    _pad_lanes(ssd_dt_bias[l]), jnp.repeat(ssd_d[l], SSM_HEAD_DIM).reshape(1, GROUP_W),
                    ssd_norm[l].reshape(1, GROUP_W))
        ya_c, ssm_fin = _mixer_ssd(u, 0, BATCH, SEQ, *ssd_args, None)
        ya_d = _mixer_ssd(u, N_CTX, DEC_BATCH, DEC_SEQ, *ssd_args, state_ssm[:, l])

        lam_init = 0.8 - 0.6 * math.exp(-0.3 * l)
        lam_rows = jnp.pad(jnp.stack([diff_lq1[l], diff_lk1[l], diff_lq2[l], diff_lk2[l]]), ((0, 4), (0, 64)))
        yb_c = _mixer_diff(u, 0, BATCH, SEQ, lam_rows, lam_init, None)
        yb_d = _mixer_diff(u, N_CTX, DEC_BATCH, DEC_SEQ, lam_rows, lam_init,
                           (cache_diff_k[:, l].reshape(DEC_BATCH, PAST_LEN, GROUP_W),
                            cache_diff_v[:, l].reshape(DEC_BATCH, PAST_LEN, GROUP_W), cos_t, sin_t))

        ml_args = (pad_taps(conv_mlstm_w[l]), conv_mlstm_b[l].reshape(1, -1), _pad_lanes(mlstm_gate_b[l]),
                   mlstm_norm[l].reshape(1, GROUP_W))
        yc_c, c_fin, n_fin, m_fin = _mixer_mlstm(u, 0, BATCH, SEQ, *ml_args, None)
        m0 = jnp.broadcast_to(state_mlstm_m[:, l][..., None, None], (DEC_BATCH, 2, MLSTM_HEADS, 1, MLSTM_HEAD_DIM))
        yc_d = _mixer_mlstm(u, N_CTX, DEC_BATCH, DEC_SEQ, *ml_args,
                            (state_mlstm_c[:, l], state_mlstm_n[:, l][:, :, :, None, :], m0))

        q_gain = jnp.tile(gqa_q_norm[l], 2).reshape(1, 128)
        k_gain = jnp.tile(gqa_k_norm[l], 2).reshape(1, 128)
        yd_c, kn = _mixer_gqa(u, 0, BATCH, SEQ, q_gain, k_gain, None)
        yd_d = _mixer_gqa(u, N_CTX, DEC_BATCH, DEC_SEQ, q_gain, k_gain,
                          (cache_gqa_k[:, l].transpose(0, 2, 1, 3), cache_gqa_v[:, l].transpose(0, 2, 1, 3), cos_t, sin_t))

        x = _outproj((ya_c, yb_c, yc_c, yd_c), (ya_d, yb_d, yc_d, yd_d), x, mod_l, w_out_b[l])
        x = _ffn(x, mod_l, norm2[l].reshape(1, D_MODEL), w_ffn_in_b[l], w_ffn_out_b[l])

        u_ctx = u[:N_CTX]
        kv = u_ctx[:, D_KV:D_KV + 256].reshape(BATCH, SEQ, GQA_KV_HEADS, 2, GQA_HEAD_DIM)
        per_layer.append((
            u_ctx[:, B_K:B_K + 512].reshape(BATCH, SEQ, DIFF_HEADS, 2, 64),
            u_ctx[:, B_V:B_V + 512].reshape(BATCH, SEQ, DIFF_HEADS, 128),
            kn.transpose(0, 2, 1, 3),
            kv[:, :, :, 1],
            ssm_fin,
            c_fin,
            n_fin[:, :, :, 0, :],
            m_fin[:, :, :, 0, 0],
        ))

    y_prompt = _final_norm(x, norm_f.reshape(1, D_MODEL), 0, N_CTX).reshape(BATCH, SEQ, D_MODEL)
    y_sample = _final_norm(x, norm_f.reshape(1, D_MODEL), N_CTX, N_DEC).reshape(DEC_BATCH, DEC_SEQ, D_MODEL)
    stacked = tuple(jnp.stack([s[i] for s in per_layer], axis=1) for i in range(8))
    return (y_prompt, y_sample) + stacked
```

```python
import functools
import math

import numpy as np
import jax
import jax.numpy as jnp
from jax import lax
from jax.experimental import pallas as pl
from jax.experimental.pallas import tpu as pltpu

F32 = jnp.float32
BF16 = jnp.bfloat16

D_MODEL = 1024
BATCH = 16
SEQ = 256
DEPTH = 4
DEC_BATCH = 2
DEC_SEQ = 2048
PAST_LEN = 256
GRID_W = 64
GROUP_W = 512
D_MIX = 2048
CHUNK = 128
Q_BLOCK = 128
D_CONV = 5
ROPE_THETA = 10000.0
EPS = 1e-6
SSM_HEADS = 8
SSM_HEAD_DIM = 64
SSM_STATE = 64
SSM_CONV_CH = 768
DIFF_HEADS = 4
MLSTM_HEADS = 4
MLSTM_HEAD_DIM = 128
GQA_KV_HEADS = 2
GQA_HEAD_DIM = 64
IN_COLS = 5664
D_FF = 2816

N_CTX = BATCH * SEQ
N_DEC = DEC_BATCH * DEC_SEQ
N_TOK = N_CTX + N_DEC

U_COLS = 6144
C_QK, C_V, C_O = 0, 1024, 1536
A_Z = 2048
B_Q, B_K, B_V = 2560, 3072, 3584
D_Q = 4096
A_XBC = 4608
D_KV = 5376
A_DT = 5632
C_G = 5760

VMEM_LIMIT_BYTES = 56 * 1024 * 1024


def _in_col_permutation():
    idx = np.full((U_COLS,), IN_COLS, np.int32)
    a0, b0, c0, d0 = 0, 1296, 2832, 4896

    def put(dst, src, n):
        idx[dst:dst + n] = np.arange(src, src + n)

    put(A_Z, a0, 512)
    put(A_XBC, a0 + 512, 768)
    put(A_DT, a0 + 1280, 16)
    put(B_Q, b0, 512)
    put(B_K, b0 + 512, 512)
    put(B_V, b0 + 1024, 512)
    put(C_QK, c0, 1024)
    put(C_V, c0 + 1024, 512)
    put(C_O, c0 + 1536, 512)
    put(C_G, c0 + 2048, 16)
    put(D_Q, d0, 512)
    for kv in range(GQA_KV_HEADS):
        put(D_KV + kv * 128, d0 + 512 + kv * 64, 64)
        put(D_KV + kv * 128 + 64, d0 + 640 + kv * 64, 64)
    return idx


_IN_PERM = _in_col_permutation()


def _runs(idx):
    out, start = [], 0
    for p in range(1, len(idx) + 1):
        pad = idx[start] == IN_COLS
        if p == len(idx) or (idx[p] == IN_COLS) != pad or (not pad and idx[p] != idx[p - 1] + 1):
            out.append((None if pad else int(idx[start]), p - start))
            start = p
    return out


_IN_SEGMENTS = _runs(_IN_PERM)


def _cparams(sem):
    return pltpu.CompilerParams(dimension_semantics=sem, vmem_limit_bytes=VMEM_LIMIT_BYTES)


def _bdot(a, b):
    return jnp.dot(a.astype(BF16), b.astype(BF16), preferred_element_type=F32)


def _bdot_t(a, b):
    return lax.dot_general(a.astype(BF16), b.astype(BF16), (((1,), (1,)), ((), ())),
                           preferred_element_type=F32)


def _split3(v):
    hi = v.astype(BF16)
    r = v - hi.astype(F32)
    mid = r.astype(BF16)
    lo = (r - mid.astype(F32)).astype(BF16)
    return hi, mid, lo


def _xdot_l(e, v):
    hi, mid, lo = _split3(v)
    return (jnp.dot(e, hi, preferred_element_type=F32) + jnp.dot(e, mid, preferred_element_type=F32)
            + jnp.dot(e, lo, preferred_element_type=F32))


def _xdot_r(v, e):
    hi, mid, lo = _split3(v)
    return (jnp.dot(hi, e, preferred_element_type=F32) + jnp.dot(mid, e, preferred_element_type=F32)
            + jnp.dot(lo, e, preferred_element_type=F32))


def _sigmoid(x):
    return 1.0 / (1.0 + jnp.exp(-x))


def _silu(x):
    return x * _sigmoid(x)


def _softplus(x):
    return jnp.maximum(x, 0.0) + jnp.log1p(jnp.exp(-jnp.abs(x)))


def _log_sigmoid(x):
    return jnp.minimum(x, 0.0) - jnp.log1p(jnp.exp(-jnp.abs(x)))


def _rms(x):
    return x * lax.rsqrt(jnp.mean(x * x, axis=-1, keepdims=True) + EPS)


def _iota(shape, dim):
    return lax.broadcasted_iota(jnp.int32, shape, dim)


def _rope(x, cos, sin_signed):
    lane = _iota(x.shape, 1)
    first = (lane % 32) < 16
    xr = jnp.where(first, pltpu.roll(x, 112, 1), pltpu.roll(x, 16, 1))
    return x * cos + xr * sin_signed


def _mod_index(tile_rows):
    n_ctx_tiles = N_CTX // tile_rows
    per_sample = DEC_SEQ // tile_rows

    def f(i):
        return jnp.where(i < n_ctx_tiles, 0, 1 + (i - n_ctx_tiles) // per_sample)

    return f


def _mod_kernel(c_ref, w_ref, b_ref, o_ref):
    c = c_ref[...]
    s = _silu(c)
    s_hi = s.astype(BF16)
    s_lo = (s - s_hi.astype(F32)).astype(BF16)
    w = w_ref[...]
    w_hi = w.astype(BF16)
    w_lo = (w - w_hi.astype(F32)).astype(BF16)
    acc = jnp.dot(s_hi, w_hi, preferred_element_type=F32)
    acc = acc + jnp.dot(s_hi, w_lo, preferred_element_type=F32)
    acc = acc + jnp.dot(s_lo, w_hi, preferred_element_type=F32)
    o_ref[...] = acc + b_ref[...]


def _modulation(cvec, w_ada, b_ada):
    tn = 1536
    n = 6 * D_MODEL
    return pl.pallas_call(
        _mod_kernel,
        grid=(DEPTH, n // tn),
        in_specs=[
            pl.BlockSpec((8, D_MODEL), lambda l, j: (0, 0)),
            pl.BlockSpec((None, D_MODEL, tn), lambda l, j: (l, 0, j)),
            pl.BlockSpec((None, 1, tn), lambda l, j: (l, 0, j)),
        ],
        out_specs=pl.BlockSpec((None, 8, tn), lambda l, j: (l, 0, j)),
        out_shape=jax.ShapeDtypeStruct((DEPTH, 8, n), F32),
        compiler_params=_cparams(("arbitrary", "arbitrary")),
        name="adaln_mod",
    )(cvec, w_ada, b_ada.reshape(DEPTH, 1, n))


ROW_SUB = 256


def _norm_mod_to(h_ref, x_ref, gain_ref, shift, scale, rows):
    def body(s, carry):
        r = pl.multiple_of(s * ROW_SUB, ROW_SUB)
        x = x_ref[pl.ds(r, ROW_SUB), :]
        h = (_rms(x) * gain_ref[...]) * (1.0 + scale) + shift
        h_ref[pl.ds(r, ROW_SUB), :] = h.astype(BF16)
        return carry

    lax.fori_loop(0, rows // ROW_SUB, body, 0)


def _inproj_kernel(x_ref, mod_ref, gain_ref, w_ref, u_ref, h_ref, *, tm):
    @pl.when(pl.program_id(1) == 0)
    def _():
        shift = mod_ref[:, 0:D_MODEL]
        scale = mod_ref[:, D_MODEL:2 * D_MODEL]
        _norm_mod_to(h_ref, x_ref, gain_ref, shift, scale, tm)

    u_ref[...] = jnp.dot(h_ref[...], w_ref[...], preferred_element_type=F32)


def _inproj(x, mod, gain, w, l):
    tm, tn = 1024, 1536
    midx = _mod_index(tm)
    return pl.pallas_call(
        functools.partial(_inproj_kernel, tm=tm),
        grid=(N_TOK // tm, U_COLS // tn),
        in_specs=[
            pl.BlockSpec((tm, D_MODEL), lambda i, j: (i, 0)),
            pl.BlockSpec((None, None, 1, 6 * D_MODEL), lambda i, j: (l, midx(i), 0, 0)),
            pl.BlockSpec((None, 1, D_MODEL), lambda i, j: (l, 0, 0)),
            pl.BlockSpec((None, D_MODEL, tn), lambda i, j: (l, 0, j)),
        ],
        out_specs=pl.BlockSpec((tm, tn), lambda i, j: (i, j)),
        out_shape=jax.ShapeDtypeStruct((N_TOK, U_COLS), F32),
        scratch_shapes=[pltpu.VMEM((tm, D_MODEL), BF16)],
        compiler_params=_cparams(("arbitrary", "arbitrary")),
        name="in_proj",
    )(x, mod, gain, w)


def _outproj_kernel(*refs, n_ctx_tiles):
    ctx_refs = refs[0:4]
    dec_refs = refs[4:8]
    x_ref, mod_ref, w_ref, o_ref = refs[8:12]
    i = pl.program_id(0)

    def compute(ys):
        acc = jnp.dot(ys[0][...], w_ref[0:512, :], preferred_element_type=F32)
        for g in range(1, 4):
            acc = acc + jnp.dot(ys[g][...], w_ref[g * 512:(g + 1) * 512, :], preferred_element_type=F32)
        gate = mod_ref[:, 2 * D_MODEL:3 * D_MODEL]
        o_ref[...] = x_ref[...] + gate * acc

    @pl.when(i < n_ctx_tiles)
    def _():
        compute(ctx_refs)

    @pl.when(i >= n_ctx_tiles)
    def _():
        compute(dec_refs)


def _outproj(ys_ctx, ys_dec, x, mod, w, l):
    tm = 512
    nct = N_CTX // tm
    midx = _mod_index(tm)
    ctx_spec = pl.BlockSpec((tm, GROUP_W), lambda i: (jnp.minimum(i, nct - 1), 0))
    dec_spec = pl.BlockSpec((tm, GROUP_W), lambda i: (jnp.maximum(i - nct, 0), 0))
    return pl.pallas_call(
        functools.partial(_outproj_kernel, n_ctx_tiles=nct),
        grid=(N_TOK // tm,),
        in_specs=[ctx_spec] * 4 + [dec_spec] * 4 + [
            pl.BlockSpec((tm, D_MODEL), lambda i: (i, 0)),
            pl.BlockSpec((None, None, 1, 6 * D_MODEL), lambda i: (l, midx(i), 0, 0)),
            pl.BlockSpec((None, D_MIX, D_MODEL), lambda i: (l, 0, 0)),
        ],
        out_specs=pl.BlockSpec((tm, D_MODEL), lambda i: (i, 0)),
        out_shape=jax.ShapeDtypeStruct((N_TOK, D_MODEL), F32),
        compiler_params=_cparams(("arbitrary",)),
        name="out_proj",
    )(*ys_ctx, *ys_dec, x, mod, w)


def _ffn_kernel(x_ref, mod_ref, gain_ref, wg_ref, wu_ref, wo_ref, o_ref, h_ref, acc_ref, *, tm, n_f):
    j = pl.program_id(1)

    @pl.when(j == 0)
    def _():
        shift = mod_ref[:, 3 * D_MODEL:4 * D_MODEL]
        scale = mod_ref[:, 4 * D_MODEL:5 * D_MODEL]
        _norm_mod_to(h_ref, x_ref, gain_ref, shift, scale, tm)

    h = h_ref[...]
    gate = jnp.dot(h, wg_ref[...], preferred_element_type=F32)
    up = jnp.dot(h, wu_ref[...], preferred_element_type=F32)
    act = (_silu(gate) * up).astype(BF16)
    part = jnp.dot(act, wo_ref[...], preferred_element_type=F32)

    @pl.when(j == 0)
    def _():
        acc_ref[...] = part

    @pl.when(j > 0)
    def _():
        acc_ref[...] += part

    @pl.when(j == n_f - 1)
    def _():
        g2 = mod_ref[:, 5 * D_MODEL:6 * D_MODEL]
        o_ref[...] = x_ref[...] + g2 * acc_ref[...]


def _ffn(x, mod, gain, w_in, w_out, l):
    tm, tf = 1024, 256
    n_f = D_FF // tf
    midx = _mod_index(tm)
    return pl.pallas_call(
        functools.partial(_ffn_kernel, tm=tm, n_f=n_f),
        grid=(N_TOK // tm, n_f),
        in_specs=[
            pl.BlockSpec((tm, D_MODEL), lambda i, j: (i, 0)),
            pl.BlockSpec((None, None, 1, 6 * D_MODEL), lambda i, j: (l, midx(i), 0, 0)),
            pl.BlockSpec((None, 1, D_MODEL), lambda i, j: (l, 0, 0)),
            pl.BlockSpec((None, D_MODEL, tf), lambda i, j: (l, 0, j)),
            pl.BlockSpec((None, D_MODEL, tf), lambda i, j: (l, 0, j + n_f)),
            pl.BlockSpec((None, tf, D_MODEL), lambda i, j: (l, j, 0)),
        ],
        out_specs=pl.BlockSpec((tm, D_MODEL), lambda i, j: (i, 0)),
        out_shape=jax.ShapeDtypeStruct((N_TOK, D_MODEL), F32),
        scratch_shapes=[pltpu.VMEM((tm, D_MODEL), BF16), pltpu.VMEM((tm, D_MODEL), F32)],
        compiler_params=_cparams(("arbitrary", "arbitrary")),
        name="ffn",
    )(x, mod, gain, w_in, w_in, w_out)


def _final_norm_kernel(x_ref, g_ref, o_ref):
    o_ref[...] = _rms(x_ref[...]) * g_ref[...]


def _final_norm(x, gain, row0, rows):
    tm = 512
    off = row0 // tm
    return pl.pallas_call(
        _final_norm_kernel,
        grid=(rows // tm,),
        in_specs=[pl.BlockSpec((tm, D_MODEL), lambda i: (i + off, 0)),
                  pl.BlockSpec((1, D_MODEL), lambda i: (0, 0))],
        out_specs=pl.BlockSpec((tm, D_MODEL), lambda i: (i, 0)),
        out_shape=jax.ShapeDtypeStruct((rows, D_MODEL), F32),
        compiler_params=_cparams(("arbitrary",)),
        name="final_norm",
    )(x, gain)


def _causal_masks():
    i = _iota((CHUNK, CHUNK), 0)
    j = _iota((CHUNK, CHUNK), 1)
    return j <= i, j >= i


def _conv_silu_chunk(pad_ref, w_ref, b_ref, r):
    win = pad_ref[pl.ds(r, CHUNK + 16), :]
    acc = b_ref[...] + w_ref[0:1, :] * win[6:6 + CHUNK, :]
    for tap in range(1, D_CONV):
        acc = acc + w_ref[tap:tap + 1, :] * win[6 + tap:6 + tap + CHUNK, :]
    return _silu(acc)


def _fill_padded(pad_ref, src_ref, seq, width):
    zeros = jnp.zeros((8, width), F32)
    pad_ref[0:8, :] = zeros
    pad_ref[seq + 8:seq + 16, :] = zeros

    def body(c, carry):
        r = pl.multiple_of(c * CHUNK, CHUNK)
        pad_ref[pl.ds(r + 8, CHUNK), :] = src_ref[pl.ds(r, CHUNK), :]
        return carry

    lax.fori_loop(0, seq // CHUNK, body, 0)


def _ssd_kernel(*refs, seq, has_state):
    (z_ref, xbc_ref, dt_ref, cw_ref, cb_ref, alog_ref, dtb_ref, dsk_ref, ng_ref) = refs[0:9]
    if has_state:
        h0_ref, y_ref = refs[9:11]
        hfin_ref = None
        pad_ref, xc_ref, yacc_ref, st_ref = refs[11:15]
    else:
        h0_ref = None
        y_ref, hfin_ref = refs[9:11]
        pad_ref, xc_ref, yacc_ref, st_ref = refs[11:15]
    nc = seq // CHUNK

    _fill_padded(pad_ref, xbc_ref, seq, SSM_CONV_CH)

    def conv_body(c, carry):
        r = pl.multiple_of(c * CHUNK, CHUNK)
        xc_ref[pl.ds(r, CHUNK), :] = _conv_silu_chunk(pad_ref, cw_ref, cb_ref, r)
        return carry

    lax.fori_loop(0, nc, conv_body, 0)

    if has_state:
        st_ref[...] = h0_ref[...]
    else:
        st_ref[...] = jnp.zeros(st_ref.shape, F32)

    mask_f, mask_b = _causal_masks()
    a_neg = -jnp.exp(alog_ref[...])

    def chunk(d, r):
        mask = mask_f if d == 0 else mask_b
        tmat = mask.astype(BF16)
        xc = xc_ref[pl.ds(r, CHUNK), :]
        dt = _softplus(dt_ref[pl.ds(r, CHUNK), :] + dtb_ref[...])
        ld = dt * a_neg
        cum_c = _xdot_l(tmat, ld)
        cum_r = cum_c.T
        last = cum_c[CHUNK - 1:CHUNK, :] if d == 0 else cum_c[0:1, :]
        e_last = jnp.exp(last)
        e_cum = jnp.exp(cum_c)
        w_end = jnp.exp(last - cum_c)
        gmats = []
        for g in range(2):
            bm = xc[:, 512 + g * 64:512 + (g + 1) * 64]
            cm = xc[:, 640 + g * 64:640 + (g + 1) * 64]
            gmats.append((bm, cm, _bdot_t(cm, bm)))
        ys = []
        for h in range(SSM_HEADS):
            col = d * SSM_HEADS + h
            bm, cm, gmat = gmats[h // 4]
            cc = cum_c[:, col:col + 1]
            cr = cum_r[col:col + 1, :]
            decay = jnp.exp(jnp.where(mask, cc - cr, -jnp.inf))
            xdt = xc[:, h * 64:(h + 1) * 64] * dt[:, col:col + 1]
            hprev = st_ref[d, h]
            y = _bdot(gmat * decay, xdt) + _bdot_t(cm, hprev) * e_cum[:, col:col + 1]
            cs = _bdot((xdt * w_end[:, col:col + 1]).T, bm)
            st_ref[d, h] = e_last[:, col:col + 1] * hprev + cs
            ys.append(y)
        return jnp.concatenate(ys, axis=1), xc[:, 0:GROUP_W]

    def fwd_body(c, carry):
        r = pl.multiple_of(c * CHUNK, CHUNK)
        y, x = chunk(0, r)
        yacc_ref[pl.ds(r, CHUNK), :] = y + dsk_ref[...] * x
        return carry

    lax.fori_loop(0, nc, fwd_body, 0)

    def bwd_body(s, carry):
        r = pl.multiple_of((nc - 1 - s) * CHUNK, CHUNK)
        y, _ = chunk(1, r)
        y = (yacc_ref[pl.ds(r, CHUNK), :] + y) * _silu(z_ref[pl.ds(r, CHUNK), :])
        y_ref[pl.ds(r, CHUNK), :] = (_rms(y) * ng_ref[...]).astype(BF16)
        return carry

    lax.fori_loop(0, nc, bwd_body, 0)

    if hfin_ref is not None:
        hfin_ref[...] = st_ref[...]


def _mixer_ssd(u, row0, nb, seq, conv_w, conv_b, a_log, dt_bias, d_skip, norm_g, h0):
    has_state = h0 is not None
    roff = row0 // seq
    vec = lambda n: pl.BlockSpec((1, n), lambda b: (0, 0))
    in_specs = [
        pl.BlockSpec((seq, GROUP_W), lambda b: (b + roff, A_Z // GROUP_W)),
        pl.BlockSpec((seq, SSM_CONV_CH), lambda b: (b + roff, A_XBC // SSM_CONV_CH)),
        pl.BlockSpec((seq, 128), lambda b: (b + roff, A_DT // 128)),
        pl.BlockSpec((8, SSM_CONV_CH), lambda b: (0, 0)),
        vec(SSM_CONV_CH), vec(128), vec(128), vec(GROUP_W), vec(GROUP_W),
    ]
    args = [u, u, u, conv_w, conv_b, a_log, dt_bias, d_skip, norm_g]
    st_block = (None, 2, SSM_HEADS, SSM_HEAD_DIM, SSM_STATE)
    y_spec = pl.BlockSpec((seq, GROUP_W), lambda b: (b, 0))
    y_shape = jax.ShapeDtypeStruct((nb * seq, GROUP_W), BF16)
    if has_state:
        in_specs.append(pl.BlockSpec(st_block, lambda b: (b, 0, 0, 0, 0)))
        args.append(h0)
        out_specs, out_shape = y_spec, y_shape
    else:
        out_specs = [y_spec, pl.BlockSpec(st_block, lambda b: (b, 0, 0, 0, 0))]
        out_shape = [y_shape, jax.ShapeDtypeStruct((nb, 2, SSM_HEADS, SSM_HEAD_DIM, SSM_STATE), F32)]
    return pl.pallas_call(
        functools.partial(_ssd_kernel, seq=seq, has_state=has_state),
        grid=(nb,),
        in_specs=in_specs,
        out_specs=out_specs,
        out_shape=out_shape,
        scratch_shapes=[
            pltpu.VMEM((seq + 16, SSM_CONV_CH), F32),
            pltpu.VMEM((seq, SSM_CONV_CH), F32),
            pltpu.VMEM((seq, GROUP_W), F32),
            pltpu.VMEM((2, SSM_HEADS, SSM_HEAD_DIM, SSM_STATE), F32),
        ],
        compiler_params=_cparams(("arbitrary",)),
        name="mixer_ssd_dec" if has_state else "mixer_ssd_ctx",
    )(*args)


def _mlstm_kernel(*refs, seq, has_state):
    (q_ref, k_ref, v_ref, o_ref, g_ref, cwq_ref, cwk_ref, cbq_ref, cbk_ref, gb_ref, ng_ref) = refs[0:11]
    if has_state:
        c0_ref, n0_ref, m0_ref, y_ref = refs[11:15]
        outs = None
    else:
        y_ref = refs[11]
        outs = refs[12:15]
    (qpad_ref, kpad_ref, qc_ref, hf_ref, hb_ref, a_ref, cl_ref, rs_ref, rm_ref, cu_ref, rows_ref,
     cst_ref, nst_ref, mst_ref) = refs[15:]
    nc = seq // CHUNK
    head = pl.program_id(1)

    _fill_padded(qpad_ref, q_ref, seq, MLSTM_HEAD_DIM)
    _fill_padded(kpad_ref, k_ref, seq, MLSTM_HEAD_DIM)

    mask_f, mask_b = _causal_masks()
    row_id = _iota((CHUNK, CHUNK), 0)
    full = (CHUNK, CHUNK)

    sel = jnp.concatenate(
        [(row_id == col).astype(BF16) for col in (head, 4 + head, 8 + head, 12 + head)], axis=1)
    tmats = (mask_f.astype(BF16), mask_b.astype(BF16))
    masks = (mask_f, mask_b)

    def local_body(s, carry):
        chunks = (2 * s, 2 * s + 1)
        qs, ks, vs, qks, gsel = [], [], [], [], []
        for c in chunks:
            r = pl.multiple_of(c * CHUNK, CHUNK)
            q = _conv_silu_chunk(qpad_ref, cwq_ref, cbq_ref, r) * (MLSTM_HEAD_DIM ** -0.5)
            k = _conv_silu_chunk(kpad_ref, cwk_ref, cbk_ref, r)
            qc_ref[pl.ds(r, CHUNK), :] = q
            qs.append(q)
            ks.append(k)
            vs.append(v_ref[pl.ds(r, CHUNK), :])
            gsel.append(_xdot_r(g_ref[pl.ds(r, CHUNK), :] + gb_ref[...], sel))
        for i in range(2):
            qks.append(_bdot_t(qs[i], ks[i]))
        items = [(i, d) for i in range(2) for d in range(2)]
        li_c = [gsel[i][:, 256 * d:256 * d + 128] for i, d in items]
        lf_c = [_log_sigmoid(gsel[i][:, 256 * d + 128:256 * d + 256]) for i, d in items]
        cum_c = [_xdot_l(tmats[d], lf_c[n]) for n, (i, d) in enumerate(items)]
        cum_r = [x.T for x in cum_c]
        li_r = [x.T for x in li_c]
        last = [cum_c[n][CHUNK - 1:CHUNK, :] if d == 0 else cum_c[n][0:1, :]
                for n, (i, d) in enumerate(items)]
        dmat = [jnp.where(masks[d], cum_c[n] - cum_r[n] + li_r[n], -jnp.inf) for n, (i, d) in enumerate(items)]
        rowmax = [jnp.max(x, axis=1, keepdims=True) for x in dmat]
        sp = [qks[i] * jnp.exp(dmat[n] - rowmax[n]) for n, (i, d) in enumerate(items)]
        m_loc = [jnp.max(last[n] - cum_r[n][0:1, :] + li_r[n][0:1, :], axis=1, keepdims=True)
                 for n in range(4)]
        kw = [ks[i] * jnp.exp(last[n] - cum_c[n] + li_c[n] - m_loc[n]) for n, (i, d) in enumerate(items)]
        kwt = [x.T for x in kw]
        a_loc = [_bdot(sp[n], vs[i]) for n, (i, d) in enumerate(items)]
        c_loc = [_bdot(kwt[n], vs[i]) for n, (i, d) in enumerate(items)]
        for n, (i, d) in enumerate(items):
            c = chunks[i]
            a_ref[c, d] = a_loc[n]
            cl_ref[c, d] = c_loc[n]
            rs_ref[c, d] = jnp.broadcast_to(jnp.sum(sp[n], axis=1, keepdims=True), full)
            rm_ref[c, d] = jnp.broadcast_to(rowmax[n], full)
            cu_ref[c, d] = cum_c[n]
            rows_ref[c, d, 0:1, :] = jnp.sum(kw[n], axis=0, keepdims=True)
            rows_ref[c, d, 1:2, :] = jnp.broadcast_to(m_loc[n], (1, CHUNK))
            rows_ref[c, d, 2:3, :] = last[n]
        return carry

    lax.fori_loop(0, nc // 2, local_body, 0)

    if has_state:
        cst_ref[...] = c0_ref[...]
        nst_ref[...] = n0_ref[...]
        mst_ref[...] = m0_ref[...]
    else:
        cst_ref[...] = jnp.zeros(cst_ref.shape, F32)
        nst_ref[...] = jnp.zeros(nst_ref.shape, F32)
        mst_ref[...] = jnp.zeros(mst_ref.shape, F32)

    def state_step(d, c):
        r = pl.multiple_of(c * CHUNK, CHUNK)
        q = qc_ref[pl.ds(r, CHUNK), :]
        c_p = cst_ref[d]
        n_p = nst_ref[d]
        m_p = mst_ref[d]
        n_loc = rows_ref[c, d, 0:1, :]
        m_loc = rows_ref[c, d, 1:2, :]
        last = rows_ref[c, d, 2:3, :]
        rowmax = rm_ref[c, d]
        inter = cu_ref[c, d] + m_p
        m_t = jnp.maximum(inter, rowmax)
        f_intra = jnp.exp(rowmax - m_t)
        w_inter = jnp.exp(inter - m_t)
        num = a_ref[c, d] * f_intra + w_inter * _bdot(q, c_p)
        den = rs_ref[c, d] * f_intra + w_inter * jnp.sum(q * n_p, axis=1, keepdims=True)
        hh = num / jnp.maximum(jnp.abs(den), jnp.exp(-m_t))
        m_new = jnp.maximum(last + m_p, m_loc)
        s_p = jnp.exp(last + m_p - m_new)
        s_l = jnp.exp(m_loc - m_new)
        cst_ref[d] = s_p[:, 0:1] * c_p + s_l[:, 0:1] * cl_ref[c, d]
        nst_ref[d] = s_p * n_p + s_l * n_loc
        mst_ref[d] = m_new
        return r, hh

    def state_body(s, carry):
        r, hh = state_step(0, s)
        hf_ref[pl.ds(r, CHUNK), :] = hh
        r, hh = state_step(1, nc - 1 - s)
        hb_ref[pl.ds(r, CHUNK), :] = hh
        return carry

    lax.fori_loop(0, nc, state_body, 0, unroll=2)

    def out_body(c, carry):
        r = pl.multiple_of(c * CHUNK, CHUNK)
        hsum = hf_ref[pl.ds(r, CHUNK), :] + hb_ref[pl.ds(r, CHUNK), :]
        y = _sigmoid(o_ref[pl.ds(r, CHUNK), :]) * (_rms(hsum) * ng_ref[...])
        y_ref[pl.ds(r, CHUNK), :] = y.astype(BF16)
        return carry

    lax.fori_loop(0, nc, out_body, 0, unroll=2)

    if outs is not None:
        outs[0][...] = cst_ref[...]
        outs[1][...] = nst_ref[...]
        outs[2][...] = mst_ref[...]


def _mixer_mlstm(u, row0, nb, seq, conv_w, conv_b, gate_b, norm_g, state):
    has_state = state is not None
    roff = row0 // seq
    hd = MLSTM_HEAD_DIM
    nc = seq // CHUNK
    col = lambda base: (lambda b, h: (b + roff, base // hd + h))
    in_specs = [
        pl.BlockSpec((seq, hd), col(C_QK)),
        pl.BlockSpec((seq, hd), col(C_QK + GROUP_W)),
        pl.BlockSpec((seq, hd), col(C_V)),
        pl.BlockSpec((seq, hd), col(C_O)),
        pl.BlockSpec((seq, 128), lambda b, h: (b + roff, C_G // 128)),
        pl.BlockSpec((8, hd), lambda b, h: (0, h)),
        pl.BlockSpec((8, hd), lambda b, h: (0, MLSTM_HEADS + h)),
        pl.BlockSpec((1, hd), lambda b, h: (0, h)),
        pl.BlockSpec((1, hd), lambda b, h: (0, MLSTM_HEADS + h)),
        pl.BlockSpec((1, 128), lambda b, h: (0, 0)),
        pl.BlockSpec((1, hd), lambda b, h: (0, h)),
    ]
    args = [u, u, u, u, u, conv_w, conv_w, conv_b, conv_b, gate_b, norm_g]
    c_spec = pl.BlockSpec((None, 2, None, hd, hd), lambda b, h: (b, 0, h, 0, 0))
    n_spec = pl.BlockSpec((None, 2, None, 1, hd), lambda b, h: (b, 0, h, 0, 0))
    y_spec = pl.BlockSpec((seq, hd), lambda b, h: (b, h))
    y_shape = jax.ShapeDtypeStruct((nb * seq, GROUP_W), BF16)
    if has_state:
        in_specs += [c_spec, n_spec, n_spec]
        args += list(state)
        out_specs, out_shape = y_spec, y_shape
    else:
        out_specs = [y_spec, c_spec, n_spec, n_spec]
        out_shape = [y_shape,
                     jax.ShapeDtypeStruct((nb, 2, MLSTM_HEADS, hd, hd), F32),
                     jax.ShapeDtypeStruct((nb, 2, MLSTM_HEADS, 1, hd), F32),
                     jax.ShapeDtypeStruct((nb, 2, MLSTM_HEADS, 1, hd), F32)]
    return pl.pallas_call(
        functools.partial(_mlstm_kernel, seq=seq, has_state=has_state),
        grid=(nb, MLSTM_HEADS),
        in_specs=in_specs,
        out_specs=out_specs,
        out_shape=out_shape,
        scratch_shapes=[
            pltpu.VMEM((seq + 16, hd), F32), pltpu.VMEM((seq + 16, hd), F32),
            pltpu.VMEM((seq, hd), F32), pltpu.VMEM((seq, hd), F32), pltpu.VMEM((seq, hd), F32),
            pltpu.VMEM((nc, 2, hd, hd), F32), pltpu.VMEM((nc, 2, hd, hd), F32),
            pltpu.VMEM((nc, 2, hd, hd), F32), pltpu.VMEM((nc, 2, hd, hd), F32), pltpu.VMEM((nc, 2, hd, hd), F32),
            pltpu.VMEM((nc, 2, 8, hd), F32),
            pltpu.VMEM((2, hd, hd), F32), pltpu.VMEM((2, 1, hd), F32), pltpu.VMEM((2, 1, hd), F32),
        ],
        compiler_params=_cparams(("arbitrary", "arbitrary")),
        name="mixer_mlstm_dec" if has_state else "mixer_mlstm_ctx",
    )(*args)


KV_SUB = 256


ATT_SUB = 256
DIFF_QB = 512
GQA_QB = 256


def _softmax_parts(s):
    e = jnp.exp(s - jnp.max(s, axis=1, keepdims=True))
    return e.astype(BF16), jnp.sum(e, axis=1, keepdims=True)


def _diff_lambda(lam_ref, lam_init):
    lp = lam_ref[...]
    return (jnp.exp(jnp.sum(lp[0:1, :] * lp[1:2, :], axis=1, keepdims=True))
            - jnp.exp(jnp.sum(lp[2:3, :] * lp[3:4, :], axis=1, keepdims=True)) + lam_init)


def _softmax_parts_t(s_t):
    e = jnp.exp(s_t - jnp.max(s_t, axis=0, keepdims=True))
    return e.astype(BF16), jnp.sum(e, axis=0, keepdims=True)


def _diff_dec_kernel(q_ref, k_ref, v_ref, lam_ref, ck_ref, cv_ref, cosq_ref, sinq_ref, cosk_ref, sink_ref,
                     o_ref, k1_ref, k2_ref, vt_ref, *, seq, lam_init):
    past = PAST_LEN

    @pl.when(pl.program_id(2) == 0)
    def _():
        ck = ck_ref[...]
        k1_ref[0:past, :] = ck[:, 0:64].astype(BF16)
        k2_ref[0:past, :] = ck[:, 64:128].astype(BF16)
        vt_ref[:, 0:past] = cv_ref[...].T.astype(BF16)
        for s in range(seq // KV_SUB):
            r = s * KV_SUB
            kk = _rope(k_ref[r:r + KV_SUB, :], cosk_ref[r:r + KV_SUB, :], sink_ref[r:r + KV_SUB, :])
            k1_ref[past + r:past + r + KV_SUB, :] = kk[:, 0:64].astype(BF16)
            k2_ref[past + r:past + r + KV_SUB, :] = kk[:, 64:128].astype(BF16)
            vt_ref[:, past + r:past + r + KV_SUB] = v_ref[r:r + KV_SUB, :].T.astype(BF16)

    lam = _diff_lambda(lam_ref, lam_init)
    q = _rope(q_ref[...], cosq_ref[...], sinq_ref[...]) * (64 ** -0.5)
    n_sub = q.shape[0] // ATT_SUB
    k_maps = (k1_ref[...], k2_ref[...])
    q_t = [q[j * ATT_SUB:(j + 1) * ATT_SUB, m * 64:(m + 1) * 64].T.astype(BF16)
           for j in range(n_sub) for m in range(2)]
    scores = [jnp.dot(k_maps[n % 2], q_t[n], preferred_element_type=F32) for n in range(2 * n_sub)]
    parts = [_softmax_parts_t(s) for s in scores]
    vt = vt_ref[...]
    pv = [jnp.dot(vt, e, preferred_element_type=F32) / l for e, l in parts]
    for j in range(n_sub):
        o = (pv[2 * j] - lam * pv[2 * j + 1]).T
        o_ref[j * ATT_SUB:(j + 1) * ATT_SUB, :] = (_rms(o) * (1.0 - lam_init)).astype(BF16)


def _diff_ctx_kernel(q_ref, k_ref, v_ref, lam_ref, o_ref, *, lam_init):
    lam = _diff_lambda(lam_ref, lam_init)
    q = q_ref[...] * (64 ** -0.5)
    k = k_ref[...].astype(BF16)
    v = v_ref[...].astype(BF16)
    cols = [h * 128 + m * 64 for h in range(DIFF_HEADS) for m in range(2)]
    scores = [_bdot_t(q[:, c0:c0 + 64], k[:, c0:c0 + 64]) for c0 in cols]
    parts = [_softmax_parts(s) for s in scores]
    pv = [jnp.dot(parts[n][0], v[:, (n // 2) * 128:(n // 2 + 1) * 128], preferred_element_type=F32)
          for n in range(2 * DIFF_HEADS)]
    outs = []
    for h in range(DIFF_HEADS):
        o = pv[2 * h] / parts[2 * h][1] - lam * (pv[2 * h + 1] / parts[2 * h + 1][1])
        outs.append(_rms(o) * (1.0 - lam_init))
    o_ref[...] = jnp.concatenate(outs, axis=1).astype(BF16)


def _mixer_diff(u, row0, nb, seq, lam_rows, lam_init, ctx):
    roff = row0 // seq
    o_shape = jax.ShapeDtypeStruct((nb * seq, GROUP_W), BF16)
    lam_spec3 = pl.BlockSpec((8, 128), lambda b, h, i: (0, 0))
    if ctx is None:
        col = lambda base: pl.BlockSpec((seq, GROUP_W), lambda b: (b + roff, base // GROUP_W))
        return pl.pallas_call(
            functools.partial(_diff_ctx_kernel, lam_init=lam_init),
            grid=(nb,),
            in_specs=[col(B_Q), col(B_K), col(B_V), pl.BlockSpec((8, 128), lambda b: (0, 0))],
            out_specs=pl.BlockSpec((seq, GROUP_W), lambda b: (b, 0)),
            out_shape=o_shape,
            compiler_params=_cparams(("arbitrary",)),
            name="mixer_diff_ctx",
        )(u, u, u, lam_rows)
    ck, cv, cos, sin = ctx
    qb = min(DIFF_QB, seq)
    qoff = row0 // qb
    nq = seq // qb
    lk = seq + PAST_LEN
    in_specs = [
        pl.BlockSpec((qb, 128), lambda b, h, i: (qoff + b * nq + i, B_Q // 128 + h)),
        pl.BlockSpec((seq, 128), lambda b, h, i: (b + roff, B_K // 128 + h)),
        pl.BlockSpec((seq, 128), lambda b, h, i: (b + roff, B_V // 128 + h)),
        lam_spec3,
        pl.BlockSpec((None, PAST_LEN, 128), lambda b, h, i: (b, 0, h)),
        pl.BlockSpec((None, PAST_LEN, 128), lambda b, h, i: (b, 0, h)),
        pl.BlockSpec((qb, 128), lambda b, h, i: (i, 0)),
        pl.BlockSpec((qb, 128), lambda b, h, i: (i, 0)),
        pl.BlockSpec((seq, 128), lambda b, h, i: (0, 0)),
        pl.BlockSpec((seq, 128), lambda b, h, i: (0, 0)),
    ]
    return pl.pallas_call(
        functools.partial(_diff_dec_kernel, seq=seq, lam_init=lam_init),
        grid=(nb, DIFF_HEADS, nq),
        in_specs=in_specs,
        out_specs=pl.BlockSpec((qb, 128), lambda b, h, i: (b * nq + i, h)),
        out_shape=o_shape,
        scratch_shapes=[pltpu.VMEM((lk, 64), BF16), pltpu.VMEM((lk, 64), BF16), pltpu.VMEM((128, lk), BF16)],
        compiler_params=_cparams(("arbitrary", "arbitrary", "arbitrary")),
        name="mixer_diff_dec",
    )(u, u, u, lam_rows, ck, cv, cos, sin, cos, sin)


def _seg_rms_pair(x, gain):
    low = _iota(x.shape, 1) < 64
    sq = x * x
    ms_lo = jnp.sum(jnp.where(low, sq, 0.0), axis=1, keepdims=True) * (1.0 / 64)
    ms_hi = jnp.sum(jnp.where(low, 0.0, sq), axis=1, keepdims=True) * (1.0 / 64)
    return x * lax.rsqrt(jnp.where(low, ms_lo, ms_hi) + EPS) * gain


def _gqa_attend(q_heads, kf, vf):
    rows = q_heads[0].shape[0]
    s = _bdot_t(jnp.concatenate(q_heads, axis=0), kf)
    e, l = _softmax_parts(s)
    o = jnp.dot(e, vf, preferred_element_type=F32) / l
    return [o[g * rows:(g + 1) * rows, :] for g in range(len(q_heads))]


def _gqa_dec_kernel(q_ref, kv_ref, qg_ref, kg_ref, ck_ref, cv_ref, cosq_ref, sinq_ref, cosk_ref, sink_ref,
                    o_ref, kf_ref, vt_ref, *, seq):
    past = PAST_LEN

    @pl.when(pl.program_id(2) == 0)
    def _():
        kf_ref[0:past, :] = ck_ref[...].astype(BF16)
        vt_ref[:, 0:past] = cv_ref[...].T.astype(BF16)
        for s in range(seq // KV_SUB):
            r = s * KV_SUB
            blk = kv_ref[r:r + KV_SUB, :]
            kn = _rope(_seg_rms_pair(blk, kg_ref[...]), cosk_ref[r:r + KV_SUB, :], sink_ref[r:r + KV_SUB, :])
            kf_ref[past + r:past + r + KV_SUB, :] = kn[:, 0:64].astype(BF16)
            vt_ref[:, past + r:past + r + KV_SUB] = blk[:, 64:128].T.astype(BF16)

    heads_t = []
    for half in range(2):
        qh = _seg_rms_pair(q_ref[:, half * 128:(half + 1) * 128], qg_ref[...])
        qh = _rope(qh, cosq_ref[...], sinq_ref[...]) * (64 ** -0.5)
        heads_t += [qh[:, 0:64].T.astype(BF16), qh[:, 64:128].T.astype(BF16)]
    kf = kf_ref[...]
    scores = [jnp.dot(kf, q_t, preferred_element_type=F32) for q_t in heads_t]
    parts = [_softmax_parts_t(s) for s in scores]
    vt = vt_ref[...]
    outs = [(jnp.dot(vt, e, preferred_element_type=F32) / l).T for e, l in parts]
    o_ref[...] = jnp.concatenate(outs, axis=1).astype(BF16)


def _gqa_ctx_kernel(q_ref, kv_ref, qg_ref, kg_ref, o_ref, kn_ref):
    outs = []
    for kvh in range(GQA_KV_HEADS):
        blk = kv_ref[:, kvh * 128:(kvh + 1) * 128]
        kn = _seg_rms_pair(blk, kg_ref[...])[:, 0:64]
        kn_ref[kvh] = kn
        heads = []
        for half in range(2):
            c0 = kvh * 256 + half * 128
            qh = _seg_rms_pair(q_ref[:, c0:c0 + 128], qg_ref[...]) * (64 ** -0.5)
            heads += [qh[:, 0:64], qh[:, 64:128]]
        outs += _gqa_attend(heads, kn.astype(BF16), blk[:, 64:128].astype(BF16))
    o_ref[...] = jnp.concatenate(outs, axis=1).astype(BF16)


def _mixer_gqa(u, row0, nb, seq, q_gain, k_gain, ctx):
    roff = row0 // seq
    o_shape = jax.ShapeDtypeStruct((nb * seq, GROUP_W), BF16)
    if ctx is None:
        gain = pl.BlockSpec((1, 128), lambda b: (0, 0))
        return pl.pallas_call(
            _gqa_ctx_kernel,
            grid=(nb,),
            in_specs=[pl.BlockSpec((seq, GROUP_W), lambda b: (b + roff, D_Q // GROUP_W)),
                      pl.BlockSpec((seq, 256), lambda b: (b + roff, D_KV // 256)), gain, gain],
            out_specs=[pl.BlockSpec((seq, GROUP_W), lambda b: (b, 0)),
                       pl.BlockSpec((None, GQA_KV_HEADS, seq, 64), lambda b: (b, 0, 0, 0))],
            out_shape=[o_shape, jax.ShapeDtypeStruct((nb, GQA_KV_HEADS, seq, 64), F32)],
            compiler_params=_cparams(("arbitrary",)),
            name="mixer_gqa_ctx",
        )(u, u, q_gain, k_gain)
    ck, cv, cos, sin = ctx
    qb = GQA_QB
    qoff = row0 // qb
    nq = seq // qb
    lk = seq + PAST_LEN
    in_specs = [
        pl.BlockSpec((qb, 256), lambda b, h, i: (qoff + b * nq + i, D_Q // 256 + h)),
        pl.BlockSpec((seq, 128), lambda b, h, i: (b + roff, D_KV // 128 + h)),
        pl.BlockSpec((1, 128), lambda b, h, i: (0, 0)),
        pl.BlockSpec((1, 128), lambda b, h, i: (0, 0)),
        pl.BlockSpec((None, None, PAST_LEN, 64), lambda b, h, i: (b, h, 0, 0)),
        pl.BlockSpec((None, None, PAST_LEN, 64), lambda b, h, i: (b, h, 0, 0)),
        pl.BlockSpec((qb, 128), lambda b, h, i: (i, 0)),
        pl.BlockSpec((qb, 128), lambda b, h, i: (i, 0)),
        pl.BlockSpec((seq, 128), lambda b, h, i: (0, 0)),
        pl.BlockSpec((seq, 128), lambda b, h, i: (0, 0)),
    ]
    return pl.pallas_call(
        functools.partial(_gqa_dec_kernel, seq=seq),
        grid=(nb, GQA_KV_HEADS, nq),
        in_specs=in_specs,
        out_specs=pl.BlockSpec((qb, 256), lambda b, h, i: (b * nq + i, h)),
        out_shape=o_shape,
        scratch_shapes=[pltpu.VMEM((lk, 64), BF16), pltpu.VMEM((64, lk), BF16)],
        compiler_params=_cparams(("arbitrary", "arbitrary", "arbitrary")),
        name="mixer_gqa_dec",
    )(u, u, q_gain, k_gain, ck, cv, cos, sin, cos, sin)


def _rope_tables(n_tok):
    rows = n_tok // GRID_W
    r, c = jnp.meshgrid(jnp.arange(rows, dtype=F32), jnp.arange(GRID_W, dtype=F32), indexing='ij')
    nf = 16
    freqs = ROPE_THETA ** (-jnp.arange(nf, dtype=F32) / nf)
    ang = jnp.stack([r.reshape(-1)[:, None] * freqs, c.reshape(-1)[:, None] * freqs], axis=1)
    cos, sin = jnp.cos(ang), jnp.sin(ang)
    cos64 = jnp.concatenate([cos, cos], axis=2).reshape(n_tok, 64)
    sin64 = jnp.concatenate([-sin, sin], axis=2).reshape(n_tok, 64)
    return jnp.tile(cos64, (1, 2)), jnp.tile(sin64, (1, 2))


def _pad_lanes(v, n=128):
    v = v.reshape(1, -1)
    return jnp.pad(v, ((0, 0), (0, n - v.shape[1])))


def kernel(x_prompt, x_sample, c, cache_diff_k, cache_diff_v, cache_gqa_k, cache_gqa_v, state_ssm, state_mlstm_c, state_mlstm_n, state_mlstm_m, c_ctx, w_ada, b_ada, norm1, norm2, w_in, w_out, conv_ssd_w, conv_ssd_b, ssd_a_log, ssd_dt_bias, ssd_d, ssd_norm, diff_lq1, diff_lk1, diff_lq2, diff_lk2, conv_mlstm_w, conv_mlstm_b, mlstm_gate_b, mlstm_norm, gqa_q_norm, gqa_k_norm, w_ffn_in, w_ffn_out, norm_f):
    w_in_b = w_in.astype(BF16)
    w_in_p = jnp.concatenate(
        [jnp.zeros((DEPTH, D_MODEL, n), BF16) if s is None else w_in_b[:, :, s:s + n] for s, n in _IN_SEGMENTS], axis=2)
    w_out_b = w_out.astype(BF16)
    gain1 = norm1.reshape(DEPTH, 1, D_MODEL)
    gain2 = norm2.reshape(DEPTH, 1, D_MODEL)
    w_ffn_in_b = w_ffn_in.astype(BF16)
    w_ffn_out_b = w_ffn_out.astype(BF16)
    cvec = jnp.concatenate([c_ctx[None, :], c, jnp.zeros((5, D_MODEL), F32)], axis=0)
    cos_t, sin_t = _rope_tables(DEC_SEQ)
    pad_taps = lambda w: jnp.pad(w, ((0, 8 - D_CONV), (0, 0)))

    mod = _modulation(cvec, w_ada, b_ada).reshape(DEPTH, 8, 1, 6 * D_MODEL)
    x = jnp.concatenate([x_prompt.reshape(N_CTX, D_MODEL), x_sample.reshape(N_DEC, D_MODEL)], axis=0)

    per_layer = []
    for l in range(DEPTH):
        u = _inproj(x, mod, gain1, w_in_p, l)

        ssd_args = (pad_taps(conv_ssd_w[l]), conv_ssd_b[l].reshape(1, -1), _pad_lanes(ssd_a_log[l]),
                    _pad_lanes(ssd_dt_bias[l]), jnp.repeat(ssd_d[l], SSM_HEAD_DIM).reshape(1, GROUP_W),
                    ssd_norm[l].reshape(1, GROUP_W))
        ya_c, ssm_fin = _mixer_ssd(u, 0, BATCH, SEQ, *ssd_args, None)
        ya_d = _mixer_ssd(u, N_CTX, DEC_BATCH, DEC_SEQ, *ssd_args, state_ssm[:, l])

        lam_init = 0.8 - 0.6 * math.exp(-0.3 * l)
        lam_rows = jnp.pad(jnp.stack([diff_lq1[l], diff_lk1[l], diff_lq2[l], diff_lk2[l]]), ((0, 4), (0, 64)))
        yb_c = _mixer_diff(u, 0, BATCH, SEQ, lam_rows, lam_init, None)
        yb_d = _mixer_diff(u, N_CTX, DEC_BATCH, DEC_SEQ, lam_rows, lam_init,
                           (cache_diff_k[:, l].reshape(DEC_BATCH, PAST_LEN, GROUP_W),
                            cache_diff_v[:, l].reshape(DEC_BATCH, PAST_LEN, GROUP_W), cos_t, sin_t))

        ml_args = (pad_taps(conv_mlstm_w[l]), conv_mlstm_b[l].reshape(1, -1), _pad_lanes(mlstm_gate_b[l]),
                   mlstm_norm[l].reshape(1, GROUP_W))
        yc_c, c_fin, n_fin, m_fin = _mixer_mlstm(u, 0, BATCH, SEQ, *ml_args, None)
        m0 = jnp.broadcast_to(state_mlstm_m[:, l][..., None, None], (DEC_BATCH, 2, MLSTM_HEADS, 1, MLSTM_HEAD_DIM))
        yc_d = _mixer_mlstm(u, N_CTX, DEC_BATCH, DEC_SEQ, *ml_args,
                            (state_mlstm_c[:, l], state_mlstm_n[:, l][:, :, :, None, :], m0))

        q_gain = jnp.tile(gqa_q_norm[l], 2).reshape(1, 128)
        k_gain = jnp.tile(gqa_k_norm[l], 2).reshape(1, 128)
        yd_c, kn = _mixer_gqa(u, 0, BATCH, SEQ, q_gain, k_gain, None)
        yd_d = _mixer_gqa(u, N_CTX, DEC_BATCH, DEC_SEQ, q_gain, k_gain,
                          (cache_gqa_k[:, l].transpose(0, 2, 1, 3), cache_gqa_v[:, l].transpose(0, 2, 1, 3), cos_t, sin_t))

        x = _outproj((ya_c, yb_c, yc_c, yd_c), (ya_d, yb_d, yc_d, yd_d), x, mod, w_out_b, l)
        x = _ffn(x, mod, gain2, w_ffn_in_b, w_ffn_out_b, l)

        u_ctx = u[:N_CTX]
        kv = u_ctx[:, D_KV:D_KV + 256].reshape(BATCH, SEQ, GQA_KV_HEADS, 2, GQA_HEAD_DIM)
        per_layer.append((
            u_ctx[:, B_K:B_K + 512].reshape(BATCH, SEQ, DIFF_HEADS, 2, 64),
            u_ctx[:, B_V:B_V + 512].reshape(BATCH, SEQ, DIFF_HEADS, 128),
            kn.transpose(0, 2, 1, 3),
            kv[:, :, :, 1],
            ssm_fin,
            c_fin,
            n_fin[:, :, :, 0, :],
            m_fin[:, :, :, 0, 0],
        ))

    y_prompt = _final_norm(x, norm_f.reshape(1, D_MODEL), 0, N_CTX).reshape(BATCH, SEQ, D_MODEL)
    y_sample = _final_norm(x, norm_f.reshape(1, D_MODEL), N_CTX, N_DEC).reshape(DEC_BATCH, DEC_SEQ, D_MODEL)
    stacked = tuple(jnp.stack([s[i] for s in per_layer], axis=1) for i in range(8))
    return (y_prompt, y_sample) + stacked
```

```python
import functools
import math

import numpy as np
import jax
import jax.numpy as jnp
from jax import lax
from jax.experimental import pallas as pl
from jax.experimental.pallas import tpu as pltpu

F32 = jnp.float32
BF16 = jnp.bfloat16

D_MODEL = 1024
BATCH = 16
SEQ = 256
DEPTH = 4
DEC_BATCH = 2
DEC_SEQ = 2048
PAST_LEN = 256
GRID_W = 64
GROUP_W = 512
D_MIX = 2048
CHUNK = 128
Q_BLOCK = 128
D_CONV = 5
ROPE_THETA = 10000.0
EPS = 1e-6
SSM_HEADS = 8
SSM_HEAD_DIM = 64
SSM_STATE = 64
SSM_CONV_CH = 768
DIFF_HEADS = 4
MLSTM_HEADS = 4
MLSTM_HEAD_DIM = 128
GQA_KV_HEADS = 2
GQA_HEAD_DIM = 64
IN_COLS = 5664
D_FF = 2816

N_CTX = BATCH * SEQ
N_DEC = DEC_BATCH * DEC_SEQ
N_TOK = N_CTX + N_DEC

U_COLS = 6144
C_QK, C_V, C_O = 0, 1024, 1536
A_Z = 2048
B_Q, B_K, B_V = 2560, 3072, 3584
D_Q = 4096
A_XBC = 4608
D_KV = 5376
A_DT = 5632
C_G = 5760

VMEM_LIMIT_BYTES = 56 * 1024 * 1024


def _in_col_permutation():
    idx = np.full((U_COLS,), IN_COLS, np.int32)
    a0, b0, c0, d0 = 0, 1296, 2832, 4896

    def put(dst, src, n):
        idx[dst:dst + n] = np.arange(src, src + n)

    put(A_Z, a0, 512)
    put(A_XBC, a0 + 512, 768)
    put(A_DT, a0 + 1280, 16)
    put(B_Q, b0, 512)
    put(B_K, b0 + 512, 512)
    put(B_V, b0 + 1024, 512)
    put(C_QK, c0, 1024)
    put(C_V, c0 + 1024, 512)
    put(C_O, c0 + 1536, 512)
    put(C_G, c0 + 2048, 16)
    put(D_Q, d0, 512)
    for kv in range(GQA_KV_HEADS):
        put(D_KV + kv * 128, d0 + 512 + kv * 64, 64)
        put(D_KV + kv * 128 + 64, d0 + 640 + kv * 64, 64)
    return idx


_IN_PERM = _in_col_permutation()


def _runs(idx):
    out, start = [], 0
    for p in range(1, len(idx) + 1):
        pad = idx[start] == IN_COLS
        if p == len(idx) or (idx[p] == IN_COLS) != pad or (not pad and idx[p] != idx[p - 1] + 1):
            out.append((None if pad else int(idx[start]), p - start))
            start = p
    return out


_IN_SEGMENTS = _runs(_IN_PERM)


def _cparams(sem):
    return pltpu.CompilerParams(dimension_semantics=sem, vmem_limit_bytes=VMEM_LIMIT_BYTES)


def _bdot(a, b):
    return jnp.dot(a.astype(BF16), b.astype(BF16), preferred_element_type=F32)


def _bdot_t(a, b):
    return lax.dot_general(a.astype(BF16), b.astype(BF16), (((1,), (1,)), ((), ())),
                           preferred_element_type=F32)


def _split3(v):
    hi = v.astype(BF16)
    r = v - hi.astype(F32)
    mid = r.astype(BF16)
    lo = (r - mid.astype(F32)).astype(BF16)
    return hi, mid, lo


def _xdot_l(e, v):
    hi, mid, lo = _split3(v)
    return (jnp.dot(e, hi, preferred_element_type=F32) + jnp.dot(e, mid, preferred_element_type=F32)
            + jnp.dot(e, lo, preferred_element_type=F32))


def _xdot_r(v, e):
    hi, mid, lo = _split3(v)
    return (jnp.dot(hi, e, preferred_element_type=F32) + jnp.dot(mid, e, preferred_element_type=F32)
            + jnp.dot(lo, e, preferred_element_type=F32))


def _sigmoid(x):
    return 1.0 / (1.0 + jnp.exp(-x))


def _silu(x):
    return x * _sigmoid(x)


def _softplus(x):
    return jnp.maximum(x, 0.0) + jnp.log1p(jnp.exp(-jnp.abs(x)))


def _log_sigmoid(x):
    return jnp.minimum(x, 0.0) - jnp.log1p(jnp.exp(-jnp.abs(x)))


def _rms(x):
    return x * lax.rsqrt(jnp.mean(x * x, axis=-1, keepdims=True) + EPS)


def _iota(shape, dim):
    return lax.broadcasted_iota(jnp.int32, shape, dim)


def _rope(x, cos, sin_signed):
    lane = _iota(x.shape, 1)
    first = (lane % 32) < 16
    xr = jnp.where(first, pltpu.roll(x, 112, 1), pltpu.roll(x, 16, 1))
    return x * cos + xr * sin_signed


def _mod_index(tile_rows):
    n_ctx_tiles = N_CTX // tile_rows
    per_sample = DEC_SEQ // tile_rows

    def f(i):
        return jnp.where(i < n_ctx_tiles, 0, 1 + (i - n_ctx_tiles) // per_sample)

    return f


def _mod_kernel(c_ref, w_ref, b_ref, o_ref):
    c = c_ref[...]
    s = _silu(c)
    s_hi = s.astype(BF16)
    s_lo = (s - s_hi.astype(F32)).astype(BF16)
    w = w_ref[...]
    w_hi = w.astype(BF16)
    w_lo = (w - w_hi.astype(F32)).astype(BF16)
    acc = jnp.dot(s_hi, w_hi, preferred_element_type=F32)
    acc = acc + jnp.dot(s_hi, w_lo, preferred_element_type=F32)
    acc = acc + jnp.dot(s_lo, w_hi, preferred_element_type=F32)
    o_ref[...] = acc + b_ref[...]


def _modulation(cvec, w_ada, b_ada):
    tn = 1536
    n = 6 * D_MODEL
    return pl.pallas_call(
        _mod_kernel,
        grid=(DEPTH, n // tn),
        in_specs=[
            pl.BlockSpec((8, D_MODEL), lambda l, j: (0, 0)),
            pl.BlockSpec((None, D_MODEL, tn), lambda l, j: (l, 0, j)),
            pl.BlockSpec((None, 1, tn), lambda l, j: (l, 0, j)),
        ],
        out_specs=pl.BlockSpec((None, 8, tn), lambda l, j: (l, 0, j)),
        out_shape=jax.ShapeDtypeStruct((DEPTH, 8, n), F32),
        compiler_params=_cparams(("arbitrary", "arbitrary")),
        name="adaln_mod",
    )(cvec, w_ada, b_ada.reshape(DEPTH, 1, n))


ROW_SUB = 256


def _norm_mod_to(h_ref, x_ref, gain_ref, shift, scale, rows):
    def body(s, carry):
        r = pl.multiple_of(s * ROW_SUB, ROW_SUB)
        x = x_ref[pl.ds(r, ROW_SUB), :]
        h = (_rms(x) * gain_ref[...]) * (1.0 + scale) + shift
        h_ref[pl.ds(r, ROW_SUB), :] = h.astype(BF16)
        return carry

    lax.fori_loop(0, rows // ROW_SUB, body, 0)


def _inproj_kernel(x_ref, mod_ref, gain_ref, w_ref, u_ref, h_ref, *, tm):
    @pl.when(pl.program_id(1) == 0)
    def _():
        shift = mod_ref[:, 0:D_MODEL]
        scale = mod_ref[:, D_MODEL:2 * D_MODEL]
        _norm_mod_to(h_ref, x_ref, gain_ref, shift, scale, tm)

    u_ref[...] = jnp.dot(h_ref[...], w_ref[...], preferred_element_type=F32)


def _inproj(x, mod, gain, w, l):
    tm, tn = 1024, 1536
    midx = _mod_index(tm)
    return pl.pallas_call(
        functools.partial(_inproj_kernel, tm=tm),
        grid=(N_TOK // tm, U_COLS // tn),
        in_specs=[
            pl.BlockSpec((tm, D_MODEL), lambda i, j: (i, 0)),
            pl.BlockSpec((None, None, 1, 6 * D_MODEL), lambda i, j: (l, midx(i), 0, 0)),
            pl.BlockSpec((None, 1, D_MODEL), lambda i, j: (l, 0, 0)),
            pl.BlockSpec((None, D_MODEL, tn), lambda i, j: (l, 0, j)),
        ],
        out_specs=pl.BlockSpec((tm, tn), lambda i, j: (i, j)),
        out_shape=jax.ShapeDtypeStruct((N_TOK, U_COLS), F32),
        scratch_shapes=[pltpu.VMEM((tm, D_MODEL), BF16)],
        compiler_params=_cparams(("arbitrary", "arbitrary")),
        name="in_proj",
    )(x, mod, gain, w)


def _outproj_kernel(*refs, n_ctx_tiles):
    ctx_refs = refs[0:4]
    dec_refs = refs[4:8]
    x_ref, mod_ref, w_ref, o_ref = refs[8:12]
    i = pl.program_id(0)

    def compute(ys):
        acc = jnp.dot(ys[0][...], w_ref[0:512, :], preferred_element_type=F32)
        for g in range(1, 4):
            acc = acc + jnp.dot(ys[g][...], w_ref[g * 512:(g + 1) * 512, :], preferred_element_type=F32)
        gate = mod_ref[:, 2 * D_MODEL:3 * D_MODEL]
        o_ref[...] = x_ref[...] + gate * acc

    @pl.when(i < n_ctx_tiles)
    def _():
        compute(ctx_refs)

    @pl.when(i >= n_ctx_tiles)
    def _():
        compute(dec_refs)


def _outproj(ys_ctx, ys_dec, x, mod, w, l):
    tm = 512
    nct = N_CTX // tm
    midx = _mod_index(tm)
    ctx_spec = pl.BlockSpec((tm, GROUP_W), lambda i: (jnp.minimum(i, nct - 1), 0))
    dec_spec = pl.BlockSpec((tm, GROUP_W), lambda i: (jnp.maximum(i - nct, 0), 0))
    return pl.pallas_call(
        functools.partial(_outproj_kernel, n_ctx_tiles=nct),
        grid=(N_TOK // tm,),
        in_specs=[ctx_spec] * 4 + [dec_spec] * 4 + [
            pl.BlockSpec((tm, D_MODEL), lambda i: (i, 0)),
            pl.BlockSpec((None, None, 1, 6 * D_MODEL), lambda i: (l, midx(i), 0, 0)),
            pl.BlockSpec((None, D_MIX, D_MODEL), lambda i: (l, 0, 0)),
        ],
        out_specs=pl.BlockSpec((tm, D_MODEL), lambda i: (i, 0)),
        out_shape=jax.ShapeDtypeStruct((N_TOK, D_MODEL), F32),
        compiler_params=_cparams(("arbitrary",)),
        name="out_proj",
    )(*ys_ctx, *ys_dec, x, mod, w)


def _ffn_kernel(x_ref, mod_ref, gain_ref, wg_ref, wu_ref, wo_ref, o_ref, h_ref, acc_ref, *, tm, n_f):
    j = pl.program_id(1)

    def partial(h):
        gate = jnp.dot(h, wg_ref[...], preferred_element_type=F32)
        up = jnp.dot(h, wu_ref[...], preferred_element_type=F32)
        act = (_silu(gate) * up).astype(BF16)
        return jnp.dot(act, wo_ref[...], preferred_element_type=F32)

    @pl.when(j == 0)
    def _():
        shift = mod_ref[:, 3 * D_MODEL:4 * D_MODEL]
        scale = mod_ref[:, 4 * D_MODEL:5 * D_MODEL]

        def body(s, carry):
            r = pl.multiple_of(s * ROW_SUB, ROW_SUB)
            x = x_ref[pl.ds(r, ROW_SUB), :]
            h = ((_rms(x) * gain_ref[...]) * (1.0 + scale) + shift).astype(BF16)
            h_ref[pl.ds(r, ROW_SUB), :] = h
            acc_ref[pl.ds(r, ROW_SUB), :] = partial(h)
            return carry

        lax.fori_loop(0, tm // ROW_SUB, body, 0)

    @pl.when(j == n_f - 1)
    def _():
        g2 = mod_ref[:, 5 * D_MODEL:6 * D_MODEL]

        def body(s, carry):
            r = pl.multiple_of(s * ROW_SUB, ROW_SUB)
            y = acc_ref[pl.ds(r, ROW_SUB), :] + partial(h_ref[pl.ds(r, ROW_SUB), :])
            o_ref[pl.ds(r, ROW_SUB), :] = x_ref[pl.ds(r, ROW_SUB), :] + g2 * y
            return carry

        lax.fori_loop(0, tm // ROW_SUB, body, 0)


def _ffn(x, mod, gain, w_in, w_out, l):
    tm, tf = 1024, D_FF // 2
    n_f = D_FF // tf
    midx = _mod_index(tm)
    return pl.pallas_call(
        functools.partial(_ffn_kernel, tm=tm, n_f=n_f),
        grid=(N_TOK // tm, n_f),
        in_specs=[
            pl.BlockSpec((tm, D_MODEL), lambda i, j: (i, 0)),
            pl.BlockSpec((None, None, 1, 6 * D_MODEL), lambda i, j: (l, midx(i), 0, 0)),
            pl.BlockSpec((None, 1, D_MODEL), lambda i, j: (l, 0, 0)),
            pl.BlockSpec((None, D_MODEL, tf), lambda i, j: (l, 0, j)),
            pl.BlockSpec((None, D_MODEL, tf), lambda i, j: (l, 0, j + n_f)),
            pl.BlockSpec((None, tf, D_MODEL), lambda i, j: (l, j, 0)),
        ],
        out_specs=pl.BlockSpec((tm, D_MODEL), lambda i, j: (i, 0)),
        out_shape=jax.ShapeDtypeStruct((N_TOK, D_MODEL), F32),
        scratch_shapes=[pltpu.VMEM((tm, D_MODEL), BF16), pltpu.VMEM((tm, D_MODEL), F32)],
        compiler_params=_cparams(("arbitrary", "arbitrary")),
        name="ffn",
    )(x, mod, gain, w_in, w_in, w_out)


def _final_norm_kernel(x_ref, g_ref, o_ref):
    o_ref[...] = _rms(x_ref[...]) * g_ref[...]


def _final_norm(x, gain, row0, rows):
    tm = 512
    off = row0 // tm
    return pl.pallas_call(
        _final_norm_kernel,
        grid=(rows // tm,),
        in_specs=[pl.BlockSpec((tm, D_MODEL), lambda i: (i + off, 0)),
                  pl.BlockSpec((1, D_MODEL), lambda i: (0, 0))],
        out_specs=pl.BlockSpec((tm, D_MODEL), lambda i: (i, 0)),
        out_shape=jax.ShapeDtypeStruct((rows, D_MODEL), F32),
        compiler_params=_cparams(("arbitrary",)),
        name="final_norm",
    )(x, gain)


def _causal_masks():
    i = _iota((CHUNK, CHUNK), 0)
    j = _iota((CHUNK, CHUNK), 1)
    return j <= i, j >= i


def _conv_silu_chunk(pad_ref, w_ref, b_ref, r):
    win = pad_ref[pl.ds(r, CHUNK + 16), :]
    acc = b_ref[...] + w_ref[0:1, :] * win[6:6 + CHUNK, :]
    for tap in range(1, D_CONV):
        acc = acc + w_ref[tap:tap + 1, :] * win[6 + tap:6 + tap + CHUNK, :]
    return _silu(acc)


def _fill_padded(pad_ref, src_ref, seq, width):
    zeros = jnp.zeros((8, width), F32)
    pad_ref[0:8, :] = zeros
    pad_ref[seq + 8:seq + 16, :] = zeros

    def body(c, carry):
        r = pl.multiple_of(c * CHUNK, CHUNK)
        pad_ref[pl.ds(r + 8, CHUNK), :] = src_ref[pl.ds(r, CHUNK), :]
        return carry

    lax.fori_loop(0, seq // CHUNK, body, 0)


def _ssd_kernel(*refs, seq, has_state, n_carry=0):
    (z_ref, xbc_ref, dt_ref, cw_ref, cb_ref, alog_ref, dtb_ref, dsk_ref, ng_ref) = refs[0:9]
    refs = refs[9 + n_carry:]
    if has_state:
        h0_ref, y_ref = refs[0:2]
        hfin_ref = None
    else:
        h0_ref = None
        y_ref, hfin_ref = refs[0:2]
    pad_ref, xc_ref, yf_ref, yb_ref, st_ref = refs[2:7]
    nc = seq // CHUNK
    hpg = SSM_HEADS // 2
    gw = hpg * SSM_HEAD_DIM

    _fill_padded(pad_ref, xbc_ref, seq, SSM_CONV_CH)

    def conv_body(c, carry):
        r = pl.multiple_of(c * CHUNK, CHUNK)
        xc_ref[pl.ds(r, CHUNK), :] = _conv_silu_chunk(pad_ref, cw_ref, cb_ref, r)
        return carry

    lax.fori_loop(0, nc, conv_body, 0)

    zero_blk = jnp.zeros((SSM_STATE, SSM_HEAD_DIM), F32)
    for d in range(2):
        for g in range(2):
            if has_state:
                rows = []
                for h4 in range(hpg):
                    blk = h0_ref[d, g * hpg + h4].T
                    rows.append(jnp.concatenate([blk if k == h4 else zero_blk for k in range(hpg)], axis=1))
                st_ref[d, g] = jnp.concatenate(rows, axis=0)
            else:
                st_ref[d, g] = jnp.zeros((gw, gw), F32)

    mask_f, mask_b = _causal_masks()
    masks = (mask_f, mask_b)
    tmats = (mask_f.astype(BF16), mask_b.astype(BF16))
    a_neg = -jnp.exp(alog_ref[...])
    lane_head = _iota((CHUNK, gw), 1) // SSM_HEAD_DIM
    blk_diag = (_iota((gw, gw), 0) // SSM_STATE) == (_iota((gw, gw), 1) // SSM_HEAD_DIM)

    def step(items):
        xcs = [xc_ref[pl.ds(r, CHUNK), :] for d, r in items]
        dts = [_softplus(dt_ref[pl.ds(r, CHUNK), :] + dtb_ref[...]) for d, r in items]
        cum_c = [_xdot_l(tmats[d], dts[n] * a_neg) for n, (d, r) in enumerate(items)]
        gms = [[_bdot_t(xcs[n][:, 640 + g * 64:640 + (g + 1) * 64], xcs[n][:, 512 + g * 64:512 + (g + 1) * 64])
                for g in range(2)] for n in range(len(items))]
        lhs_y, rhs_y, lhs_s, xgs, sts, elcols = [], [], [], [], [], []
        for n, (d, r) in enumerate(items):
            cum_r = cum_c[n].T
            dt_r = dts[n].T
            last = cum_c[n][CHUNK - 1:CHUNK, :] if d == 0 else cum_c[n][0:1, :]
            e_last = jnp.exp(last)
            for g in range(2):
                xg = xcs[n][:, g * gw:(g + 1) * gw]
                bt = xcs[n][:, 512 + g * 64:512 + (g + 1) * 64].T
                cm = xcs[n][:, 640 + g * 64:640 + (g + 1) * 64]
                s_l, cec_l, btw_l, xm_l, el_l = [], [], [], [], []
                for h4 in range(hpg):
                    col = d * SSM_HEADS + g * hpg + h4
                    cc = jnp.broadcast_to(cum_c[n][:, col:col + 1], (CHUNK, CHUNK))
                    cr = cum_r[col:col + 1, :]
                    dtr = dt_r[col:col + 1, :]
                    decay = jnp.exp(jnp.where(masks[d], cc - cr, -jnp.inf))
                    s_l.append((decay * gms[n][g] * dtr).astype(BF16))
                    cec_l.append((cm * jnp.exp(cc[:, 0:SSM_STATE])).astype(BF16))
                    btw_l.append((bt * (jnp.exp(last[:, col:col + 1] - cr) * dtr)).astype(BF16))
                    xm_l.append(jnp.where(lane_head == h4, xg, 0.0).astype(BF16))
                    el_l.append(jnp.broadcast_to(e_last[:, col:col + 1], (SSM_STATE, gw)))
                st = st_ref[d, g]
                lhs_y.append(jnp.concatenate(s_l + cec_l, axis=1))
                rhs_y.append(jnp.concatenate(xm_l + [st.astype(BF16)], axis=0))
                lhs_s.append(jnp.concatenate(btw_l, axis=0))
                xgs.append(xg.astype(BF16))
                sts.append(st)
                elcols.append(jnp.concatenate(el_l, axis=0))
        ys = [jnp.dot(lhs_y[k], rhs_y[k], preferred_element_type=F32) for k in range(len(lhs_y))]
        css = [jnp.dot(lhs_s[k], xgs[k], preferred_element_type=F32) for k in range(len(lhs_s))]
        outs = []
        for n, (d, r) in enumerate(items):
            for g in range(2):
                k = 2 * n + g
                st_ref[d, g] = elcols[k] * sts[k] + jnp.where(blk_diag, css[k], 0.0)
            outs.append(jnp.concatenate([ys[2 * n], ys[2 * n + 1]], axis=1))
        return outs

    def scan_body(s, carry):
        rf = pl.multiple_of(s * CHUNK, CHUNK)
        rb = pl.multiple_of((nc - 1 - s) * CHUNK, CHUNK)
        y_f, y_b = step([(0, rf), (1, rb)])
        yf_ref[pl.ds(rf, CHUNK), :] = y_f
        yb_ref[pl.ds(rb, CHUNK), :] = y_b
        return carry

    lax.fori_loop(0, nc, scan_body, 0)

    def out_body(c, carry):
        r = pl.multiple_of(c * CHUNK, CHUNK)
        x = xc_ref[pl.ds(r, CHUNK), 0:GROUP_W]
        y = yf_ref[pl.ds(r, CHUNK), :] + yb_ref[pl.ds(r, CHUNK), :] + dsk_ref[...] * x
        y = y * _silu(z_ref[pl.ds(r, CHUNK), :])
        y_ref[pl.ds(r, CHUNK), :] = (_rms(y) * ng_ref[...]).astype(BF16)
        return carry

    lax.fori_loop(0, nc, out_body, 0)

    if hfin_ref is not None:
        for d in range(2):
            for g in range(2):
                st = st_ref[d, g]
                for h4 in range(hpg):
                    lo = h4 * SSM_STATE
                    hfin_ref[d, g * hpg + h4] = st[lo:lo + SSM_STATE, lo:lo + SSM_HEAD_DIM].T


def _carried(carry):
    if carry is None:
        return [], []
    return [pl.BlockSpec(memory_space=pl.ANY)] * len(carry), list(carry)


def _mixer_ssd(u, row0, nb, seq, conv_w, conv_b, a_log, dt_bias, d_skip, norm_g, h0, layer=0, carry=None):
    has_state = h0 is not None
    roff = row0 // seq
    vec = lambda n: pl.BlockSpec((1, n), lambda b: (0, 0))
    in_specs = [
        pl.BlockSpec((seq, GROUP_W), lambda b: (b + roff, A_Z // GROUP_W)),
        pl.BlockSpec((seq, SSM_CONV_CH), lambda b: (b + roff, A_XBC // SSM_CONV_CH)),
        pl.BlockSpec((seq, 128), lambda b: (b + roff, A_DT // 128)),
        pl.BlockSpec((8, SSM_CONV_CH), lambda b: (0, 0)),
        vec(SSM_CONV_CH), vec(128), vec(128), vec(GROUP_W), vec(GROUP_W),
    ]
    args = [u, u, u, conv_w, conv_b, a_log, dt_bias, d_skip, norm_g]
    st_block = (None, 2, SSM_HEADS, SSM_HEAD_DIM, SSM_STATE)
    y_spec = pl.BlockSpec((seq, GROUP_W), lambda b: (b, 0))
    y_shape = jax.ShapeDtypeStruct((nb * seq, GROUP_W), BF16)
    aliases = {}
    n_carry = 0
    if has_state:
        in_specs.append(pl.BlockSpec(st_block, lambda b: (b, 0, 0, 0, 0)))
        args.append(h0)
        out_specs, out_shape = y_spec, y_shape
    else:
        c_specs, c_args = _carried(carry)
        n_carry = len(c_args)
        aliases = {len(args) + k: 1 + k for k in range(n_carry)}
        in_specs += c_specs
        args += c_args
        out_specs = [y_spec, pl.BlockSpec((None,) + st_block, lambda b: (b, layer, 0, 0, 0, 0))]
        out_shape = [y_shape, jax.ShapeDtypeStruct((nb, DEPTH, 2, SSM_HEADS, SSM_HEAD_DIM, SSM_STATE), F32)]
    return pl.pallas_call(
        functools.partial(_ssd_kernel, seq=seq, has_state=has_state, n_carry=n_carry),
        grid=(nb,),
        in_specs=in_specs,
        out_specs=out_specs,
        out_shape=out_shape,
        input_output_aliases=aliases,
        scratch_shapes=[
            pltpu.VMEM((seq + 16, SSM_CONV_CH), F32),
            pltpu.VMEM((seq, SSM_CONV_CH), F32),
            pltpu.VMEM((seq, GROUP_W), F32),
            pltpu.VMEM((seq, GROUP_W), F32),
            pltpu.VMEM((2, 2, GROUP_W // 2, GROUP_W // 2), F32),
        ],
        compiler_params=_cparams(("arbitrary",)),
        name="mixer_ssd_dec" if has_state else "mixer_ssd_ctx",
    )(*args)


def _mlstm_kernel(*refs, seq, has_state, n_carry=0):
    (q_ref, k_ref, v_ref, o_ref, g_ref, cwq_ref, cwk_ref, cbq_ref, cbk_ref, gb_ref, ng_ref) = refs[0:11]
    refs = refs[11 + n_carry:]
    if has_state:
        c0_ref, n0_ref, m0_ref, y_ref = refs[0:4]
        outs = None
    else:
        y_ref = refs[0]
        outs = refs[1:4]
    (qpad_ref, kpad_ref, qc_ref, hf_ref, hb_ref, a_ref, cl_ref, rs_ref, rm_ref, cu_ref, rows_ref,
     cst_ref, nst_ref, mst_ref) = refs[4:]
    nc = seq // CHUNK
    head = pl.program_id(1)

    _fill_padded(qpad_ref, q_ref, seq, MLSTM_HEAD_DIM)
    _fill_padded(kpad_ref, k_ref, seq, MLSTM_HEAD_DIM)

    mask_f, mask_b = _causal_masks()
    row_id = _iota((CHUNK, CHUNK), 0)
    full = (CHUNK, CHUNK)

    sel = jnp.concatenate(
        [(row_id == col).astype(BF16) for col in (head, 4 + head, 8 + head, 12 + head)], axis=1)
    tmats = (mask_f.astype(BF16), mask_b.astype(BF16))
    masks = (mask_f, mask_b)

    def local_body(s, carry):
        chunks = (2 * s, 2 * s + 1)
        qs, ks, vs, qks, gsel = [], [], [], [], []
        for c in chunks:
            r = pl.multiple_of(c * CHUNK, CHUNK)
            q = _conv_silu_chunk(qpad_ref, cwq_ref, cbq_ref, r) * (MLSTM_HEAD_DIM ** -0.5)
            k = _conv_silu_chunk(kpad_ref, cwk_ref, cbk_ref, r)
            qc_ref[pl.ds(r, CHUNK), :] = q
            qs.append(q)
            ks.append(k)
            vs.append(v_ref[pl.ds(r, CHUNK), :])
            gsel.append(_xdot_r(g_ref[pl.ds(r, CHUNK), :] + gb_ref[...], sel))
        for i in range(2):
            qks.append(_bdot_t(qs[i], ks[i]))
        items = [(i, d) for i in range(2) for d in range(2)]
        li_c = [gsel[i][:, 256 * d:256 * d + 128] for i, d in items]
        lf_c = [_log_sigmoid(gsel[i][:, 256 * d + 128:256 * d + 256]) for i, d in items]
        cum_c = [_xdot_l(tmats[d], lf_c[n]) for n, (i, d) in enumerate(items)]
        cum_r = [x.T for x in cum_c]
        li_r = [x.T for x in li_c]
        last = [cum_c[n][CHUNK - 1:CHUNK, :] if d == 0 else cum_c[n][0:1, :]
                for n, (i, d) in enumerate(items)]
        dmat = [jnp.where(masks[d], cum_c[n] - cum_r[n] + li_r[n], -jnp.inf) for n, (i, d) in enumerate(items)]
        rowmax = [jnp.max(x, axis=1, keepdims=True) for x in dmat]
        sp = [qks[i] * jnp.exp(dmat[n] - rowmax[n]) for n, (i, d) in enumerate(items)]
        m_loc = [jnp.max(last[n] - cum_r[n][0:1, :] + li_r[n][0:1, :], axis=1, keepdims=True)
                 for n in range(4)]
        kw = [ks[i] * jnp.exp(last[n] - cum_c[n] + li_c[n] - m_loc[n]) for n, (i, d) in enumerate(items)]
        kwt = [x.T for x in kw]
        a_loc = [_bdot(sp[n], vs[i]) for n, (i, d) in enumerate(items)]
        c_loc = [_bdot(kwt[n], vs[i]) for n, (i, d) in enumerate(items)]
        for n, (i, d) in enumerate(items):
            c = chunks[i]
            a_ref[c, d] = a_loc[n]
            cl_ref[c, d] = c_loc[n]
            rs_ref[c, d] = jnp.broadcast_to(jnp.sum(sp[n], axis=1, keepdims=True), full)
            rm_ref[c, d] = jnp.broadcast_to(rowmax[n], full)
            cu_ref[c, d] = cum_c[n]
            rows_ref[c, d, 0:1, :] = jnp.sum(kw[n], axis=0, keepdims=True)
            rows_ref[c, d, 1:2, :] = jnp.broadcast_to(m_loc[n], (1, CHUNK))
            rows_ref[c, d, 2:3, :] = last[n]
        return carry

    lax.fori_loop(0, nc // 2, local_body, 0)

    if has_state:
        cst_ref[...] = c0_ref[...]
        nst_ref[...] = n0_ref[...]
        mst_ref[...] = m0_ref[...]
    else:
        cst_ref[...] = jnp.zeros(cst_ref.shape, F32)
        nst_ref[...] = jnp.zeros(nst_ref.shape, F32)
        mst_ref[...] = jnp.zeros(mst_ref.shape, F32)

    def state_step(d, c):
        r = pl.multiple_of(c * CHUNK, CHUNK)
        q = qc_ref[pl.ds(r, CHUNK), :]
        c_p = cst_ref[d]
        n_p = nst_ref[d]
        m_p = mst_ref[d]
        n_loc = rows_ref[c, d, 0:1, :]
        m_loc = rows_ref[c, d, 1:2, :]
        last = rows_ref[c, d, 2:3, :]
        rowmax = rm_ref[c, d]
        inter = cu_ref[c, d] + m_p
        m_t = jnp.maximum(inter, rowmax)
        f_intra = jnp.exp(rowmax - m_t)
        w_inter = jnp.exp(inter - m_t)
        num = a_ref[c, d] * f_intra + w_inter * _bdot(q, c_p)
        den = rs_ref[c, d] * f_intra + w_inter * jnp.sum(q * n_p, axis=1, keepdims=True)
        hh = num / jnp.maximum(jnp.abs(den), jnp.exp(-m_t))
        m_new = jnp.maximum(last + m_p, m_loc)
        s_p = jnp.exp(last + m_p - m_new)
        s_l = jnp.exp(m_loc - m_new)
        cst_ref[d] = s_p[:, 0:1] * c_p + s_l[:, 0:1] * cl_ref[c, d]
        nst_ref[d] = s_p * n_p + s_l * n_loc
        mst_ref[d] = m_new
        return r, hh

    def state_body(s, carry):
        r, hh = state_step(0, s)
        hf_ref[pl.ds(r, CHUNK), :] = hh
        r, hh = state_step(1, nc - 1 - s)
        hb_ref[pl.ds(r, CHUNK), :] = hh
        return carry

    lax.fori_loop(0, nc, state_body, 0, unroll=2)

    def out_body(c, carry):
        r = pl.multiple_of(c * CHUNK, CHUNK)
        hsum = hf_ref[pl.ds(r, CHUNK), :] + hb_ref[pl.ds(r, CHUNK), :]
        y = _sigmoid(o_ref[pl.ds(r, CHUNK), :]) * (_rms(hsum) * ng_ref[...])
        y_ref[pl.ds(r, CHUNK), :] = y.astype(BF16)
        return carry

    lax.fori_loop(0, nc, out_body, 0, unroll=2)

    if outs is not None:
        outs[0][...] = cst_ref[...]
        outs[1][...] = nst_ref[...]
        outs[2][...] = mst_ref[...]


def _mixer_mlstm(u, row0, nb, seq, conv_w, conv_b, gate_b, norm_g, state, layer=0, carry=None):
    has_state = state is not None
    roff = row0 // seq
    hd = MLSTM_HEAD_DIM
    nc = seq // CHUNK
    col = lambda base: (lambda b, h: (b + roff, base // hd + h))
    in_specs = [
        pl.BlockSpec((seq, hd), col(C_QK)),
        pl.BlockSpec((seq, hd), col(C_QK + GROUP_W)),
        pl.BlockSpec((seq, hd), col(C_V)),
        pl.BlockSpec((seq, hd), col(C_O)),
        pl.BlockSpec((seq, 128), lambda b, h: (b + roff, C_G // 128)),
        pl.BlockSpec((8, hd), lambda b, h: (0, h)),
        pl.BlockSpec((8, hd), lambda b, h: (0, MLSTM_HEADS + h)),
        pl.BlockSpec((1, hd), lambda b, h: (0, h)),
        pl.BlockSpec((1, hd), lambda b, h: (0, MLSTM_HEADS + h)),
        pl.BlockSpec((1, 128), lambda b, h: (0, 0)),
        pl.BlockSpec((1, hd), lambda b, h: (0, h)),
    ]
    args = [u, u, u, u, u, conv_w, conv_w, conv_b, conv_b, gate_b, norm_g]
    c_spec = pl.BlockSpec((None, 2, None, hd, hd), lambda b, h: (b, 0, h, 0, 0))
    n_spec = pl.BlockSpec((None, 2, None, 1, hd), lambda b, h: (b, 0, h, 0, 0))
    y_spec = pl.BlockSpec((seq, hd), lambda b, h: (b, h))
    y_shape = jax.ShapeDtypeStruct((nb * seq, GROUP_W), BF16)
    aliases = {}
    n_carry = 0
    if has_state:
        in_specs += [c_spec, n_spec, n_spec]
        args += list(state)
        out_specs, out_shape = y_spec, y_shape
    else:
        c_specs, c_args = _carried(carry)
        n_carry = len(c_args)
        aliases = {len(args) + k: 1 + k for k in range(n_carry)}
        in_specs += c_specs
        args += c_args
        c_all = pl.BlockSpec((None, None, 2, None, hd, hd), lambda b, h: (b, layer, 0, h, 0, 0))
        n_all = pl.BlockSpec((None, None, 2, None, 1, hd), lambda b, h: (b, layer, 0, h, 0, 0))
        out_specs = [y_spec, c_all, n_all, n_all]
        out_shape = [y_shape,
                     jax.ShapeDtypeStruct((nb, DEPTH, 2, MLSTM_HEADS, hd, hd), F32),
                     jax.ShapeDtypeStruct((nb, DEPTH, 2, MLSTM_HEADS, 1, hd), F32),
                     jax.ShapeDtypeStruct((nb, DEPTH, 2, MLSTM_HEADS, 1, hd), F32)]
    return pl.pallas_call(
        functools.partial(_mlstm_kernel, seq=seq, has_state=has_state, n_carry=n_carry),
        grid=(nb, MLSTM_HEADS),
        in_specs=in_specs,
        out_specs=out_specs,
        out_shape=out_shape,
        input_output_aliases=aliases,
        scratch_shapes=[
            pltpu.VMEM((seq + 16, hd), F32), pltpu.VMEM((seq + 16, hd), F32),
            pltpu.VMEM((seq, hd), F32), pltpu.VMEM((seq, hd), F32), pltpu.VMEM((seq, hd), F32),
            pltpu.VMEM((nc, 2, hd, hd), F32), pltpu.VMEM((nc, 2, hd, hd), F32),
            pltpu.VMEM((nc, 2, hd, hd), F32), pltpu.VMEM((nc, 2, hd, hd), F32), pltpu.VMEM((nc, 2, hd, hd), F32),
            pltpu.VMEM((nc, 2, 8, hd), F32),
            pltpu.VMEM((2, hd, hd), F32), pltpu.VMEM((2, 1, hd), F32), pltpu.VMEM((2, 1, hd), F32),
        ],
        compiler_params=_cparams(("arbitrary", "arbitrary")),
        name="mixer_mlstm_dec" if has_state else "mixer_mlstm_ctx",
    )(*args)


KV_SUB = 256


ATT_SUB = 256
DIFF_QB = 512
GQA_QB = 256


def _softmax_parts(s):
    e = jnp.exp(s - jnp.max(s, axis=1, keepdims=True))
    return e.astype(BF16), jnp.sum(e, axis=1, keepdims=True)


def _diff_lambda(lam_ref, lam_init):
    lp = lam_ref[...]
    return (jnp.exp(jnp.sum(lp[0:1, :] * lp[1:2, :], axis=1, keepdims=True))
            - jnp.exp(jnp.sum(lp[2:3, :] * lp[3:4, :], axis=1, keepdims=True)) + lam_init)


def _softmax_parts_t(s_t):
    e = jnp.exp(s_t - jnp.max(s_t, axis=0, keepdims=True))
    return e.astype(BF16), jnp.sum(e, axis=0, keepdims=True)


def _diff_dec_kernel(q_ref, k_ref, v_ref, lam_ref, ck_ref, cv_ref, cosq_ref, sinq_ref, cosk_ref, sink_ref,
                     o_ref, k1_ref, k2_ref, vt_ref, *, seq, lam_init):
    past = PAST_LEN

    @pl.when(pl.program_id(2) == 0)
    def _():
        ck = ck_ref[...]
        k1_ref[0:past, :] = ck[:, 0:64].astype(BF16)
        k2_ref[0:past, :] = ck[:, 64:128].astype(BF16)
        vt_ref[:, 0:past] = cv_ref[...].T.astype(BF16)
        for s in range(seq // KV_SUB):
            r = s * KV_SUB
            kk = _rope(k_ref[r:r + KV_SUB, :], cosk_ref[r:r + KV_SUB, :], sink_ref[r:r + KV_SUB, :])
            k1_ref[past + r:past + r + KV_SUB, :] = kk[:, 0:64].astype(BF16)
            k2_ref[past + r:past + r + KV_SUB, :] = kk[:, 64:128].astype(BF16)
            vt_ref[:, past + r:past + r + KV_SUB] = v_ref[r:r + KV_SUB, :].T.astype(BF16)

    lam = _diff_lambda(lam_ref, lam_init)
    q = _rope(q_ref[...], cosq_ref[...], sinq_ref[...]) * (64 ** -0.5)
    n_sub = q.shape[0] // ATT_SUB
    k_maps = (k1_ref[...], k2_ref[...])
    q_t = [q[j * ATT_SUB:(j + 1) * ATT_SUB, m * 64:(m + 1) * 64].T.astype(BF16)
           for j in range(n_sub) for m in range(2)]
    scores = [jnp.dot(k_maps[n % 2], q_t[n], preferred_element_type=F32) for n in range(2 * n_sub)]
    parts = [_softmax_parts_t(s) for s in scores]
    vt = vt_ref[...]
    pv = [jnp.dot(vt, e, preferred_element_type=F32) / l for e, l in parts]
    for j in range(n_sub):
        o = (pv[2 * j] - lam * pv[2 * j + 1]).T
        o_ref[j * ATT_SUB:(j + 1) * ATT_SUB, :] = (_rms(o) * (1.0 - lam_init)).astype(BF16)


def _diff_ctx_kernel(*refs, lam_init, n_carry=0):
    q_ref, k_ref, v_ref, lam_ref = refs[0:4]
    o_ref, ko_ref, vo_ref = refs[4 + n_carry:]
    ko_ref[...] = k_ref[...]
    vo_ref[...] = v_ref[...]
    lam = _diff_lambda(lam_ref, lam_init)
    q = q_ref[...] * (64 ** -0.5)
    k = k_ref[...].astype(BF16)
    v = v_ref[...].astype(BF16)
    cols = [h * 128 + m * 64 for h in range(DIFF_HEADS) for m in range(2)]
    scores = [_bdot_t(q[:, c0:c0 + 64], k[:, c0:c0 + 64]) for c0 in cols]
    parts = [_softmax_parts(s) for s in scores]
    pv = [jnp.dot(parts[n][0], v[:, (n // 2) * 128:(n // 2 + 1) * 128], preferred_element_type=F32)
          for n in range(2 * DIFF_HEADS)]
    outs = []
    for h in range(DIFF_HEADS):
        o = pv[2 * h] / parts[2 * h][1] - lam * (pv[2 * h + 1] / parts[2 * h + 1][1])
        outs.append(_rms(o) * (1.0 - lam_init))
    o_ref[...] = jnp.concatenate(outs, axis=1).astype(BF16)


def _mixer_diff(u, row0, nb, seq, lam_rows, lam_init, ctx, layer=0, carry=None):
    roff = row0 // seq
    o_shape = jax.ShapeDtypeStruct((nb * seq, GROUP_W), BF16)
    lam_spec3 = pl.BlockSpec((8, 128), lambda b, h, i: (0, 0))
    if ctx is None:
        col = lambda base: pl.BlockSpec((seq, GROUP_W), lambda b: (b + roff, base // GROUP_W))
        c_specs, c_args = _carried(carry)
        slab = pl.BlockSpec((None, None, seq, GROUP_W), lambda b: (b, layer, 0, 0))
        slab_shape = jax.ShapeDtypeStruct((nb, DEPTH, seq, GROUP_W), F32)
        return pl.pallas_call(
            functools.partial(_diff_ctx_kernel, lam_init=lam_init, n_carry=len(c_args)),
            grid=(nb,),
            in_specs=[col(B_Q), col(B_K), col(B_V), pl.BlockSpec((8, 128), lambda b: (0, 0))] + c_specs,
            out_specs=[pl.BlockSpec((seq, GROUP_W), lambda b: (b, 0)), slab, slab],
            out_shape=[o_shape, slab_shape, slab_shape],
            input_output_aliases={4 + k: 1 + k for k in range(len(c_args))},
            compiler_params=_cparams(("arbitrary",)),
            name="mixer_diff_ctx",
        )(u, u, u, lam_rows, *c_args)
    ck, cv, cos, sin = ctx
    qb = min(DIFF_QB, seq)
    qoff = row0 // qb
    nq = seq // qb
    lk = seq + PAST_LEN
    in_specs = [
        pl.BlockSpec((qb, 128), lambda b, h, i: (qoff + b * nq + i, B_Q // 128 + h)),
        pl.BlockSpec((seq, 128), lambda b, h, i: (b + roff, B_K // 128 + h)),
        pl.BlockSpec((seq, 128), lambda b, h, i: (b + roff, B_V // 128 + h)),
        lam_spec3,
        pl.BlockSpec((None, PAST_LEN, 128), lambda b, h, i: (b, 0, h)),
        pl.BlockSpec((None, PAST_LEN, 128), lambda b, h, i: (b, 0, h)),
        pl.BlockSpec((qb, 128), lambda b, h, i: (i, 0)),
        pl.BlockSpec((qb, 128), lambda b, h, i: (i, 0)),
        pl.BlockSpec((seq, 128), lambda b, h, i: (0, 0)),
        pl.BlockSpec((seq, 128), lambda b, h, i: (0, 0)),
    ]
    return pl.pallas_call(
        functools.partial(_diff_dec_kernel, seq=seq, lam_init=lam_init),
        grid=(nb, DIFF_HEADS, nq),
        in_specs=in_specs,
        out_specs=pl.BlockSpec((qb, 128), lambda b, h, i: (b * nq + i, h)),
        out_shape=o_shape,
        scratch_shapes=[pltpu.VMEM((lk, 64), BF16), pltpu.VMEM((lk, 64), BF16), pltpu.VMEM((128, lk), BF16)],
        compiler_params=_cparams(("arbitrary", "arbitrary", "arbitrary")),
        name="mixer_diff_dec",
    )(u, u, u, lam_rows, ck, cv, cos, sin, cos, sin)


def _seg_rms_pair(x, gain):
    low = _iota(x.shape, 1) < 64
    sq = x * x
    ms_lo = jnp.sum(jnp.where(low, sq, 0.0), axis=1, keepdims=True) * (1.0 / 64)
    ms_hi = jnp.sum(jnp.where(low, 0.0, sq), axis=1, keepdims=True) * (1.0 / 64)
    return x * lax.rsqrt(jnp.where(low, ms_lo, ms_hi) + EPS) * gain


def _gqa_attend(q_heads, kf, vf):
    rows = q_heads[0].shape[0]
    s = _bdot_t(jnp.concatenate(q_heads, axis=0), kf)
    e, l = _softmax_parts(s)
    o = jnp.dot(e, vf, preferred_element_type=F32) / l
    return [o[g * rows:(g + 1) * rows, :] for g in range(len(q_heads))]


def _gqa_dec_kernel(q_ref, kv_ref, qg_ref, kg_ref, ck_ref, cv_ref, cosq_ref, sinq_ref, cosk_ref, sink_ref,
                    o_ref, kf_ref, vt_ref, *, seq):
    past = PAST_LEN

    @pl.when(pl.program_id(2) == 0)
    def _():
        kf_ref[0:past, :] = ck_ref[...].astype(BF16)
        vt_ref[:, 0:past] = cv_ref[...].T.astype(BF16)
        for s in range(seq // KV_SUB):
            r = s * KV_SUB
            blk = kv_ref[r:r + KV_SUB, :]
            kn = _rope(_seg_rms_pair(blk, kg_ref[...]), cosk_ref[r:r + KV_SUB, :], sink_ref[r:r + KV_SUB, :])
            kf_ref[past + r:past + r + KV_SUB, :] = kn[:, 0:64].astype(BF16)
            vt_ref[:, past + r:past + r + KV_SUB] = blk[:, 64:128].T.astype(BF16)

    heads_t = []
    for half in range(2):
        qh = _seg_rms_pair(q_ref[:, half * 128:(half + 1) * 128], qg_ref[...])
        qh = _rope(qh, cosq_ref[...], sinq_ref[...]) * (64 ** -0.5)
        heads_t += [qh[:, 0:64].T.astype(BF16), qh[:, 64:128].T.astype(BF16)]
    kf = kf_ref[...]
    scores = [jnp.dot(kf, q_t, preferred_element_type=F32) for q_t in heads_t]
    parts = [_softmax_parts_t(s) for s in scores]
    vt = vt_ref[...]
    outs = [(jnp.dot(vt, e, preferred_element_type=F32) / l).T for e, l in parts]
    o_ref[...] = jnp.concatenate(outs, axis=1).astype(BF16)


def _gqa_ctx_kernel(*refs, n_carry=0):
    q_ref, kv_ref, qg_ref, kg_ref = refs[0:4]
    o_ref, kn_ref, vo_ref = refs[4 + n_carry:]
    outs = []
    for kvh in range(GQA_KV_HEADS):
        blk = kv_ref[:, kvh * 128:(kvh + 1) * 128]
        kn = _seg_rms_pair(blk, kg_ref[...])[:, 0:64]
        kn_ref[:, kvh * 64:(kvh + 1) * 64] = kn
        vo_ref[:, kvh * 64:(kvh + 1) * 64] = blk[:, 64:128]
        heads = []
        for half in range(2):
            c0 = kvh * 256 + half * 128
            qh = _seg_rms_pair(q_ref[:, c0:c0 + 128], qg_ref[...]) * (64 ** -0.5)
            heads += [qh[:, 0:64], qh[:, 64:128]]
        outs += _gqa_attend(heads, kn.astype(BF16), blk[:, 64:128].astype(BF16))
    o_ref[...] = jnp.concatenate(outs, axis=1).astype(BF16)


def _mixer_gqa(u, row0, nb, seq, q_gain, k_gain, ctx, layer=0, carry=None):
    roff = row0 // seq
    o_shape = jax.ShapeDtypeStruct((nb * seq, GROUP_W), BF16)
    if ctx is None:
        gain = pl.BlockSpec((1, 128), lambda b: (0, 0))
        c_specs, c_args = _carried(carry)
        kvw = GQA_KV_HEADS * GQA_HEAD_DIM
        slab = pl.BlockSpec((None, None, seq, kvw), lambda b: (b, layer, 0, 0))
        slab_shape = jax.ShapeDtypeStruct((nb, DEPTH, seq, kvw), F32)
        return pl.pallas_call(
            functools.partial(_gqa_ctx_kernel, n_carry=len(c_args)),
            grid=(nb,),
            in_specs=[pl.BlockSpec((seq, GROUP_W), lambda b: (b + roff, D_Q // GROUP_W)),
                      pl.BlockSpec((seq, 256), lambda b: (b + roff, D_KV // 256)), gain, gain] + c_specs,
            out_specs=[pl.BlockSpec((seq, GROUP_W), lambda b: (b, 0)), slab, slab],
            out_shape=[o_shape, slab_shape, slab_shape],
            input_output_aliases={4 + k: 1 + k for k in range(len(c_args))},
            compiler_params=_cparams(("arbitrary",)),
            name="mixer_gqa_ctx",
        )(u, u, q_gain, k_gain, *c_args)
    ck, cv, cos, sin = ctx
    qb = GQA_QB
    qoff = row0 // qb
    nq = seq // qb
    lk = seq + PAST_LEN
    in_specs = [
        pl.BlockSpec((qb, 256), lambda b, h, i: (qoff + b * nq + i, D_Q // 256 + h)),
        pl.BlockSpec((seq, 128), lambda b, h, i: (b + roff, D_KV // 128 + h)),
        pl.BlockSpec((1, 128), lambda b, h, i: (0, 0)),
        pl.BlockSpec((1, 128), lambda b, h, i: (0, 0)),
        pl.BlockSpec((None, None, PAST_LEN, 64), lambda b, h, i: (b, h, 0, 0)),
        pl.BlockSpec((None, None, PAST_LEN, 64), lambda b, h, i: (b, h, 0, 0)),
        pl.BlockSpec((qb, 128), lambda b, h, i: (i, 0)),
        pl.BlockSpec((qb, 128), lambda b, h, i: (i, 0)),
        pl.BlockSpec((seq, 128), lambda b, h, i: (0, 0)),
        pl.BlockSpec((seq, 128), lambda b, h, i: (0, 0)),
    ]
    return pl.pallas_call(
        functools.partial(_gqa_dec_kernel, seq=seq),
        grid=(nb, GQA_KV_HEADS, nq),
        in_specs=in_specs,
        out_specs=pl.BlockSpec((qb, 256), lambda b, h, i: (b * nq + i, h)),
        out_shape=o_shape,
        scratch_shapes=[pltpu.VMEM((lk, 64), BF16), pltpu.VMEM((64, lk), BF16)],
        compiler_params=_cparams(("arbitrary", "arbitrary", "arbitrary")),
        name="mixer_gqa_dec",
    )(u, u, q_gain, k_gain, ck, cv, cos, sin, cos, sin)


def _rope_tables(n_tok):
    rows = n_tok // GRID_W
    r, c = jnp.meshgrid(jnp.arange(rows, dtype=F32), jnp.arange(GRID_W, dtype=F32), indexing='ij')
    nf = 16
    freqs = ROPE_THETA ** (-jnp.arange(nf, dtype=F32) / nf)
    ang = jnp.stack([r.reshape(-1)[:, None] * freqs, c.reshape(-1)[:, None] * freqs], axis=1)
    cos, sin = jnp.cos(ang), jnp.sin(ang)
    cos64 = jnp.concatenate([cos, cos], axis=2).reshape(n_tok, 64)
    sin64 = jnp.concatenate([-sin, sin], axis=2).reshape(n_tok, 64)
    return jnp.tile(cos64, (1, 2)), jnp.tile(sin64, (1, 2))


def _pad_lanes(v, n=128):
    v = v.reshape(1, -1)
    return jnp.pad(v, ((0, 0), (0, n - v.shape[1])))


def kernel(x_prompt, x_sample, c, cache_diff_k, cache_diff_v, cache_gqa_k, cache_gqa_v, state_ssm, state_mlstm_c, state_mlstm_n, state_mlstm_m, c_ctx, w_ada, b_ada, norm1, norm2, w_in, w_out, conv_ssd_w, conv_ssd_b, ssd_a_log, ssd_dt_bias, ssd_d, ssd_norm, diff_lq1, diff_lk1, diff_lq2, diff_lk2, conv_mlstm_w, conv_mlstm_b, mlstm_gate_b, mlstm_norm, gqa_q_norm, gqa_k_norm, w_ffn_in, w_ffn_out, norm_f):
    w_in_p = jnp.zeros((DEPTH, D_MODEL, U_COLS), BF16)
    col = 0
    for s, n in _IN_SEGMENTS:
        if s is not None:
            w_in_p = lax.dynamic_update_slice(w_in_p, w_in[:, :, s:s + n].astype(BF16), (0, 0, col))
        col += n
    w_out_b = w_out.astype(BF16)
    gain1 = norm1.reshape(DEPTH, 1, D_MODEL)
    gain2 = norm2.reshape(DEPTH, 1, D_MODEL)
    w_ffn_in_b = w_ffn_in.astype(BF16)
    w_ffn_out_b = w_ffn_out.astype(BF16)
    cvec = jnp.concatenate([c_ctx[None, :], c, jnp.zeros((5, D_MODEL), F32)], axis=0)
    cos_t, sin_t = _rope_tables(DEC_SEQ)
    pad_taps = lambda w: jnp.pad(w, ((0, 8 - D_CONV), (0, 0)))

    mod = _modulation(cvec, w_ada, b_ada).reshape(DEPTH, 8, 1, 6 * D_MODEL)
    x = jnp.concatenate([x_prompt.reshape(N_CTX, D_MODEL), x_sample.reshape(N_DEC, D_MODEL)], axis=0)

    ssm_all = diff_kv_all = mlstm_all = gqa_kv_all = None
    for l in range(DEPTH):
        u = _inproj(x, mod, gain1, w_in_p, l)

        ssd_args = (pad_taps(conv_ssd_w[l]), conv_ssd_b[l].reshape(1, -1), _pad_lanes(ssd_a_log[l]),
                    _pad_lanes(ssd_dt_bias[l]), jnp.repeat(ssd_d[l], SSM_HEAD_DIM).reshape(1, GROUP_W),
                    ssd_norm[l].reshape(1, GROUP_W))
        ya_c, ssm_new = _mixer_ssd(u, 0, BATCH, SEQ, *ssd_args, None, l, ssm_all)
        ssm_all = (ssm_new,)
        ya_d = _mixer_ssd(u, N_CTX, DEC_BATCH, DEC_SEQ, *ssd_args, state_ssm[:, l])

        lam_init = 0.8 - 0.6 * math.exp(-0.3 * l)
        lam_rows = jnp.pad(jnp.stack([diff_lq1[l], diff_lk1[l], diff_lq2[l], diff_lk2[l]]), ((0, 4), (0, 64)))
        yb_c, *diff_kv_all = _mixer_diff(u, 0, BATCH, SEQ, lam_rows, lam_init, None, l, diff_kv_all)
        yb_d = _mixer_diff(u, N_CTX, DEC_BATCH, DEC_SEQ, lam_rows, lam_init,
                           (cache_diff_k[:, l].reshape(DEC_BATCH, PAST_LEN, GROUP_W),
                            cache_diff_v[:, l].reshape(DEC_BATCH, PAST_LEN, GROUP_W), cos_t, sin_t))

        ml_args = (pad_taps(conv_mlstm_w[l]), conv_mlstm_b[l].reshape(1, -1), _pad_lanes(mlstm_gate_b[l]),
                   mlstm_norm[l].reshape(1, GROUP_W))
        yc_c, *mlstm_all = _mixer_mlstm(u, 0, BATCH, SEQ, *ml_args, None, l, mlstm_all)
        m0 = jnp.broadcast_to(state_mlstm_m[:, l][..., None, None], (DEC_BATCH, 2, MLSTM_HEADS, 1, MLSTM_HEAD_DIM))
        yc_d = _mixer_mlstm(u, N_CTX, DEC_BATCH, DEC_SEQ, *ml_args,
                            (state_mlstm_c[:, l], state_mlstm_n[:, l][:, :, :, None, :], m0))

        q_gain = jnp.tile(gqa_q_norm[l], 2).reshape(1, 128)
        k_gain = jnp.tile(gqa_k_norm[l], 2).reshape(1, 128)
        yd_c, *gqa_kv_all = _mixer_gqa(u, 0, BATCH, SEQ, q_gain, k_gain, None, l, gqa_kv_all)
        yd_d = _mixer_gqa(u, N_CTX, DEC_BATCH, DEC_SEQ, q_gain, k_gain,
                          (cache_gqa_k[:, l].transpose(0, 2, 1, 3), cache_gqa_v[:, l].transpose(0, 2, 1, 3), cos_t, sin_t))

        x = _outproj((ya_c, yb_c, yc_c, yd_c), (ya_d, yb_d, yc_d, yd_d), x, mod, w_out_b, l)
        x = _ffn(x, mod, gain2, w_ffn_in_b, w_ffn_out_b, l)

    y_prompt = _final_norm(x, norm_f.reshape(1, D_MODEL), 0, N_CTX).reshape(BATCH, SEQ, D_MODEL)
    y_sample = _final_norm(x, norm_f.reshape(1, D_MODEL), N_CTX, N_DEC).reshape(DEC_BATCH, DEC_SEQ, D_MODEL)
    c_all, n_all, m_all = mlstm_all
    return (y_prompt, y_sample,
            diff_kv_all[0].reshape(BATCH, DEPTH, SEQ, DIFF_HEADS, 2, 64),
            diff_kv_all[1].reshape(BATCH, DEPTH, SEQ, DIFF_HEADS, 128),
            gqa_kv_all[0].reshape(BATCH, DEPTH, SEQ, GQA_KV_HEADS, GQA_HEAD_DIM),
            gqa_kv_all[1].reshape(BATCH, DEPTH, SEQ, GQA_KV_HEADS, GQA_HEAD_DIM),
            ssm_all[0], c_all, n_all[:, :, :, :, 0, :], m_all[:, :, :, :, 0, 0])
```

```python
import functools
import math

import numpy as np
import jax
import jax.numpy as jnp
from jax import lax
from jax.experimental import pallas as pl
from jax.experimental.pallas import tpu as pltpu

F32 = jnp.float32
BF16 = jnp.bfloat16

D_MODEL = 1024
BATCH = 16
SEQ = 256
DEPTH = 4
DEC_BATCH = 2
DEC_SEQ = 2048
PAST_LEN = 256
GRID_W = 64
GROUP_W = 512
D_MIX = 2048
CHUNK = 128
Q_BLOCK = 128
D_CONV = 5
ROPE_THETA = 10000.0
EPS = 1e-6
SSM_HEADS = 8
SSM_HEAD_DIM = 64
SSM_STATE = 64
SSM_CONV_CH = 768
DIFF_HEADS = 4
MLSTM_HEADS = 4
MLSTM_HEAD_DIM = 128
GQA_KV_HEADS = 2
GQA_HEAD_DIM = 64
IN_COLS = 5664
D_FF = 2816

N_CTX = BATCH * SEQ
N_DEC = DEC_BATCH * DEC_SEQ
N_TOK = N_CTX + N_DEC

U_COLS = 6144
C_QK, C_V, C_O = 0, 1024, 1536
A_Z = 2048
B_Q, B_K, B_V = 2560, 3072, 3584
D_Q = 4096
A_XBC = 4608
D_KV = 5376
A_DT = 5632
C_G = 5760

VMEM_LIMIT_BYTES = 56 * 1024 * 1024


def _in_col_permutation():
    idx = np.full((U_COLS,), IN_COLS, np.int32)
    a0, b0, c0, d0 = 0, 1296, 2832, 4896

    def put(dst, src, n):
        idx[dst:dst + n] = np.arange(src, src + n)

    put(A_Z, a0, 512)
    put(A_XBC, a0 + 512, 768)
    put(A_DT, a0 + 1280, 16)
    put(B_Q, b0, 512)
    put(B_K, b0 + 512, 512)
    put(B_V, b0 + 1024, 512)
    put(C_QK, c0, 1024)
    put(C_V, c0 + 1024, 512)
    put(C_O, c0 + 1536, 512)
    put(C_G, c0 + 2048, 16)
    put(D_Q, d0, 512)
    for kv in range(GQA_KV_HEADS):
        put(D_KV + kv * 128, d0 + 512 + kv * 64, 64)
        put(D_KV + kv * 128 + 64, d0 + 640 + kv * 64, 64)
    return idx


_IN_PERM = _in_col_permutation()


def _runs(idx):
    out, start = [], 0
    for p in range(1, len(idx) + 1):
        pad = idx[start] == IN_COLS
        if p == len(idx) or (idx[p] == IN_COLS) != pad or (not pad and idx[p] != idx[p - 1] + 1):
            out.append((None if pad else int(idx[start]), p - start))
            start = p
    return out


_IN_SEGMENTS = _runs(_IN_PERM)


def _cparams(sem):
    return pltpu.CompilerParams(dimension_semantics=sem, vmem_limit_bytes=VMEM_LIMIT_BYTES)


def _bdot(a, b):
    return jnp.dot(a.astype(BF16), b.astype(BF16), preferred_element_type=F32)


def _bdot_t(a, b):
    return lax.dot_general(a.astype(BF16), b.astype(BF16), (((1,), (1,)), ((), ())),
                           preferred_element_type=F32)


def _split3(v):
    hi = v.astype(BF16)
    r = v - hi.astype(F32)
    mid = r.astype(BF16)
    lo = (r - mid.astype(F32)).astype(BF16)
    return hi, mid, lo


def _xdot_l(e, v):
    hi, mid, lo = _split3(v)
    return (jnp.dot(e, hi, preferred_element_type=F32) + jnp.dot(e, mid, preferred_element_type=F32)
            + jnp.dot(e, lo, preferred_element_type=F32))


def _xdot_r(v, e):
    hi, mid, lo = _split3(v)
    return (jnp.dot(hi, e, preferred_element_type=F32) + jnp.dot(mid, e, preferred_element_type=F32)
            + jnp.dot(lo, e, preferred_element_type=F32))


def _sigmoid(x):
    return 1.0 / (1.0 + jnp.exp(-x))


def _silu(x):
    return x * _sigmoid(x)


def _softplus(x):
    return jnp.maximum(x, 0.0) + jnp.log1p(jnp.exp(-jnp.abs(x)))


def _log_sigmoid(x):
    return jnp.minimum(x, 0.0) - jnp.log1p(jnp.exp(-jnp.abs(x)))


def _rms(x):
    return x * lax.rsqrt(jnp.mean(x * x, axis=-1, keepdims=True) + EPS)


def _iota(shape, dim):
    return lax.broadcasted_iota(jnp.int32, shape, dim)


def _rope(x, cos, sin_signed):
    lane = _iota(x.shape, 1)
    first = (lane % 32) < 16
    xr = jnp.where(first, pltpu.roll(x, 112, 1), pltpu.roll(x, 16, 1))
    return x * cos + xr * sin_signed


def _mod_index(tile_rows):
    n_ctx_tiles = N_CTX // tile_rows
    per_sample = DEC_SEQ // tile_rows

    def f(i):
        return jnp.where(i < n_ctx_tiles, 0, 1 + (i - n_ctx_tiles) // per_sample)

    return f


def _mod_kernel(c_ref, w_ref, b_ref, o_ref):
    c = c_ref[...]
    s = _silu(c)
    s_hi = s.astype(BF16)
    s_lo = (s - s_hi.astype(F32)).astype(BF16)
    w = w_ref[...]
    w_hi = w.astype(BF16)
    w_lo = (w - w_hi.astype(F32)).astype(BF16)
    acc = jnp.dot(s_hi, w_hi, preferred_element_type=F32)
    acc = acc + jnp.dot(s_hi, w_lo, preferred_element_type=F32)
    acc = acc + jnp.dot(s_lo, w_hi, preferred_element_type=F32)
    o_ref[...] = acc + b_ref[...]


def _modulation(cvec, w_ada, b_ada):
    tn = 1536
    n = 6 * D_MODEL
    return pl.pallas_call(
        _mod_kernel,
        grid=(DEPTH, n // tn),
        in_specs=[
            pl.BlockSpec((8, D_MODEL), lambda l, j: (0, 0)),
            pl.BlockSpec((None, D_MODEL, tn), lambda l, j: (l, 0, j)),
            pl.BlockSpec((None, 1, tn), lambda l, j: (l, 0, j)),
        ],
        out_specs=pl.BlockSpec((None, 8, tn), lambda l, j: (l, 0, j)),
        out_shape=jax.ShapeDtypeStruct((DEPTH, 8, n), F32),
        compiler_params=_cparams(("arbitrary", "arbitrary")),
        name="adaln_mod",
    )(cvec, w_ada, b_ada.reshape(DEPTH, 1, n))


ROW_SUB = 256


def _norm_mod_to(h_ref, x_ref, gain_ref, shift, scale, rows):
    def body(s, carry):
        r = pl.multiple_of(s * ROW_SUB, ROW_SUB)
        x = x_ref[pl.ds(r, ROW_SUB), :]
        h = (_rms(x) * gain_ref[...]) * (1.0 + scale) + shift
        h_ref[pl.ds(r, ROW_SUB), :] = h.astype(BF16)
        return carry

    lax.fori_loop(0, rows // ROW_SUB, body, 0)


def _inproj_kernel(x_ref, mod_ref, gain_ref, w_ref, u_ref, h_ref, *, tm):
    @pl.when(pl.program_id(1) == 0)
    def _():
        shift = mod_ref[:, 0:D_MODEL]
        scale = mod_ref[:, D_MODEL:2 * D_MODEL]
        _norm_mod_to(h_ref, x_ref, gain_ref, shift, scale, tm)

    u_ref[...] = jnp.dot(h_ref[...], w_ref[...], preferred_element_type=F32)


def _inproj(x, mod, gain, w, l):
    tm, tn = 1024, 1536
    midx = _mod_index(tm)
    return pl.pallas_call(
        functools.partial(_inproj_kernel, tm=tm),
        grid=(N_TOK // tm, U_COLS // tn),
        in_specs=[
            pl.BlockSpec((tm, D_MODEL), lambda i, j: (i, 0)),
            pl.BlockSpec((None, None, 1, 6 * D_MODEL), lambda i, j: (l, midx(i), 0, 0)),
            pl.BlockSpec((None, 1, D_MODEL), lambda i, j: (l, 0, 0)),
            pl.BlockSpec((None, D_MODEL, tn), lambda i, j: (l, 0, j)),
        ],
        out_specs=pl.BlockSpec((tm, tn), lambda i, j: (i, j)),
        out_shape=jax.ShapeDtypeStruct((N_TOK, U_COLS), F32),
        scratch_shapes=[pltpu.VMEM((tm, D_MODEL), BF16)],
        compiler_params=_cparams(("arbitrary", "arbitrary")),
        name="in_proj",
    )(x, mod, gain, w)


def _outproj_kernel(*refs, n_ctx_tiles):
    ctx_refs = refs[0:4]
    dec_refs = refs[4:8]
    x_ref, mod_ref, w_ref, o_ref = refs[8:12]
    i = pl.program_id(0)

    def compute(ys):
        acc = jnp.dot(ys[0][...], w_ref[0:512, :], preferred_element_type=F32)
        for g in range(1, 4):
            acc = acc + jnp.dot(ys[g][...], w_ref[g * 512:(g + 1) * 512, :], preferred_element_type=F32)
        gate = mod_ref[:, 2 * D_MODEL:3 * D_MODEL]
        o_ref[...] = x_ref[...] + gate * acc

    @pl.when(i < n_ctx_tiles)
    def _():
        compute(ctx_refs)

    @pl.when(i >= n_ctx_tiles)
    def _():
        compute(dec_refs)


def _outproj(ys_ctx, ys_dec, x, mod, w, l):
    tm = 512
    nct = N_CTX // tm
    midx = _mod_index(tm)
    ctx_spec = pl.BlockSpec((tm, GROUP_W), lambda i: (jnp.minimum(i, nct - 1), 0))
    dec_spec = pl.BlockSpec((tm, GROUP_W), lambda i: (jnp.maximum(i - nct, 0), 0))
    return pl.pallas_call(
        functools.partial(_outproj_kernel, n_ctx_tiles=nct),
        grid=(N_TOK // tm,),
        in_specs=[ctx_spec] * 4 + [dec_spec] * 4 + [
            pl.BlockSpec((tm, D_MODEL), lambda i: (i, 0)),
            pl.BlockSpec((None, None, 1, 6 * D_MODEL), lambda i: (l, midx(i), 0, 0)),
            pl.BlockSpec((None, D_MIX, D_MODEL), lambda i: (l, 0, 0)),
        ],
        out_specs=pl.BlockSpec((tm, D_MODEL), lambda i: (i, 0)),
        out_shape=jax.ShapeDtypeStruct((N_TOK, D_MODEL), F32),
        compiler_params=_cparams(("arbitrary",)),
        name="out_proj",
    )(*ys_ctx, *ys_dec, x, mod, w)


def _ffn_kernel(x_ref, mod_ref, gain_ref, wg_ref, wu_ref, wo_ref, o_ref, h_ref, acc_ref, *, tm, n_f):
    j = pl.program_id(1)

    def partial(h):
        gate = jnp.dot(h, wg_ref[...], preferred_element_type=F32)
        up = jnp.dot(h, wu_ref[...], preferred_element_type=F32)
        act = (_silu(gate) * up).astype(BF16)
        return jnp.dot(act, wo_ref[...], preferred_element_type=F32)

    @pl.when(j == 0)
    def _():
        shift = mod_ref[:, 3 * D_MODEL:4 * D_MODEL]
        scale = mod_ref[:, 4 * D_MODEL:5 * D_MODEL]

        def body(s, carry):
            r = pl.multiple_of(s * ROW_SUB, ROW_SUB)
            x = x_ref[pl.ds(r, ROW_SUB), :]
            h = ((_rms(x) * gain_ref[...]) * (1.0 + scale) + shift).astype(BF16)
            h_ref[pl.ds(r, ROW_SUB), :] = h
            acc_ref[pl.ds(r, ROW_SUB), :] = partial(h)
            return carry

        lax.fori_loop(0, tm // ROW_SUB, body, 0)

    @pl.when(j == n_f - 1)
    def _():
        g2 = mod_ref[:, 5 * D_MODEL:6 * D_MODEL]

        def body(s, carry):
            r = pl.multiple_of(s * ROW_SUB, ROW_SUB)
            y = acc_ref[pl.ds(r, ROW_SUB), :] + partial(h_ref[pl.ds(r, ROW_SUB), :])
            o_ref[pl.ds(r, ROW_SUB), :] = x_ref[pl.ds(r, ROW_SUB), :] + g2 * y
            return carry

        lax.fori_loop(0, tm // ROW_SUB, body, 0)


def _ffn(x, mod, gain, w_in, w_out, l):
    tm, tf = 1024, D_FF // 2
    n_f = D_FF // tf
    midx = _mod_index(tm)
    return pl.pallas_call(
        functools.partial(_ffn_kernel, tm=tm, n_f=n_f),
        grid=(N_TOK // tm, n_f),
        in_specs=[
            pl.BlockSpec((tm, D_MODEL), lambda i, j: (i, 0)),
            pl.BlockSpec((None, None, 1, 6 * D_MODEL), lambda i, j: (l, midx(i), 0, 0)),
            pl.BlockSpec((None, 1, D_MODEL), lambda i, j: (l, 0, 0)),
            pl.BlockSpec((None, D_MODEL, tf), lambda i, j: (l, 0, j)),
            pl.BlockSpec((None, D_MODEL, tf), lambda i, j: (l, 0, j + n_f)),
            pl.BlockSpec((None, tf, D_MODEL), lambda i, j: (l, j, 0)),
        ],
        out_specs=pl.BlockSpec((tm, D_MODEL), lambda i, j: (i, 0)),
        out_shape=jax.ShapeDtypeStruct((N_TOK, D_MODEL), F32),
        scratch_shapes=[pltpu.VMEM((tm, D_MODEL), BF16), pltpu.VMEM((tm, D_MODEL), F32)],
        compiler_params=_cparams(("arbitrary", "arbitrary")),
        name="ffn",
    )(x, mod, gain, w_in, w_in, w_out)


def _final_norm_kernel(x_ref, g_ref, o_ref):
    o_ref[...] = _rms(x_ref[...]) * g_ref[...]


def _final_norm(x, gain, row0, rows):
    tm = 512
    off = row0 // tm
    return pl.pallas_call(
        _final_norm_kernel,
        grid=(rows // tm,),
        in_specs=[pl.BlockSpec((tm, D_MODEL), lambda i: (i + off, 0)),
                  pl.BlockSpec((1, D_MODEL), lambda i: (0, 0))],
        out_specs=pl.BlockSpec((tm, D_MODEL), lambda i: (i, 0)),
        out_shape=jax.ShapeDtypeStruct((rows, D_MODEL), F32),
        compiler_params=_cparams(("arbitrary",)),
        name="final_norm",
    )(x, gain)


def _causal_masks():
    i = _iota((CHUNK, CHUNK), 0)
    j = _iota((CHUNK, CHUNK), 1)
    return j <= i, j >= i


def _conv_silu_chunk(pad_ref, w_ref, b_ref, r):
    win = pad_ref[pl.ds(r, CHUNK + 16), :]
    acc = b_ref[...] + w_ref[0:1, :] * win[6:6 + CHUNK, :]
    for tap in range(1, D_CONV):
        acc = acc + w_ref[tap:tap + 1, :] * win[6 + tap:6 + tap + CHUNK, :]
    return _silu(acc)


def _fill_padded(pad_ref, src_ref, seq, width, n_seq=1):
    zeros = jnp.zeros((8, width), F32)
    for i in range(n_seq):
        base = i * (seq + 16)
        pad_ref[base:base + 8, :] = zeros
        pad_ref[base + seq + 8:base + seq + 16, :] = zeros

        def body(c, carry, base=base, src0=i * seq):
            r = pl.multiple_of(c * CHUNK, CHUNK)
            pad_ref[pl.ds(base + 8 + r, CHUNK), :] = src_ref[pl.ds(src0 + r, CHUNK), :]
            return carry

        lax.fori_loop(0, seq // CHUNK, body, 0)


SSD_CPI = 2


def _ssd_kernel(*refs, seq, has_state, n_carry=0):
    (z_ref, xbc_ref, dt_ref, cw_ref, cb_ref, alog_ref, dtb_ref, dsk_ref, ng_ref) = refs[0:9]
    refs = refs[9 + n_carry:]
    if has_state:
        h0_ref, y_ref = refs[0:2]
        hfin_ref = None
    else:
        h0_ref = None
        y_ref, hfin_ref = refs[0:2]
    pad_ref, xc_ref, yf_ref, yb_ref, st_ref = refs[2:7]
    nc = seq // CHUNK
    hpg = SSM_HEADS // 2
    gw = hpg * SSM_HEAD_DIM

    _fill_padded(pad_ref, xbc_ref, seq, SSM_CONV_CH)

    def conv_body(c, carry):
        r = pl.multiple_of(c * CHUNK, CHUNK)
        xc_ref[pl.ds(r, CHUNK), :] = _conv_silu_chunk(pad_ref, cw_ref, cb_ref, r)
        return carry

    lax.fori_loop(0, nc, conv_body, 0)

    zero_blk = jnp.zeros((SSM_STATE, SSM_HEAD_DIM), F32)
    for d in range(2):
        for g in range(2):
            if has_state:
                rows = []
                for h4 in range(hpg):
                    blk = h0_ref[d, g * hpg + h4].T
                    rows.append(jnp.concatenate([blk if k == h4 else zero_blk for k in range(hpg)], axis=1))
                st_ref[d, g] = jnp.concatenate(rows, axis=0)
            else:
                st_ref[d, g] = jnp.zeros((gw, gw), F32)

    mask_f, mask_b = _causal_masks()
    masks = (mask_f, mask_b)
    tmats = (mask_f.astype(BF16), mask_b.astype(BF16))
    a_neg = -jnp.exp(alog_ref[...])
    lane_head = _iota((CHUNK, gw), 1) // SSM_HEAD_DIM
    blk_diag = (_iota((gw, gw), 0) // SSM_STATE) == (_iota((gw, gw), 1) // SSM_HEAD_DIM)

    def step(items):
        xcs = [xc_ref[pl.ds(r, CHUNK), :] for d, r in items]
        dts = [_softplus(dt_ref[pl.ds(r, CHUNK), :] + dtb_ref[...]) for d, r in items]
        cum_c = [_xdot_l(tmats[d], dts[n] * a_neg) for n, (d, r) in enumerate(items)]
        gms = [[_bdot_t(xcs[n][:, 640 + g * 64:640 + (g + 1) * 64], xcs[n][:, 512 + g * 64:512 + (g + 1) * 64])
                for g in range(2)] for n in range(len(items))]
        lhs_d, rhs_d, lhs_o, lhs_s, xgs, elcols = [], [], [], [], [], []
        for n, (d, r) in enumerate(items):
            cum_r = cum_c[n].T
            dt_r = dts[n].T
            last = cum_c[n][CHUNK - 1:CHUNK, :] if d == 0 else cum_c[n][0:1, :]
            e_last = jnp.exp(last)
            for g in range(2):
                xg = xcs[n][:, g * gw:(g + 1) * gw]
                bt = xcs[n][:, 512 + g * 64:512 + (g + 1) * 64].T
                cm = xcs[n][:, 640 + g * 64:640 + (g + 1) * 64]
                s_l, cec_l, btw_l, xm_l, el_l = [], [], [], [], []
                for h4 in range(hpg):
                    col = d * SSM_HEADS + g * hpg + h4
                    cc = jnp.broadcast_to(cum_c[n][:, col:col + 1], (CHUNK, CHUNK))
                    cr = cum_r[col:col + 1, :]
                    dtr = dt_r[col:col + 1, :]
                    decay = jnp.exp(jnp.where(masks[d], cc - cr, -jnp.inf))
                    s_l.append((decay * gms[n][g] * dtr).astype(BF16))
                    cec_l.append((cm * jnp.exp(cc[:, 0:SSM_STATE])).astype(BF16))
                    btw_l.append((bt * (jnp.exp(last[:, col:col + 1] - cr) * dtr)).astype(BF16))
                    xm_l.append(jnp.where(lane_head == h4, xg, 0.0).astype(BF16))
                    el_l.append(jnp.broadcast_to(e_last[:, col:col + 1], (SSM_STATE, gw)))
                lhs_d.append(jnp.concatenate(s_l, axis=1))
                rhs_d.append(jnp.concatenate(xm_l, axis=0))
                lhs_o.append(jnp.concatenate(cec_l, axis=1))
                lhs_s.append(jnp.concatenate(btw_l, axis=0))
                xgs.append(xg.astype(BF16))
                elcols.append(jnp.concatenate(el_l, axis=0))
        yds = [jnp.dot(lhs_d[k], rhs_d[k], preferred_element_type=F32) for k in range(len(lhs_d))]
        css = [jnp.dot(lhs_s[k], xgs[k], preferred_element_type=F32) for k in range(len(lhs_s))]
        outs = []
        for n, (d, r) in enumerate(items):
            ys = []
            for g in range(2):
                k = 2 * n + g
                st = st_ref[d, g]
                ys.append(yds[k] + jnp.dot(lhs_o[k], st.astype(BF16), preferred_element_type=F32))
                st_ref[d, g] = elcols[k] * st + jnp.where(blk_diag, css[k], 0.0)
            outs.append(jnp.concatenate(ys, axis=1))
        return outs

    def scan_body(s, carry):
        cf = SSD_CPI * s
        cb = nc - 1 - cf
        rows_f = [pl.multiple_of((cf + t) * CHUNK, CHUNK) for t in range(SSD_CPI)]
        rows_b = [pl.multiple_of((cb - t) * CHUNK, CHUNK) for t in range(SSD_CPI)]
        ys = step([(0, r) for r in rows_f] + [(1, r) for r in rows_b])
        for t in range(SSD_CPI):
            yf_ref[pl.ds(rows_f[t], CHUNK), :] = ys[t]
            yb_ref[pl.ds(rows_b[t], CHUNK), :] = ys[SSD_CPI + t]
        return carry

    lax.fori_loop(0, nc // SSD_CPI, scan_body, 0)

    def out_body(c, carry):
        r = pl.multiple_of(c * CHUNK, CHUNK)
        x = xc_ref[pl.ds(r, CHUNK), 0:GROUP_W]
        y = yf_ref[pl.ds(r, CHUNK), :] + yb_ref[pl.ds(r, CHUNK), :] + dsk_ref[...] * x
        y = y * _silu(z_ref[pl.ds(r, CHUNK), :])
        y_ref[pl.ds(r, CHUNK), :] = (_rms(y) * ng_ref[...]).astype(BF16)
        return carry

    lax.fori_loop(0, nc, out_body, 0)

    if hfin_ref is not None:
        for d in range(2):
            for g in range(2):
                st = st_ref[d, g]
                for h4 in range(hpg):
                    lo = h4 * SSM_STATE
                    hfin_ref[d, g * hpg + h4] = st[lo:lo + SSM_STATE, lo:lo + SSM_HEAD_DIM].T


def _carried(carry):
    if carry is None:
        return [], []
    return [pl.BlockSpec(memory_space=pl.ANY)] * len(carry), list(carry)


def _mixer_ssd(u, row0, nb, seq, conv_w, conv_b, a_log, dt_bias, d_skip, norm_g, h0, layer=0, carry=None):
    has_state = h0 is not None
    roff = row0 // seq
    vec = lambda n: pl.BlockSpec((1, n), lambda b: (0, 0))
    in_specs = [
        pl.BlockSpec((seq, GROUP_W), lambda b: (b + roff, A_Z // GROUP_W)),
        pl.BlockSpec((seq, SSM_CONV_CH), lambda b: (b + roff, A_XBC // SSM_CONV_CH)),
        pl.BlockSpec((seq, 128), lambda b: (b + roff, A_DT // 128)),
        pl.BlockSpec((8, SSM_CONV_CH), lambda b: (0, 0)),
        vec(SSM_CONV_CH), vec(128), vec(128), vec(GROUP_W), vec(GROUP_W),
    ]
    args = [u, u, u, conv_w, conv_b, a_log, dt_bias, d_skip, norm_g]
    st_block = (None, 2, SSM_HEADS, SSM_HEAD_DIM, SSM_STATE)
    y_spec = pl.BlockSpec((seq, GROUP_W), lambda b: (b, 0))
    y_shape = jax.ShapeDtypeStruct((nb * seq, GROUP_W), BF16)
    aliases = {}
    n_carry = 0
    if has_state:
        in_specs.append(pl.BlockSpec(st_block, lambda b: (b, 0, 0, 0, 0)))
        args.append(h0)
        out_specs, out_shape = y_spec, y_shape
    else:
        c_specs, c_args = _carried(carry)
        n_carry = len(c_args)
        aliases = {len(args) + k: 1 + k for k in range(n_carry)}
        in_specs += c_specs
        args += c_args
        out_specs = [y_spec, pl.BlockSpec((None,) + st_block, lambda b: (b, layer, 0, 0, 0, 0))]
        out_shape = [y_shape, jax.ShapeDtypeStruct((nb, DEPTH, 2, SSM_HEADS, SSM_HEAD_DIM, SSM_STATE), F32)]
    return pl.pallas_call(
        functools.partial(_ssd_kernel, seq=seq, has_state=has_state, n_carry=n_carry),
        grid=(nb,),
        in_specs=in_specs,
        out_specs=out_specs,
        out_shape=out_shape,
        input_output_aliases=aliases,
        scratch_shapes=[
            pltpu.VMEM((seq + 16, SSM_CONV_CH), F32),
            pltpu.VMEM((seq, SSM_CONV_CH), F32),
            pltpu.VMEM((seq, GROUP_W), F32),
            pltpu.VMEM((seq, GROUP_W), F32),
            pltpu.VMEM((2, 2, GROUP_W // 2, GROUP_W // 2), F32),
        ],
        compiler_params=_cparams(("arbitrary",)),
        name="mixer_ssd_dec" if has_state else "mixer_ssd_ctx",
    )(*args)


MLSTM_CPI = 4


def _mlstm_kernel(*refs, seq, n_seq, has_state, n_carry=0):
    (q_ref, k_ref, v_ref, o_ref, g_ref, cwq_ref, cwk_ref, cbq_ref, cbk_ref, gb_ref, ng_ref) = refs[0:11]
    refs = refs[11 + n_carry:]
    if has_state:
        c0_ref, n0_ref, m0_ref, y_ref = refs[0:4]
        outs = None
    else:
        y_ref = refs[0]
        outs = refs[1:4]
    (qpad_ref, kpad_ref, qc_ref, hf_ref, hb_ref, a_ref, cl_ref, rs_ref, rm_ref, cu_ref, rows_ref,
     cst_ref, nst_ref, mst_ref) = refs[4:]
    nc = seq // CHUNK
    head = pl.program_id(1)

    _fill_padded(qpad_ref, q_ref, seq, MLSTM_HEAD_DIM, n_seq)
    _fill_padded(kpad_ref, k_ref, seq, MLSTM_HEAD_DIM, n_seq)

    mask_f, mask_b = _causal_masks()
    row_id = _iota((CHUNK, CHUNK), 0)
    full = (CHUNK, CHUNK)

    sel = jnp.concatenate(
        [(row_id == col).astype(BF16) for col in (head, 4 + head, 8 + head, 12 + head)], axis=1)
    tmats = (mask_f.astype(BF16), mask_b.astype(BF16))
    masks = (mask_f, mask_b)
    forget_col = (_iota((CHUNK, CHUNK), 1) % 8) >= MLSTM_HEADS

    def local_body(s, carry):
        chunks = tuple(MLSTM_CPI * s + t for t in range(MLSTM_CPI))
        qs, ks, vs, qks, gsel = [], [], [], [], []
        for c in chunks:
            r = c * CHUNK if isinstance(c, int) else pl.multiple_of(c * CHUNK, CHUNK)
            rp = r if n_seq == 1 else c * CHUNK + 16 * (c // nc)
            q = _conv_silu_chunk(qpad_ref, cwq_ref, cbq_ref, rp) * (MLSTM_HEAD_DIM ** -0.5)
            k = _conv_silu_chunk(kpad_ref, cwk_ref, cbk_ref, rp)
            qc_ref[pl.ds(r, CHUNK), :] = q
            qs.append(q)
            ks.append(k)
            vs.append(v_ref[pl.ds(r, CHUNK), :])
            g = g_ref[pl.ds(r, CHUNK), :] + gb_ref[...]
            g = jnp.where(forget_col, _log_sigmoid(g), g)
            gsel.append([jnp.dot(p, sel, preferred_element_type=F32) for p in _split3(g)])
        for i in range(MLSTM_CPI):
            qks.append(_bdot_t(qs[i], ks[i]))
        items = [(i, d) for i in range(MLSTM_CPI) for d in range(2)]
        li_c = [gsel[i][0][:, 256 * d:256 * d + 128] + gsel[i][1][:, 256 * d:256 * d + 128]
                + gsel[i][2][:, 256 * d:256 * d + 128] for i, d in items]
        cum_c = []
        for i, d in items:
            lf = [gsel[i][p][:, 256 * d + 128:256 * d + 256].astype(BF16) for p in range(3)]
            cum_c.append(jnp.dot(tmats[d], lf[0], preferred_element_type=F32)
                         + jnp.dot(tmats[d], lf[1], preferred_element_type=F32)
                         + jnp.dot(tmats[d], lf[2], preferred_element_type=F32))
        cum_r = [x.T for x in cum_c]
        li_r = [x.T for x in li_c]
        last = [cum_c[n][CHUNK - 1:CHUNK, :] if d == 0 else cum_c[n][0:1, :]
                for n, (i, d) in enumerate(items)]
        dmat = [jnp.where(masks[d], cum_c[n] - cum_r[n] + li_r[n], -jnp.inf) for n, (i, d) in enumerate(items)]
        rowmax = [jnp.max(x, axis=1, keepdims=True) for x in dmat]
        sp = [qks[i] * jnp.exp(dmat[n] - rowmax[n]) for n, (i, d) in enumerate(items)]
        m_loc = [jnp.max(last[n] - cum_r[n][0:1, :] + li_r[n][0:1, :], axis=1, keepdims=True)
                 for n in range(len(items))]
        kw = [ks[i] * jnp.exp(last[n] - cum_c[n] + li_c[n] - m_loc[n]) for n, (i, d) in enumerate(items)]
        kwt = [x.T for x in kw]
        a_loc = [_bdot(sp[n], vs[i]) for n, (i, d) in enumerate(items)]
        c_loc = [_bdot(kwt[n], vs[i]) for n, (i, d) in enumerate(items)]
        for n, (i, d) in enumerate(items):
            c = chunks[i]
            a_ref[c, d] = a_loc[n]
            cl_ref[c, d] = c_loc[n]
            rs_ref[c, d] = jnp.broadcast_to(jnp.sum(sp[n], axis=1, keepdims=True), full)
            rm_ref[c, d] = jnp.broadcast_to(rowmax[n], full)
            cu_ref[c, d] = cum_c[n]
            rows_ref[c, d, 0:1, :] = jnp.sum(kw[n], axis=0, keepdims=True)
            rows_ref[c, d, 1:2, :] = jnp.broadcast_to(m_loc[n], (1, CHUNK))
            rows_ref[c, d, 2:3, :] = last[n]
        return carry

    n_local = n_seq * nc // MLSTM_CPI
    if n_local == 1:
        local_body(0, 0)
    else:
        lax.fori_loop(0, n_local, local_body, 0)

    if has_state:
        cst_ref[...] = c0_ref[...]
        nst_ref[...] = n0_ref[...]
        mst_ref[...] = m0_ref[...]
    else:
        cst_ref[...] = jnp.zeros(cst_ref.shape, F32)
        nst_ref[...] = jnp.zeros(nst_ref.shape, F32)
        mst_ref[...] = jnp.zeros(mst_ref.shape, F32)

    def state_step(i, d, lc):
        c = i * nc + lc
        r = pl.multiple_of(c * CHUNK, CHUNK)
        q = qc_ref[pl.ds(r, CHUNK), :]
        c_p = cst_ref[i, d]
        n_p = nst_ref[i, d]
        m_p = mst_ref[i, d]
        n_loc = rows_ref[c, d, 0:1, :]
        m_loc = rows_ref[c, d, 1:2, :]
        last = rows_ref[c, d, 2:3, :]
        rowmax = rm_ref[c, d]
        inter = cu_ref[c, d] + m_p
        m_t = jnp.maximum(inter, rowmax)
        f_intra = jnp.exp(rowmax - m_t)
        w_inter = jnp.exp(inter - m_t)
        num = a_ref[c, d] * f_intra + w_inter * _bdot(q, c_p)
        den = rs_ref[c, d] * f_intra + w_inter * jnp.sum(q * n_p, axis=1, keepdims=True)
        hh = num / jnp.maximum(jnp.abs(den), jnp.exp(-m_t))
        m_new = jnp.maximum(last + m_p, m_loc)
        s_p = jnp.exp(last + m_p - m_new)
        s_l = jnp.exp(m_loc - m_new)
        cst_ref[i, d] = s_p[:, 0:1] * c_p + s_l[:, 0:1] * cl_ref[c, d]
        nst_ref[i, d] = s_p * n_p + s_l * n_loc
        mst_ref[i, d] = m_new
        return r, hh

    def state_body(s, carry):
        for i in range(n_seq):
            r, hh = state_step(i, 0, s)
            hf_ref[pl.ds(r, CHUNK), :] = hh
            r, hh = state_step(i, 1, nc - 1 - s)
            hb_ref[pl.ds(r, CHUNK), :] = hh
        return carry

    lax.fori_loop(0, nc, state_body, 0, unroll=2)

    def out_body(c, carry):
        r = pl.multiple_of(c * CHUNK, CHUNK)
        hsum = hf_ref[pl.ds(r, CHUNK), :] + hb_ref[pl.ds(r, CHUNK), :]
        y = _sigmoid(o_ref[pl.ds(r, CHUNK), :]) * (_rms(hsum) * ng_ref[...])
        y_ref[pl.ds(r, CHUNK), :] = y.astype(BF16)
        return carry

    lax.fori_loop(0, n_seq * nc, out_body, 0, unroll=2)

    if outs is not None:
        outs[0][...] = cst_ref[...]
        outs[1][...] = nst_ref[...]
        outs[2][...] = mst_ref[...]


def _mixer_mlstm(u, row0, nb, seq, conv_w, conv_b, gate_b, norm_g, state, layer=0, carry=None):
    has_state = state is not None
    hd = MLSTM_HEAD_DIM
    nc = seq // CHUNK
    n_seq = max(1, MLSTM_CPI // nc)
    rows = n_seq * seq
    gnc = n_seq * nc
    roff = row0 // rows
    col = lambda base: (lambda b, h: (b + roff, base // hd + h))
    in_specs = [
        pl.BlockSpec((rows, hd), col(C_QK)),
        pl.BlockSpec((rows, hd), col(C_QK + GROUP_W)),
        pl.BlockSpec((rows, hd), col(C_V)),
        pl.BlockSpec((rows, hd), col(C_O)),
        pl.BlockSpec((rows, 128), lambda b, h: (b + roff, C_G // 128)),
        pl.BlockSpec((8, hd), lambda b, h: (0, h)),
        pl.BlockSpec((8, hd), lambda b, h: (0, MLSTM_HEADS + h)),
        pl.BlockSpec((1, hd), lambda b, h: (0, h)),
        pl.BlockSpec((1, hd), lambda b, h: (0, MLSTM_HEADS + h)),
        pl.BlockSpec((1, 128), lambda b, h: (0, 0)),
        pl.BlockSpec((1, hd), lambda b, h: (0, h)),
    ]
    args = [u, u, u, u, u, conv_w, conv_w, conv_b, conv_b, gate_b, norm_g]
    c_spec = pl.BlockSpec((n_seq, 2, None, hd, hd), lambda b, h: (b, 0, h, 0, 0))
    n_spec = pl.BlockSpec((n_seq, 2, None, 1, hd), lambda b, h: (b, 0, h, 0, 0))
    y_spec = pl.BlockSpec((rows, hd), lambda b, h: (b, h))
    y_shape = jax.ShapeDtypeStruct((nb * seq, GROUP_W), BF16)
    aliases = {}
    n_carry = 0
    if has_state:
        in_specs += [c_spec, n_spec, n_spec]
        args += list(state)
        out_specs, out_shape = y_spec, y_shape
    else:
        c_specs, c_args = _carried(carry)
        n_carry = len(c_args)
        aliases = {len(args) + k: 1 + k for k in range(n_carry)}
        in_specs += c_specs
        args += c_args
        c_all = pl.BlockSpec((n_seq, None, 2, None, hd, hd), lambda b, h: (b, layer, 0, h, 0, 0))
        n_all = pl.BlockSpec((n_seq, None, 2, None, 1, hd), lambda b, h: (b, layer, 0, h, 0, 0))
        out_specs = [y_spec, c_all, n_all, n_all]
        out_shape = [y_shape,
                     jax.ShapeDtypeStruct((nb, DEPTH, 2, MLSTM_HEADS, hd, hd), F32),
                     jax.ShapeDtypeStruct((nb, DEPTH, 2, MLSTM_HEADS, 1, hd), F32),
                     jax.ShapeDtypeStruct((nb, DEPTH, 2, MLSTM_HEADS, 1, hd), F32)]
    return pl.pallas_call(
        functools.partial(_mlstm_kernel, seq=seq, n_seq=n_seq, has_state=has_state, n_carry=n_carry),
        grid=(nb // n_seq, MLSTM_HEADS),
        in_specs=in_specs,
        out_specs=out_specs,
        out_shape=out_shape,
        input_output_aliases=aliases,
        scratch_shapes=[
            pltpu.VMEM((rows + 16 * n_seq, hd), F32), pltpu.VMEM((rows + 16 * n_seq, hd), F32),
            pltpu.VMEM((rows, hd), F32), pltpu.VMEM((rows, hd), F32), pltpu.VMEM((rows, hd), F32),
            pltpu.VMEM((gnc, 2, hd, hd), F32), pltpu.VMEM((gnc, 2, hd, hd), F32),
            pltpu.VMEM((gnc, 2, hd, hd), F32), pltpu.VMEM((gnc, 2, hd, hd), F32), pltpu.VMEM((gnc, 2, hd, hd), F32),
            pltpu.VMEM((gnc, 2, 8, hd), F32),
            pltpu.VMEM((n_seq, 2, hd, hd), F32), pltpu.VMEM((n_seq, 2, 1, hd), F32), pltpu.VMEM((n_seq, 2, 1, hd), F32),
        ],
        compiler_params=_cparams(("arbitrary", "arbitrary")),
        name="mixer_mlstm_dec" if has_state else "mixer_mlstm_ctx",
    )(*args)


KV_SUB = 256


ATT_SUB = 256
DIFF_QB = 1024
GQA_QB = 512


def _softmax_parts(s):
    e = jnp.exp(s - jnp.max(s, axis=1, keepdims=True))
    return e.astype(BF16), jnp.sum(e, axis=1, keepdims=True)


def _diff_lambda(lam_ref, lam_init):
    lp = lam_ref[...]
    return (jnp.exp(jnp.sum(lp[0:1, :] * lp[1:2, :], axis=1, keepdims=True))
            - jnp.exp(jnp.sum(lp[2:3, :] * lp[3:4, :], axis=1, keepdims=True)) + lam_init)


def _exp_scores_t(s_t):
    e = jnp.exp(s_t - jnp.max(s_t, axis=0, keepdims=True))
    return e.astype(BF16), jnp.sum(e, axis=0, keepdims=True)


def _pv_t(vt, el):
    e, l = el
    return jnp.dot(vt, e, preferred_element_type=F32) / l


def _diff_dec_kernel(q_ref, k_ref, v_ref, lam_ref, ck_ref, cv_ref, cosq_ref, sinq_ref, cosk_ref, sink_ref,
                     o_ref, k1_ref, k2_ref, vt_ref, *, seq, lam_init):
    past = PAST_LEN

    @pl.when(pl.program_id(2) == 0)
    def _():
        ck = ck_ref[...]
        k1_ref[0:past, :] = ck[:, 0:64].astype(BF16)
        k2_ref[0:past, :] = ck[:, 64:128].astype(BF16)
        vt_ref[:, 0:past] = cv_ref[...].T.astype(BF16)
        for s in range(seq // KV_SUB):
            r = s * KV_SUB
            kk = _rope(k_ref[r:r + KV_SUB, :], cosk_ref[r:r + KV_SUB, :], sink_ref[r:r + KV_SUB, :])
            k1_ref[past + r:past + r + KV_SUB, :] = kk[:, 0:64].astype(BF16)
            k2_ref[past + r:past + r + KV_SUB, :] = kk[:, 64:128].astype(BF16)
            vt_ref[:, past + r:past + r + KV_SUB] = v_ref[r:r + KV_SUB, :].T.astype(BF16)

    lam = _diff_lambda(lam_ref, lam_init)
    q = _rope(q_ref[...], cosq_ref[...], sinq_ref[...]) * (64 ** -0.5)
    n_sub = q.shape[0] // ATT_SUB
    k_maps = (k1_ref[...], k2_ref[...])
    q_t = [q[j * ATT_SUB:(j + 1) * ATT_SUB, m * 64:(m + 1) * 64].T.astype(BF16)
           for j in range(n_sub) for m in range(2)]
    scores = [jnp.dot(k_maps[n % 2], q_t[n], preferred_element_type=F32) for n in range(2 * n_sub)]
    probs = [_exp_scores_t(s) for s in scores]
    vt = vt_ref[...]
    pv = [_pv_t(vt, el) for el in probs]
    for j in range(n_sub):
        o = (pv[2 * j] - lam * pv[2 * j + 1]).T
        o_ref[j * ATT_SUB:(j + 1) * ATT_SUB, :] = (_rms(o) * (1.0 - lam_init)).astype(BF16)


def _diff_ctx_kernel(*refs, lam_init, n_carry=0):
    q_ref, k_ref, v_ref, lam_ref = refs[0:4]
    o_ref, ko_ref, vo_ref = refs[4 + n_carry:]
    ko_ref[...] = k_ref[...]
    vo_ref[...] = v_ref[...]
    lam = _diff_lambda(lam_ref, lam_init)
    q = q_ref[...] * (64 ** -0.5)
    k = k_ref[...].astype(BF16)
    v = v_ref[...].astype(BF16)
    cols = [h * 128 + m * 64 for h in range(DIFF_HEADS) for m in range(2)]
    scores = [_bdot_t(q[:, c0:c0 + 64], k[:, c0:c0 + 64]) for c0 in cols]
    parts = [_softmax_parts(s) for s in scores]
    pv = [jnp.dot(parts[n][0], v[:, (n // 2) * 128:(n // 2 + 1) * 128], preferred_element_type=F32)
          for n in range(2 * DIFF_HEADS)]
    outs = []
    for h in range(DIFF_HEADS):
        o = pv[2 * h] / parts[2 * h][1] - lam * (pv[2 * h + 1] / parts[2 * h + 1][1])
        outs.append(_rms(o) * (1.0 - lam_init))
    o_ref[...] = jnp.concatenate(outs, axis=1).astype(BF16)


def _mixer_diff(u, row0, nb, seq, lam_rows, lam_init, ctx, layer=0, carry=None):
    roff = row0 // seq
    o_shape = jax.ShapeDtypeStruct((nb * seq, GROUP_W), BF16)
    lam_spec3 = pl.BlockSpec((8, 128), lambda b, h, i: (0, 0))
    if ctx is None:
        col = lambda base: pl.BlockSpec((seq, GROUP_W), lambda b: (b + roff, base // GROUP_W))
        c_specs, c_args = _carried(carry)
        slab = pl.BlockSpec((None, None, seq, GROUP_W), lambda b: (b, layer, 0, 0))
        slab_shape = jax.ShapeDtypeStruct((nb, DEPTH, seq, GROUP_W), F32)
        return pl.pallas_call(
            functools.partial(_diff_ctx_kernel, lam_init=lam_init, n_carry=len(c_args)),
            grid=(nb,),
            in_specs=[col(B_Q), col(B_K), col(B_V), pl.BlockSpec((8, 128), lambda b: (0, 0))] + c_specs,
            out_specs=[pl.BlockSpec((seq, GROUP_W), lambda b: (b, 0)), slab, slab],
            out_shape=[o_shape, slab_shape, slab_shape],
            input_output_aliases={4 + k: 1 + k for k in range(len(c_args))},
            compiler_params=_cparams(("arbitrary",)),
            name="mixer_diff_ctx",
        )(u, u, u, lam_rows, *c_args)
    ck, cv, cos, sin = ctx
    qb = min(DIFF_QB, seq)
    qoff = row0 // qb
    nq = seq // qb
    lk = seq + PAST_LEN
    in_specs = [
        pl.BlockSpec((qb, 128), lambda b, h, i: (qoff + b * nq + i, B_Q // 128 + h)),
        pl.BlockSpec((seq, 128), lambda b, h, i: (b + roff, B_K // 128 + h)),
        pl.BlockSpec((seq, 128), lambda b, h, i: (b + roff, B_V // 128 + h)),
        lam_spec3,
        pl.BlockSpec((None, PAST_LEN, 128), lambda b, h, i: (b, 0, h)),
        pl.BlockSpec((None, PAST_LEN, 128), lambda b, h, i: (b, 0, h)),
        pl.BlockSpec((qb, 128), lambda b, h, i: (i, 0)),
        pl.BlockSpec((qb, 128), lambda b, h, i: (i, 0)),
        pl.BlockSpec((seq, 128), lambda b, h, i: (0, 0)),
        pl.BlockSpec((seq, 128), lambda b, h, i: (0, 0)),
    ]
    return pl.pallas_call(
        functools.partial(_diff_dec_kernel, seq=seq, lam_init=lam_init),
        grid=(nb, DIFF_HEADS, nq),
        in_specs=in_specs,
        out_specs=pl.BlockSpec((qb, 128), lambda b, h, i: (b * nq + i, h)),
        out_shape=o_shape,
        scratch_shapes=[pltpu.VMEM((lk, 64), BF16), pltpu.VMEM((lk, 64), BF16), pltpu.VMEM((128, lk), BF16)],
        compiler_params=_cparams(("arbitrary", "arbitrary", "arbitrary")),
        name="mixer_diff_dec",
    )(u, u, u, lam_rows, ck, cv, cos, sin, cos, sin)


def _seg_rms_pair(x, gain):
    low = _iota(x.shape, 1) < 64
    sq = x * x
    ms_lo = jnp.sum(jnp.where(low, sq, 0.0), axis=1, keepdims=True) * (1.0 / 64)
    ms_hi = jnp.sum(jnp.where(low, 0.0, sq), axis=1, keepdims=True) * (1.0 / 64)
    return x * lax.rsqrt(jnp.where(low, ms_lo, ms_hi) + EPS) * gain


def _gqa_attend(q_heads, kf, vf):
    rows = q_heads[0].shape[0]
    s = _bdot_t(jnp.concatenate(q_heads, axis=0), kf)
    e, l = _softmax_parts(s)
    o = jnp.dot(e, vf, preferred_element_type=F32) / l
    return [o[g * rows:(g + 1) * rows, :] for g in range(len(q_heads))]


def _gqa_dec_kernel(q_ref, kv_ref, qg_ref, kg_ref, ck_ref, cv_ref, cosq_ref, sinq_ref, cosk_ref, sink_ref,
                    o_ref, kf_ref, vt_ref, *, seq):
    past = PAST_LEN

    @pl.when(pl.program_id(2) == 0)
    def _():
        kf_ref[0:past, :] = ck_ref[...].astype(BF16)
        vt_ref[:, 0:past] = cv_ref[...].T.astype(BF16)
        for s in range(seq // KV_SUB):
            r = s * KV_SUB
            blk = kv_ref[r:r + KV_SUB, :]
            kn = _rope(_seg_rms_pair(blk, kg_ref[...]), cosk_ref[r:r + KV_SUB, :], sink_ref[r:r + KV_SUB, :])
            kf_ref[past + r:past + r + KV_SUB, :] = kn[:, 0:64].astype(BF16)
            vt_ref[:, past + r:past + r + KV_SUB] = blk[:, 64:128].T.astype(BF16)

    n_sub = q_ref.shape[0] // ATT_SUB
    heads_t = []
    for j in range(n_sub):
        rows = slice(j * ATT_SUB, (j + 1) * ATT_SUB)
        for half in range(2):
            qh = _seg_rms_pair(q_ref[rows, half * 128:(half + 1) * 128], qg_ref[...])
            qh = _rope(qh, cosq_ref[rows, :], sinq_ref[rows, :]) * (64 ** -0.5)
            heads_t += [qh[:, 0:64].T.astype(BF16), qh[:, 64:128].T.astype(BF16)]
    kf = kf_ref[...]
    scores = [jnp.dot(kf, q_t, preferred_element_type=F32) for q_t in heads_t]
    probs = [_exp_scores_t(s) for s in scores]
    vt = vt_ref[...]
    outs = [_pv_t(vt, el).T for el in probs]
    for j in range(n_sub):
        o_ref[j * ATT_SUB:(j + 1) * ATT_SUB, :] = jnp.concatenate(outs[4 * j:4 * j + 4], axis=1).astype(BF16)


def _gqa_ctx_kernel(*refs, n_carry=0):
    q_ref, kv_ref, qg_ref, kg_ref = refs[0:4]
    o_ref, kn_ref, vo_ref = refs[4 + n_carry:]
    outs = []
    for kvh in range(GQA_KV_HEADS):
        blk = kv_ref[:, kvh * 128:(kvh + 1) * 128]
        kn = _seg_rms_pair(blk, kg_ref[...])[:, 0:64]
        kn_ref[:, kvh * 64:(kvh + 1) * 64] = kn
        vo_ref[:, kvh * 64:(kvh + 1) * 64] = blk[:, 64:128]
        heads = []
        for half in range(2):
            c0 = kvh * 256 + half * 128
            qh = _seg_rms_pair(q_ref[:, c0:c0 + 128], qg_ref[...]) * (64 ** -0.5)
            heads += [qh[:, 0:64], qh[:, 64:128]]
        outs += _gqa_attend(heads, kn.astype(BF16), blk[:, 64:128].astype(BF16))
    o_ref[...] = jnp.concatenate(outs, axis=1).astype(BF16)


def _mixer_gqa(u, row0, nb, seq, q_gain, k_gain, ctx, layer=0, carry=None):
    roff = row0 // seq
    o_shape = jax.ShapeDtypeStruct((nb * seq, GROUP_W), BF16)
    if ctx is None:
        gain = pl.BlockSpec((1, 128), lambda b: (0, 0))
        c_specs, c_args = _carried(carry)
        kvw = GQA_KV_HEADS * GQA_HEAD_DIM
        slab = pl.BlockSpec((None, None, seq, kvw), lambda b: (b, layer, 0, 0))
        slab_shape = jax.ShapeDtypeStruct((nb, DEPTH, seq, kvw), F32)
        return pl.pallas_call(
            functools.partial(_gqa_ctx_kernel, n_carry=len(c_args)),
            grid=(nb,),
            in_specs=[pl.BlockSpec((seq, GROUP_W), lambda b: (b + roff, D_Q // GROUP_W)),
                      pl.BlockSpec((seq, 256), lambda b: (b + roff, D_KV // 256)), gain, gain] + c_specs,
            out_specs=[pl.BlockSpec((seq, GROUP_W), lambda b: (b, 0)), slab, slab],
            out_shape=[o_shape, slab_shape, slab_shape],
            input_output_aliases={4 + k: 1 + k for k in range(len(c_args))},
            compiler_params=_cparams(("arbitrary",)),
            name="mixer_gqa_ctx",
        )(u, u, q_gain, k_gain, *c_args)
    ck, cv, cos, sin = ctx
    qb = GQA_QB
    qoff = row0 // qb
    nq = seq // qb
    lk = seq + PAST_LEN
    in_specs = [
        pl.BlockSpec((qb, 256), lambda b, h, i: (qoff + b * nq + i, D_Q // 256 + h)),
        pl.BlockSpec((seq, 128), lambda b, h, i: (b + roff, D_KV // 128 + h)),
        pl.BlockSpec((1, 128), lambda b, h, i: (0, 0)),
        pl.BlockSpec((1, 128), lambda b, h, i: (0, 0)),
        pl.BlockSpec((None, None, PAST_LEN, 64), lambda b, h, i: (b, h, 0, 0)),
        pl.BlockSpec((None, None, PAST_LEN, 64), lambda b, h, i: (b, h, 0, 0)),
        pl.BlockSpec((qb, 128), lambda b, h, i: (i, 0)),
        pl.BlockSpec((qb, 128), lambda b, h, i: (i, 0)),
        pl.BlockSpec((seq, 128), lambda b, h, i: (0, 0)),
        pl.BlockSpec((seq, 128), lambda b, h, i: (0, 0)),
    ]
    return pl.pallas_call(
        functools.partial(_gqa_dec_kernel, seq=seq),
        grid=(nb, GQA_KV_HEADS, nq),
        in_specs=in_specs,
        out_specs=pl.BlockSpec((qb, 256), lambda b, h, i: (b * nq + i, h)),
        out_shape=o_shape,
        scratch_shapes=[pltpu.VMEM((lk, 64), BF16), pltpu.VMEM((64, lk), BF16)],
        compiler_params=_cparams(("arbitrary", "arbitrary", "arbitrary")),
        name="mixer_gqa_dec",
    )(u, u, q_gain, k_gain, ck, cv, cos, sin, cos, sin)


def _rope_tables(n_tok):
    rows = n_tok // GRID_W
    r, c = jnp.meshgrid(jnp.arange(rows, dtype=F32), jnp.arange(GRID_W, dtype=F32), indexing='ij')
    nf = 16
    freqs = ROPE_THETA ** (-jnp.arange(nf, dtype=F32) / nf)
    ang = jnp.stack([r.reshape(-1)[:, None] * freqs, c.reshape(-1)[:, None] * freqs], axis=1)
    cos, sin = jnp.cos(ang), jnp.sin(ang)
    cos64 = jnp.concatenate([cos, cos], axis=2).reshape(n_tok, 64)
    sin64 = jnp.concatenate([-sin, sin], axis=2).reshape(n_tok, 64)
    return jnp.tile(cos64, (1, 2)), jnp.tile(sin64, (1, 2))


def _pad_lanes(v, n=128):
    v = v.reshape(1, -1)
    return jnp.pad(v, ((0, 0), (0, n - v.shape[1])))


def kernel(x_prompt, x_sample, c, cache_diff_k, cache_diff_v, cache_gqa_k, cache_gqa_v, state_ssm, state_mlstm_c, state_mlstm_n, state_mlstm_m, c_ctx, w_ada, b_ada, norm1, norm2, w_in, w_out, conv_ssd_w, conv_ssd_b, ssd_a_log, ssd_dt_bias, ssd_d, ssd_norm, diff_lq1, diff_lk1, diff_lq2, diff_lk2, conv_mlstm_w, conv_mlstm_b, mlstm_gate_b, mlstm_norm, gqa_q_norm, gqa_k_norm, w_ffn_in, w_ffn_out, norm_f):
    w_in_p = jnp.zeros((DEPTH, D_MODEL, U_COLS), BF16)
    col = 0
    for s, n in _IN_SEGMENTS:
        if s is not None:
            w_in_p = lax.dynamic_update_slice(w_in_p, w_in[:, :, s:s + n].astype(BF16), (0, 0, col))
        col += n
    w_out_b = w_out.astype(BF16)
    gain1 = norm1.reshape(DEPTH, 1, D_MODEL)
    gain2 = norm2.reshape(DEPTH, 1, D_MODEL)
    w_ffn_in_b = w_ffn_in.astype(BF16)
    w_ffn_out_b = w_ffn_out.astype(BF16)
    cvec = jnp.concatenate([c_ctx[None, :], c, jnp.zeros((5, D_MODEL), F32)], axis=0)
    cos_t, sin_t = _rope_tables(DEC_SEQ)
    pad_taps = lambda w: jnp.pad(w, ((0, 8 - D_CONV), (0, 0)))

    mod = _modulation(cvec, w_ada, b_ada).reshape(DEPTH, 8, 1, 6 * D_MODEL)
    x = jnp.concatenate([x_prompt.reshape(N_CTX, D_MODEL), x_sample.reshape(N_DEC, D_MODEL)], axis=0)

    ssm_all = diff_kv_all = mlstm_all = gqa_kv_all = None
    for l in range(DEPTH):
        u = _inproj(x, mod, gain1, w_in_p, l)

        ssd_args = (pad_taps(conv_ssd_w[l]), conv_ssd_b[l].reshape(1, -1), _pad_lanes(ssd_a_log[l]),
                    _pad_lanes(ssd_dt_bias[l]), jnp.repeat(ssd_d[l], SSM_HEAD_DIM).reshape(1, GROUP_W),
                    ssd_norm[l].reshape(1, GROUP_W))
        ya_c, ssm_new = _mixer_ssd(u, 0, BATCH, SEQ, *ssd_args, None, l, ssm_all)
        ssm_all = (ssm_new,)
        ya_d = _mixer_ssd(u, N_CTX, DEC_BATCH, DEC_SEQ, *ssd_args, state_ssm[:, l])

        lam_init = 0.8 - 0.6 * math.exp(-0.3 * l)
        lam_rows = jnp.pad(jnp.stack([diff_lq1[l], diff_lk1[l], diff_lq2[l], diff_lk2[l]]), ((0, 4), (0, 64)))
        yb_c, *diff_kv_all = _mixer_diff(u, 0, BATCH, SEQ, lam_rows, lam_init, None, l, diff_kv_all)
        yb_d = _mixer_diff(u, N_CTX, DEC_BATCH, DEC_SEQ, lam_rows, lam_init,
                           (cache_diff_k[:, l].reshape(DEC_BATCH, PAST_LEN, GROUP_W),
                            cache_diff_v[:, l].reshape(DEC_BATCH, PAST_LEN, GROUP_W), cos_t, sin_t))

        ml_args = (pad_taps(conv_mlstm_w[l]), conv_mlstm_b[l].reshape(1, -1), _pad_lanes(mlstm_gate_b[l]),
                   mlstm_norm[l].reshape(1, GROUP_W))
        yc_c, *mlstm_all = _mixer_mlstm(u, 0, BATCH, SEQ, *ml_args, None, l, mlstm_all)
        m0 = jnp.broadcast_to(state_mlstm_m[:, l][..., None, None], (DEC_BATCH, 2, MLSTM_HEADS, 1, MLSTM_HEAD_DIM))
        yc_d = _mixer_mlstm(u, N_CTX, DEC_BATCH, DEC_SEQ, *ml_args,
                            (state_mlstm_c[:, l], state_mlstm_n[:, l][:, :, :, None, :], m0))

        q_gain = jnp.tile(gqa_q_norm[l], 2).reshape(1, 128)
        k_gain = jnp.tile(gqa_k_norm[l], 2).reshape(1, 128)
        yd_c, *gqa_kv_all = _mixer_gqa(u, 0, BATCH, SEQ, q_gain, k_gain, None, l, gqa_kv_all)
        yd_d = _mixer_gqa(u, N_CTX, DEC_BATCH, DEC_SEQ, q_gain, k_gain,
                          (cache_gqa_k[:, l].transpose(0, 2, 1, 3), cache_gqa_v[:, l].transpose(0, 2, 1, 3), cos_t, sin_t))

        x = _outproj((ya_c, yb_c, yc_c, yd_c), (ya_d, yb_d, yc_d, yd_d), x, mod, w_out_b, l)
        x = _ffn(x, mod, gain2, w_ffn_in_b, w_ffn_out_b, l)

    y_prompt = _final_norm(x, norm_f.reshape(1, D_MODEL), 0, N_CTX).reshape(BATCH, SEQ, D_MODEL)
    y_sample = _final_norm(x, norm_f.reshape(1, D_MODEL), N_CTX, N_DEC).reshape(DEC_BATCH, DEC_SEQ, D_MODEL)
    c_all, n_all, m_all = mlstm_all
    return (y_prompt, y_sample,
            diff_kv_all[0].reshape(BATCH, DEPTH, SEQ, DIFF_HEADS, 2, 64),
            diff_kv_all[1].reshape(BATCH, DEPTH, SEQ, DIFF_HEADS, 128),
            gqa_kv_all[0].reshape(BATCH, DEPTH, SEQ, GQA_KV_HEADS, GQA_HEAD_DIM),
            gqa_kv_all[1].reshape(BATCH, DEPTH, SEQ, GQA_KV_HEADS, GQA_HEAD_DIM),
            ssm_all[0], c_all, n_all[:, :, :, :, 0, :], m_all[:, :, :, :, 0, 0])
```

```python
import functools
import math

import numpy as np
import jax
import jax.numpy as jnp
from jax import lax
from jax.experimental import pallas as pl
from jax.experimental.pallas import tpu as pltpu

F32 = jnp.float32
BF16 = jnp.bfloat16

D_MODEL = 1024
BATCH = 16
SEQ = 256
DEPTH = 4
DEC_BATCH = 2
DEC_SEQ = 2048
PAST_LEN = 256
GRID_W = 64
GROUP_W = 512
D_MIX = 2048
CHUNK = 128
Q_BLOCK = 128
D_CONV = 5
ROPE_THETA = 10000.0
EPS = 1e-6
SSM_HEADS = 8
SSM_HEAD_DIM = 64
SSM_STATE = 64
SSM_CONV_CH = 768
DIFF_HEADS = 4
MLSTM_HEADS = 4
MLSTM_HEAD_DIM = 128
GQA_KV_HEADS = 2
GQA_HEAD_DIM = 64
IN_COLS = 5664
D_FF = 2816

N_CTX = BATCH * SEQ
N_DEC = DEC_BATCH * DEC_SEQ
N_TOK = N_CTX + N_DEC

U_COLS = 6144
C_QK, C_V, C_O = 0, 1024, 1536
A_Z = 2048
B_Q, B_K, B_V = 2560, 3072, 3584
D_Q = 4096
A_XBC = 4608
D_KV = 5376
A_DT = 5632
C_G = 5760

VMEM_LIMIT_BYTES = 56 * 1024 * 1024


def _in_col_permutation():
    idx = np.full((U_COLS,), IN_COLS, np.int32)
    a0, b0, c0, d0 = 0, 1296, 2832, 4896

    def put(dst, src, n):
        idx[dst:dst + n] = np.arange(src, src + n)

    put(A_Z, a0, 512)
    put(A_XBC, a0 + 512, 768)
    put(A_DT, a0 + 1280, 16)
    put(B_Q, b0, 512)
    put(B_K, b0 + 512, 512)
    put(B_V, b0 + 1024, 512)
    put(C_QK, c0, 1024)
    put(C_V, c0 + 1024, 512)
    put(C_O, c0 + 1536, 512)
    put(C_G, c0 + 2048, 16)
    put(D_Q, d0, 512)
    for kv in range(GQA_KV_HEADS):
        put(D_KV + kv * 128, d0 + 512 + kv * 64, 64)
        put(D_KV + kv * 128 + 64, d0 + 640 + kv * 64, 64)
    return idx


_IN_PERM = _in_col_permutation()


def _runs(idx):
    out, start = [], 0
    for p in range(1, len(idx) + 1):
        pad = idx[start] == IN_COLS
        if p == len(idx) or (idx[p] == IN_COLS) != pad or (not pad and idx[p] != idx[p - 1] + 1):
            out.append((None if pad else int(idx[start]), p - start))
            start = p
    return out


_IN_SEGMENTS = _runs(_IN_PERM)


def _cparams(sem):
    return pltpu.CompilerParams(dimension_semantics=sem, vmem_limit_bytes=VMEM_LIMIT_BYTES)


def _bdot(a, b):
    return jnp.dot(a.astype(BF16), b.astype(BF16), preferred_element_type=F32)


def _bdot_t(a, b):
    return lax.dot_general(a.astype(BF16), b.astype(BF16), (((1,), (1,)), ((), ())),
                           preferred_element_type=F32)


def _split3(v):
    hi = v.astype(BF16)
    r = v - hi.astype(F32)
    mid = r.astype(BF16)
    lo = (r - mid.astype(F32)).astype(BF16)
    return hi, mid, lo


def _xdot_l(e, v):
    hi, mid, lo = _split3(v)
    return (jnp.dot(e, hi, preferred_element_type=F32) + jnp.dot(e, mid, preferred_element_type=F32)
            + jnp.dot(e, lo, preferred_element_type=F32))


def _xdot_r(v, e):
    hi, mid, lo = _split3(v)
    return (jnp.dot(hi, e, preferred_element_type=F32) + jnp.dot(mid, e, preferred_element_type=F32)
            + jnp.dot(lo, e, preferred_element_type=F32))


def _sigmoid(x):
    return 1.0 / (1.0 + jnp.exp(-x))


def _silu(x):
    return x * _sigmoid(x)


def _softplus(x):
    return jnp.maximum(x, 0.0) + jnp.log1p(jnp.exp(-jnp.abs(x)))


def _log_sigmoid(x):
    return jnp.minimum(x, 0.0) - jnp.log1p(jnp.exp(-jnp.abs(x)))


def _rms(x):
    return x * lax.rsqrt(jnp.mean(x * x, axis=-1, keepdims=True) + EPS)


def _iota(shape, dim):
    return lax.broadcasted_iota(jnp.int32, shape, dim)


def _rope(x, cos, sin_signed):
    lane = _iota(x.shape, 1)
    first = (lane % 32) < 16
    xr = jnp.where(first, pltpu.roll(x, 112, 1), pltpu.roll(x, 16, 1))
    return x * cos + xr * sin_signed


def _mod_index(tile_rows):
    n_ctx_tiles = N_CTX // tile_rows
    per_sample = DEC_SEQ // tile_rows

    def f(i):
        return jnp.where(i < n_ctx_tiles, 0, 1 + (i - n_ctx_tiles) // per_sample)

    return f


def _win_prep_kernel(w_ref, o_ref):
    w = w_ref[...].astype(BF16)
    col = 0
    for s, n in _IN_SEGMENTS:
        o_ref[:, col:col + n] = jnp.zeros((w.shape[0], n), BF16) if s is None else w[:, s:s + n]
        col += n


def _prep_w_in(w_in):
    tr = 256
    return pl.pallas_call(
        _win_prep_kernel,
        grid=(DEPTH, D_MODEL // tr),
        in_specs=[pl.BlockSpec((None, tr, IN_COLS), lambda l, i: (l, i, 0))],
        out_specs=pl.BlockSpec((None, tr, U_COLS), lambda l, i: (l, i, 0)),
        out_shape=jax.ShapeDtypeStruct((DEPTH, D_MODEL, U_COLS), BF16),
        compiler_params=_cparams(("arbitrary", "arbitrary")),
        name="w_in_layout",
    )(w_in)


def _mod_kernel(c_ref, w_ref, b_ref, o_ref):
    c = c_ref[...]
    s = _silu(c)
    s_hi = s.astype(BF16)
    s_lo = (s - s_hi.astype(F32)).astype(BF16)
    w = w_ref[...]
    w_hi = w.astype(BF16)
    w_lo = (w - w_hi.astype(F32)).astype(BF16)
    acc = jnp.dot(s_hi, w_hi, preferred_element_type=F32)
    acc = acc + jnp.dot(s_hi, w_lo, preferred_element_type=F32)
    acc = acc + jnp.dot(s_lo, w_hi, preferred_element_type=F32)
    o_ref[...] = acc + b_ref[...]


def _modulation(cvec, w_ada, b_ada):
    tn = 1536
    n = 6 * D_MODEL
    return pl.pallas_call(
        _mod_kernel,
        grid=(DEPTH, n // tn),
        in_specs=[
            pl.BlockSpec((8, D_MODEL), lambda l, j: (0, 0)),
            pl.BlockSpec((None, D_MODEL, tn), lambda l, j: (l, 0, j)),
            pl.BlockSpec((None, 1, tn), lambda l, j: (l, 0, j)),
        ],
        out_specs=pl.BlockSpec((None, 8, tn), lambda l, j: (l, 0, j)),
        out_shape=jax.ShapeDtypeStruct((DEPTH, 8, n), F32),
        compiler_params=_cparams(("arbitrary", "arbitrary")),
        name="adaln_mod",
    )(cvec, w_ada, b_ada.reshape(DEPTH, 1, n))


ROW_SUB = 256


def _norm_mod_to(h_ref, x_ref, gain_ref, shift, scale, rows):
    def body(s, carry):
        r = pl.multiple_of(s * ROW_SUB, ROW_SUB)
        x = x_ref[pl.ds(r, ROW_SUB), :]
        h = (_rms(x) * gain_ref[...]) * (1.0 + scale) + shift
        h_ref[pl.ds(r, ROW_SUB), :] = h.astype(BF16)
        return carry

    lax.fori_loop(0, rows // ROW_SUB, body, 0)


def _inproj_kernel(x_ref, mod_ref, gain_ref, w_ref, u_ref, h_ref, *, tm):
    @pl.when(pl.program_id(1) == 0)
    def _():
        shift = mod_ref[:, 0:D_MODEL]
        scale = mod_ref[:, D_MODEL:2 * D_MODEL]
        _norm_mod_to(h_ref, x_ref, gain_ref, shift, scale, tm)

    u_ref[...] = jnp.dot(h_ref[...], w_ref[...], preferred_element_type=F32)


def _inproj(x, mod, gain, w, l):
    tm, tn = 2048, 768
    midx = _mod_index(tm)
    return pl.pallas_call(
        functools.partial(_inproj_kernel, tm=tm),
        grid=(N_TOK // tm, U_COLS // tn),
        in_specs=[
            pl.BlockSpec((tm, D_MODEL), lambda i, j: (i, 0)),
            pl.BlockSpec((None, None, 1, 6 * D_MODEL), lambda i, j: (l, midx(i), 0, 0)),
            pl.BlockSpec((None, 1, D_MODEL), lambda i, j: (l, 0, 0)),
            pl.BlockSpec((None, D_MODEL, tn), lambda i, j: (l, 0, j)),
        ],
        out_specs=pl.BlockSpec((tm, tn), lambda i, j: (i, j)),
        out_shape=jax.ShapeDtypeStruct((N_TOK, U_COLS), F32),
        scratch_shapes=[pltpu.VMEM((tm, D_MODEL), BF16)],
        compiler_params=_cparams(("arbitrary", "arbitrary")),
        name="in_proj",
    )(x, mod, gain, w)


def _outproj_kernel(*refs, n_ctx_tiles):
    ctx_refs = refs[0:4]
    dec_refs = refs[4:8]
    x_ref, mod_ref, w_ref, o_ref = refs[8:12]
    i = pl.program_id(0)

    def compute(ys):
        acc = jnp.dot(ys[0][...], w_ref[0:512, :], preferred_element_type=F32)
        for g in range(1, 4):
            acc = acc + jnp.dot(ys[g][...], w_ref[g * 512:(g + 1) * 512, :], preferred_element_type=F32)
        gate = mod_ref[:, 2 * D_MODEL:3 * D_MODEL]
        o_ref[...] = x_ref[...] + gate * acc

    @pl.when(i < n_ctx_tiles)
    def _():
        compute(ctx_refs)

    @pl.when(i >= n_ctx_tiles)
    def _():
        compute(dec_refs)


def _outproj(ys_ctx, ys_dec, x, mod, w, l):
    tm = 512
    nct = N_CTX // tm
    midx = _mod_index(tm)
    ctx_spec = pl.BlockSpec((tm, GROUP_W), lambda i: (jnp.minimum(i, nct - 1), 0))
    dec_spec = pl.BlockSpec((tm, GROUP_W), lambda i: (jnp.maximum(i - nct, 0), 0))
    return pl.pallas_call(
        functools.partial(_outproj_kernel, n_ctx_tiles=nct),
        grid=(N_TOK // tm,),
        in_specs=[ctx_spec] * 4 + [dec_spec] * 4 + [
            pl.BlockSpec((tm, D_MODEL), lambda i: (i, 0)),
            pl.BlockSpec((None, None, 1, 6 * D_MODEL), lambda i: (l, midx(i), 0, 0)),
            pl.BlockSpec((None, D_MIX, D_MODEL), lambda i: (l, 0, 0)),
        ],
        out_specs=pl.BlockSpec((tm, D_MODEL), lambda i: (i, 0)),
        out_shape=jax.ShapeDtypeStruct((N_TOK, D_MODEL), F32),
        compiler_params=_cparams(("arbitrary",)),
        name="out_proj",
    )(*ys_ctx, *ys_dec, x, mod, w)


def _ffn_kernel(x_ref, mod_ref, gain_ref, wg_ref, wu_ref, wo_ref, o_ref, h_ref, acc_ref, *, tm, n_f):
    j = pl.program_id(1)

    def partial(h):
        gate = jnp.dot(h, wg_ref[...], preferred_element_type=F32)
        up = jnp.dot(h, wu_ref[...], preferred_element_type=F32)
        act = (_silu(gate) * up).astype(BF16)
        return jnp.dot(act, wo_ref[...], preferred_element_type=F32)

    @pl.when(j == 0)
    def _():
        shift = mod_ref[:, 3 * D_MODEL:4 * D_MODEL]
        scale = mod_ref[:, 4 * D_MODEL:5 * D_MODEL]

        def body(s, carry):
            r = pl.multiple_of(s * ROW_SUB, ROW_SUB)
            x = x_ref[pl.ds(r, ROW_SUB), :]
            h = ((_rms(x) * gain_ref[...]) * (1.0 + scale) + shift).astype(BF16)
            h_ref[pl.ds(r, ROW_SUB), :] = h
            acc_ref[pl.ds(r, ROW_SUB), :] = partial(h)
            return carry

        lax.fori_loop(0, tm // ROW_SUB, body, 0)

    @pl.when(j == n_f - 1)
    def _():
        g2 = mod_ref[:, 5 * D_MODEL:6 * D_MODEL]

        def body(s, carry):
            r = pl.multiple_of(s * ROW_SUB, ROW_SUB)
            y = acc_ref[pl.ds(r, ROW_SUB), :] + partial(h_ref[pl.ds(r, ROW_SUB), :])
            o_ref[pl.ds(r, ROW_SUB), :] = x_ref[pl.ds(r, ROW_SUB), :] + g2 * y
            return carry

        lax.fori_loop(0, tm // ROW_SUB, body, 0)


def _ffn(x, mod, gain, w_in, w_out, l):
    tm, tf = 1024, D_FF // 2
    n_f = D_FF // tf
    midx = _mod_index(tm)
    return pl.pallas_call(
        functools.partial(_ffn_kernel, tm=tm, n_f=n_f),
        grid=(N_TOK // tm, n_f),
        in_specs=[
            pl.BlockSpec((tm, D_MODEL), lambda i, j: (i, 0)),
            pl.BlockSpec((None, None, 1, 6 * D_MODEL), lambda i, j: (l, midx(i), 0, 0)),
            pl.BlockSpec((None, 1, D_MODEL), lambda i, j: (l, 0, 0)),
            pl.BlockSpec((None, D_MODEL, tf), lambda i, j: (l, 0, j)),
            pl.BlockSpec((None, D_MODEL, tf), lambda i, j: (l, 0, j + n_f)),
            pl.BlockSpec((None, tf, D_MODEL), lambda i, j: (l, j, 0)),
        ],
        out_specs=pl.BlockSpec((tm, D_MODEL), lambda i, j: (i, 0)),
        out_shape=jax.ShapeDtypeStruct((N_TOK, D_MODEL), F32),
        scratch_shapes=[pltpu.VMEM((tm, D_MODEL), BF16), pltpu.VMEM((tm, D_MODEL), F32)],
        compiler_params=_cparams(("arbitrary", "arbitrary")),
        name="ffn",
    )(x, mod, gain, w_in, w_in, w_out)


def _final_norm_kernel(x_ref, g_ref, o_ref):
    o_ref[...] = _rms(x_ref[...]) * g_ref[...]


def _final_norm(x, gain, row0, rows):
    tm = 512
    off = row0 // tm
    return pl.pallas_call(
        _final_norm_kernel,
        grid=(rows // tm,),
        in_specs=[pl.BlockSpec((tm, D_MODEL), lambda i: (i + off, 0)),
                  pl.BlockSpec((1, D_MODEL), lambda i: (0, 0))],
        out_specs=pl.BlockSpec((tm, D_MODEL), lambda i: (i, 0)),
        out_shape=jax.ShapeDtypeStruct((rows, D_MODEL), F32),
        compiler_params=_cparams(("arbitrary",)),
        name="final_norm",
    )(x, gain)


def _causal_masks():
    i = _iota((CHUNK, CHUNK), 0)
    j = _iota((CHUNK, CHUNK), 1)
    return j <= i, j >= i


def _conv_silu_chunk(pad_ref, w_ref, b_ref, r):
    win = pad_ref[pl.ds(r, CHUNK + 16), :]
    acc = b_ref[...] + w_ref[0:1, :] * win[6:6 + CHUNK, :]
    for tap in range(1, D_CONV):
        acc = acc + w_ref[tap:tap + 1, :] * win[6 + tap:6 + tap + CHUNK, :]
    return _silu(acc)


def _fill_padded(pad_ref, src_ref, seq, width, n_seq=1):
    zeros = jnp.zeros((8, width), F32)
    for i in range(n_seq):
        base = i * (seq + 16)
        pad_ref[base:base + 8, :] = zeros
        pad_ref[base + seq + 8:base + seq + 16, :] = zeros

        def body(c, carry, base=base, src0=i * seq):
            r = pl.multiple_of(c * CHUNK, CHUNK)
            pad_ref[pl.ds(base + 8 + r, CHUNK), :] = src_ref[pl.ds(src0 + r, CHUNK), :]
            return carry

        lax.fori_loop(0, seq // CHUNK, body, 0)


SSD_CPI = 2


def _ssd_kernel(*refs, seq, has_state, n_carry=0):
    (z_ref, xbc_ref, dt_ref, cw_ref, cb_ref, alog_ref, dtb_ref, dsk_ref, ng_ref) = refs[0:9]
    refs = refs[9 + n_carry:]
    if has_state:
        h0_ref, y_ref = refs[0:2]
        hfin_ref = None
    else:
        h0_ref = None
        y_ref, hfin_ref = refs[0:2]
    pad_ref, xc_ref, yf_ref, yb_ref, st_ref = refs[2:7]
    nc = seq // CHUNK
    hpg = SSM_HEADS // 2
    gw = hpg * SSM_HEAD_DIM

    _fill_padded(pad_ref, xbc_ref, seq, SSM_CONV_CH)

    def conv_body(c, carry):
        r = pl.multiple_of(c * CHUNK, CHUNK)
        xc_ref[pl.ds(r, CHUNK), :] = _conv_silu_chunk(pad_ref, cw_ref, cb_ref, r)
        return carry

    lax.fori_loop(0, nc, conv_body, 0)

    zero_blk = jnp.zeros((SSM_STATE, SSM_HEAD_DIM), F32)
    for d in range(2):
        for g in range(2):
            if has_state:
                rows = []
                for h4 in range(hpg):
                    blk = h0_ref[d, g * hpg + h4].T
                    rows.append(jnp.concatenate([blk if k == h4 else zero_blk for k in range(hpg)], axis=1))
                st_ref[d, g] = jnp.concatenate(rows, axis=0)
            else:
                st_ref[d, g] = jnp.zeros((gw, gw), F32)

    mask_f, mask_b = _causal_masks()
    masks = (mask_f, mask_b)
    tmats = (mask_f.astype(BF16), mask_b.astype(BF16))
    a_neg = -jnp.exp(alog_ref[...])
    lane_head = _iota((CHUNK, gw), 1) // SSM_HEAD_DIM
    blk_diag = (_iota((gw, gw), 0) // SSM_STATE) == (_iota((gw, gw), 1) // SSM_HEAD_DIM)

    def step(items):
        xcs = [xc_ref[pl.ds(r, CHUNK), :] for d, r in items]
        dts = [_softplus(dt_ref[pl.ds(r, CHUNK), :] + dtb_ref[...]) for d, r in items]
        cum_c = [_xdot_l(tmats[d], dts[n] * a_neg) for n, (d, r) in enumerate(items)]
        gms = [[_bdot_t(xcs[n][:, 640 + g * 64:640 + (g + 1) * 64], xcs[n][:, 512 + g * 64:512 + (g + 1) * 64])
                for g in range(2)] for n in range(len(items))]
        lhs_d, rhs_d, lhs_o, lhs_s, xgs, elcols = [], [], [], [], [], []
        for n, (d, r) in enumerate(items):
            cum_r = cum_c[n].T
            dt_r = dts[n].T
            last = cum_c[n][CHUNK - 1:CHUNK, :] if d == 0 else cum_c[n][0:1, :]
            e_last = jnp.exp(last)
            for g in range(2):
                xg = xcs[n][:, g * gw:(g + 1) * gw]
                bt = xcs[n][:, 512 + g * 64:512 + (g + 1) * 64].T
                cm = xcs[n][:, 640 + g * 64:640 + (g + 1) * 64]
                s_l, cec_l, btw_l, xm_l, el_l = [], [], [], [], []
                for h4 in range(hpg):
                    col = d * SSM_HEADS + g * hpg + h4
                    cc = jnp.broadcast_to(cum_c[n][:, col:col + 1], (CHUNK, CHUNK))
                    cr = cum_r[col:col + 1, :]
                    dtr = dt_r[col:col + 1, :]
                    decay = jnp.exp(jnp.where(masks[d], cc - cr, -jnp.inf))
                    s_l.append((decay * gms[n][g] * dtr).astype(BF16))
                    cec_l.append((cm * jnp.exp(cc[:, 0:SSM_STATE])).astype(BF16))
                    btw_l.append((bt * (jnp.exp(last[:, col:col + 1] - cr) * dtr)).astype(BF16))
                    xm_l.append(jnp.where(lane_head == h4, xg, 0.0).astype(BF16))
                    el_l.append(jnp.broadcast_to(e_last[:, col:col + 1], (SSM_STATE, gw)))
                lhs_d.append(jnp.concatenate(s_l, axis=1))
                rhs_d.append(jnp.concatenate(xm_l, axis=0))
                lhs_o.append(jnp.concatenate(cec_l, axis=1))
                lhs_s.append(jnp.concatenate(btw_l, axis=0))
                xgs.append(xg.astype(BF16))
                elcols.append(jnp.concatenate(el_l, axis=0))
        yds = [jnp.dot(lhs_d[k], rhs_d[k], preferred_element_type=F32) for k in range(len(lhs_d))]
        css = [jnp.dot(lhs_s[k], xgs[k], preferred_element_type=F32) for k in range(len(lhs_s))]
        outs = []
        for n, (d, r) in enumerate(items):
            ys = []
            for g in range(2):
                k = 2 * n + g
                st = st_ref[d, g]
                ys.append(yds[k] + jnp.dot(lhs_o[k], st.astype(BF16), preferred_element_type=F32))
                st_ref[d, g] = elcols[k] * st + jnp.where(blk_diag, css[k], 0.0)
            outs.append(jnp.concatenate(ys, axis=1))
        return outs

    def scan_body(s, carry):
        cf = SSD_CPI * s
        cb = nc - 1 - cf
        rows_f = [pl.multiple_of((cf + t) * CHUNK, CHUNK) for t in range(SSD_CPI)]
        rows_b = [pl.multiple_of((cb - t) * CHUNK, CHUNK) for t in range(SSD_CPI)]
        ys = step([(0, r) for r in rows_f] + [(1, r) for r in rows_b])
        for t in range(SSD_CPI):
            yf_ref[pl.ds(rows_f[t], CHUNK), :] = ys[t]
            yb_ref[pl.ds(rows_b[t], CHUNK), :] = ys[SSD_CPI + t]
        return carry

    lax.fori_loop(0, nc // SSD_CPI, scan_body, 0)

    def out_body(c, carry):
        r = pl.multiple_of(c * CHUNK, CHUNK)
        x = xc_ref[pl.ds(r, CHUNK), 0:GROUP_W]
        y = yf_ref[pl.ds(r, CHUNK), :] + yb_ref[pl.ds(r, CHUNK), :] + dsk_ref[...] * x
        y = y * _silu(z_ref[pl.ds(r, CHUNK), :])
        y_ref[pl.ds(r, CHUNK), :] = (_rms(y) * ng_ref[...]).astype(BF16)
        return carry

    lax.fori_loop(0, nc, out_body, 0)

    if hfin_ref is not None:
        for d in range(2):
            for g in range(2):
                st = st_ref[d, g]
                for h4 in range(hpg):
                    lo = h4 * SSM_STATE
                    hfin_ref[d, g * hpg + h4] = st[lo:lo + SSM_STATE, lo:lo + SSM_HEAD_DIM].T


def _carried(carry):
    if carry is None:
        return [], []
    return [pl.BlockSpec(memory_space=pl.ANY)] * len(carry), list(carry)


def _mixer_ssd(u, row0, nb, seq, conv_w, conv_b, a_log, dt_bias, d_skip, norm_g, h0, layer=0, carry=None):
    has_state = h0 is not None
    roff = row0 // seq
    vec = lambda n: pl.BlockSpec((1, n), lambda b: (0, 0))
    in_specs = [
        pl.BlockSpec((seq, GROUP_W), lambda b: (b + roff, A_Z // GROUP_W)),
        pl.BlockSpec((seq, SSM_CONV_CH), lambda b: (b + roff, A_XBC // SSM_CONV_CH)),
        pl.BlockSpec((seq, 128), lambda b: (b + roff, A_DT // 128)),
        pl.BlockSpec((8, SSM_CONV_CH), lambda b: (0, 0)),
        vec(SSM_CONV_CH), vec(128), vec(128), vec(GROUP_W), vec(GROUP_W),
    ]
    args = [u, u, u, conv_w, conv_b, a_log, dt_bias, d_skip, norm_g]
    st_block = (None, 2, SSM_HEADS, SSM_HEAD_DIM, SSM_STATE)
    y_spec = pl.BlockSpec((seq, GROUP_W), lambda b: (b, 0))
    y_shape = jax.ShapeDtypeStruct((nb * seq, GROUP_W), BF16)
    aliases = {}
    n_carry = 0
    if has_state:
        in_specs.append(pl.BlockSpec(st_block, lambda b: (b, 0, 0, 0, 0)))
        args.append(h0)
        out_specs, out_shape = y_spec, y_shape
    else:
        c_specs, c_args = _carried(carry)
        n_carry = len(c_args)
        aliases = {len(args) + k: 1 + k for k in range(n_carry)}
        in_specs += c_specs
        args += c_args
        out_specs = [y_spec, pl.BlockSpec((None,) + st_block, lambda b: (b, layer, 0, 0, 0, 0))]
        out_shape = [y_shape, jax.ShapeDtypeStruct((nb, DEPTH, 2, SSM_HEADS, SSM_HEAD_DIM, SSM_STATE), F32)]
    return pl.pallas_call(
        functools.partial(_ssd_kernel, seq=seq, has_state=has_state, n_carry=n_carry),
        grid=(nb,),
        in_specs=in_specs,
        out_specs=out_specs,
        out_shape=out_shape,
        input_output_aliases=aliases,
        scratch_shapes=[
            pltpu.VMEM((seq + 16, SSM_CONV_CH), F32),
            pltpu.VMEM((seq, SSM_CONV_CH), F32),
            pltpu.VMEM((seq, GROUP_W), F32),
            pltpu.VMEM((seq, GROUP_W), F32),
            pltpu.VMEM((2, 2, GROUP_W // 2, GROUP_W // 2), F32),
        ],
        compiler_params=_cparams(("arbitrary",)),
        name="mixer_ssd_dec" if has_state else "mixer_ssd_ctx",
    )(*args)


MLSTM_CPI = 4


def _mlstm_kernel(*refs, seq, n_seq, has_state, n_carry=0):
    (q_ref, k_ref, v_ref, o_ref, g_ref, cwq_ref, cwk_ref, cbq_ref, cbk_ref, gb_ref, ng_ref) = refs[0:11]
    refs = refs[11 + n_carry:]
    if has_state:
        c0_ref, n0_ref, m0_ref, y_ref = refs[0:4]
        outs = None
    else:
        y_ref = refs[0]
        outs = refs[1:4]
    (qpad_ref, kpad_ref, qc_ref, hf_ref, hb_ref, a_ref, cl_ref, rs_ref, rm_ref, cu_ref, rows_ref,
     cst_ref, nst_ref, mst_ref) = refs[4:]
    nc = seq // CHUNK
    head = pl.program_id(1)

    _fill_padded(qpad_ref, q_ref, seq, MLSTM_HEAD_DIM, n_seq)
    _fill_padded(kpad_ref, k_ref, seq, MLSTM_HEAD_DIM, n_seq)

    mask_f, mask_b = _causal_masks()
    row_id = _iota((CHUNK, CHUNK), 0)
    full = (CHUNK, CHUNK)

    sel = jnp.concatenate(
        [(row_id == col).astype(BF16) for col in (head, 4 + head, 8 + head, 12 + head)], axis=1)
    tmats = (mask_f.astype(BF16), mask_b.astype(BF16))
    masks = (mask_f, mask_b)
    forget_col = (_iota((CHUNK, CHUNK), 1) % 8) >= MLSTM_HEADS

    def local_body(s, carry):
        chunks = tuple(MLSTM_CPI * s + t for t in range(MLSTM_CPI))
        qs, ks, vs, qks, gsel = [], [], [], [], []
        for c in chunks:
            r = c * CHUNK if isinstance(c, int) else pl.multiple_of(c * CHUNK, CHUNK)
            rp = r if n_seq == 1 else c * CHUNK + 16 * (c // nc)
            q = _conv_silu_chunk(qpad_ref, cwq_ref, cbq_ref, rp) * (MLSTM_HEAD_DIM ** -0.5)
            k = _conv_silu_chunk(kpad_ref, cwk_ref, cbk_ref, rp)
            qc_ref[pl.ds(r, CHUNK), :] = q
            qs.append(q)
            ks.append(k)
            vs.append(v_ref[pl.ds(r, CHUNK), :])
            g = g_ref[pl.ds(r, CHUNK), :] + gb_ref[...]
            g = jnp.where(forget_col, _log_sigmoid(g), g)
            gsel.append([jnp.dot(p, sel, preferred_element_type=F32) for p in _split3(g)])
        for i in range(MLSTM_CPI):
            qks.append(_bdot_t(qs[i], ks[i]))
        items = [(i, d) for i in range(MLSTM_CPI) for d in range(2)]
        li_c = [gsel[i][0][:, 256 * d:256 * d + 128] + gsel[i][1][:, 256 * d:256 * d + 128]
                + gsel[i][2][:, 256 * d:256 * d + 128] for i, d in items]
        cum_c = []
        for i, d in items:
            lf = [gsel[i][p][:, 256 * d + 128:256 * d + 256].astype(BF16) for p in range(3)]
            cum_c.append(jnp.dot(tmats[d], lf[0], preferred_element_type=F32)
                         + jnp.dot(tmats[d], lf[1], preferred_element_type=F32)
                         + jnp.dot(tmats[d], lf[2], preferred_element_type=F32))
        cum_r = [x.T for x in cum_c]
        li_r = [x.T for x in li_c]
        last = [cum_c[n][CHUNK - 1:CHUNK, :] if d == 0 else cum_c[n][0:1, :]
                for n, (i, d) in enumerate(items)]
        dmat = [jnp.where(masks[d], cum_c[n] - cum_r[n] + li_r[n], -jnp.inf) for n, (i, d) in enumerate(items)]
        rowmax = [jnp.max(x, axis=1, keepdims=True) for x in dmat]
        sp = [qks[i] * jnp.exp(dmat[n] - rowmax[n]) for n, (i, d) in enumerate(items)]
        m_loc = [jnp.max(last[n] - cum_r[n][0:1, :] + li_r[n][0:1, :], axis=1, keepdims=True)
                 for n in range(len(items))]
        kw = [ks[i] * jnp.exp(last[n] - cum_c[n] + li_c[n] - m_loc[n]) for n, (i, d) in enumerate(items)]
        kwt = [x.T for x in kw]
        a_loc = [_bdot(sp[n], vs[i]) for n, (i, d) in enumerate(items)]
        c_loc = [_bdot(kwt[n], vs[i]) for n, (i, d) in enumerate(items)]
        for n, (i, d) in enumerate(items):
            c = chunks[i]
            a_ref[c, d] = a_loc[n]
            cl_ref[c, d] = c_loc[n]
            rs_ref[c, d] = jnp.broadcast_to(jnp.sum(sp[n], axis=1, keepdims=True), full)
            rm_ref[c, d] = jnp.broadcast_to(rowmax[n], full)
            cu_ref[c, d] = cum_c[n]
            rows_ref[c, d, 0:1, :] = jnp.sum(kw[n], axis=0, keepdims=True)
            rows_ref[c, d, 1:2, :] = jnp.broadcast_to(m_loc[n], (1, CHUNK))
            rows_ref[c, d, 2:3, :] = last[n]
        return carry

    n_local = n_seq * nc // MLSTM_CPI
    if n_local == 1:
        local_body(0, 0)
    else:
        lax.fori_loop(0, n_local, local_body, 0)

    if has_state:
        cst_ref[...] = c0_ref[...]
        nst_ref[...] = n0_ref[...]
        mst_ref[...] = m0_ref[...]
    else:
        cst_ref[...] = jnp.zeros(cst_ref.shape, F32)
        nst_ref[...] = jnp.zeros(nst_ref.shape, F32)
        mst_ref[...] = jnp.zeros(mst_ref.shape, F32)

    def state_step(i, d, lc):
        c = i * nc + lc
        r = pl.multiple_of(c * CHUNK, CHUNK)
        q = qc_ref[pl.ds(r, CHUNK), :]
        c_p = cst_ref[i, d]
        n_p = nst_ref[i, d]
        m_p = mst_ref[i, d]
        n_loc = rows_ref[c, d, 0:1, :]
        m_loc = rows_ref[c, d, 1:2, :]
        last = rows_ref[c, d, 2:3, :]
        rowmax = rm_ref[c, d]
        inter = cu_ref[c, d] + m_p
        m_t = jnp.maximum(inter, rowmax)
        f_intra = jnp.exp(rowmax - m_t)
        w_inter = jnp.exp(inter - m_t)
        num = a_ref[c, d] * f_intra + w_inter * _bdot(q, c_p)
        den = rs_ref[c, d] * f_intra + w_inter * jnp.sum(q * n_p, axis=1, keepdims=True)
        hh = num / jnp.maximum(jnp.abs(den), jnp.exp(-m_t))
        m_new = jnp.maximum(last + m_p, m_loc)
        s_p = jnp.exp(last + m_p - m_new)
        s_l = jnp.exp(m_loc - m_new)
        cst_ref[i, d] = s_p[:, 0:1] * c_p + s_l[:, 0:1] * cl_ref[c, d]
        nst_ref[i, d] = s_p * n_p + s_l * n_loc
        mst_ref[i, d] = m_new
        return r, hh

    def state_body(s, carry):
        for i in range(n_seq):
            r, hh = state_step(i, 0, s)
            hf_ref[pl.ds(r, CHUNK), :] = hh
            r, hh = state_step(i, 1, nc - 1 - s)
            hb_ref[pl.ds(r, CHUNK), :] = hh
        return carry

    lax.fori_loop(0, nc, state_body, 0, unroll=2)

    def out_body(c, carry):
        r = pl.multiple_of(c * CHUNK, CHUNK)
        hsum = hf_ref[pl.ds(r, CHUNK), :] + hb_ref[pl.ds(r, CHUNK), :]
        y = _sigmoid(o_ref[pl.ds(r, CHUNK), :]) * (_rms(hsum) * ng_ref[...])
        y_ref[pl.ds(r, CHUNK), :] = y.astype(BF16)
        return carry

    lax.fori_loop(0, n_seq * nc, out_body, 0, unroll=2)

    if outs is not None:
        outs[0][...] = cst_ref[...]
        outs[1][...] = nst_ref[...]
        outs[2][...] = mst_ref[...]


def _mixer_mlstm(u, row0, nb, seq, conv_w, conv_b, gate_b, norm_g, state, layer=0, carry=None):
    has_state = state is not None
    hd = MLSTM_HEAD_DIM
    nc = seq // CHUNK
    n_seq = max(1, MLSTM_CPI // nc)
    rows = n_seq * seq
    gnc = n_seq * nc
    roff = row0 // rows
    col = lambda base: (lambda b, h: (b + roff, base // hd + h))
    in_specs = [
        pl.BlockSpec((rows, hd), col(C_QK)),
        pl.BlockSpec((rows, hd), col(C_QK + GROUP_W)),
        pl.BlockSpec((rows, hd), col(C_V)),
        pl.BlockSpec((rows, hd), col(C_O)),
        pl.BlockSpec((rows, 128), lambda b, h: (b + roff, C_G // 128)),
        pl.BlockSpec((8, hd), lambda b, h: (0, h)),
        pl.BlockSpec((8, hd), lambda b, h: (0, MLSTM_HEADS + h)),
        pl.BlockSpec((1, hd), lambda b, h: (0, h)),
        pl.BlockSpec((1, hd), lambda b, h: (0, MLSTM_HEADS + h)),
        pl.BlockSpec((1, 128), lambda b, h: (0, 0)),
        pl.BlockSpec((1, hd), lambda b, h: (0, h)),
    ]
    args = [u, u, u, u, u, conv_w, conv_w, conv_b, conv_b, gate_b, norm_g]
    c_spec = pl.BlockSpec((n_seq, 2, None, hd, hd), lambda b, h: (b, 0, h, 0, 0))
    n_spec = pl.BlockSpec((n_seq, 2, None, 1, hd), lambda b, h: (b, 0, h, 0, 0))
    y_spec = pl.BlockSpec((rows, hd), lambda b, h: (b, h))
    y_shape = jax.ShapeDtypeStruct((nb * seq, GROUP_W), BF16)
    aliases = {}
    n_carry = 0
    if has_state:
        in_specs += [c_spec, n_spec, n_spec]
        args += list(state)
        out_specs, out_shape = y_spec, y_shape
    else:
        c_specs, c_args = _carried(carry)
        n_carry = len(c_args)
        aliases = {len(args) + k: 1 + k for k in range(n_carry)}
        in_specs += c_specs
        args += c_args
        c_all = pl.BlockSpec((n_seq, None, 2, None, hd, hd), lambda b, h: (b, layer, 0, h, 0, 0))
        n_all = pl.BlockSpec((n_seq, None, 2, None, 1, hd), lambda b, h: (b, layer, 0, h, 0, 0))
        out_specs = [y_spec, c_all, n_all, n_all]
        out_shape = [y_shape,
                     jax.ShapeDtypeStruct((nb, DEPTH, 2, MLSTM_HEADS, hd, hd), F32),
                     jax.ShapeDtypeStruct((nb, DEPTH, 2, MLSTM_HEADS, 1, hd), F32),
                     jax.ShapeDtypeStruct((nb, DEPTH, 2, MLSTM_HEADS, 1, hd), F32)]
    return pl.pallas_call(
        functools.partial(_mlstm_kernel, seq=seq, n_seq=n_seq, has_state=has_state, n_carry=n_carry),
        grid=(nb // n_seq, MLSTM_HEADS),
        in_specs=in_specs,
        out_specs=out_specs,
        out_shape=out_shape,
        input_output_aliases=aliases,
        scratch_shapes=[
            pltpu.VMEM((rows + 16 * n_seq, hd), F32), pltpu.VMEM((rows + 16 * n_seq, hd), F32),
            pltpu.VMEM((rows, hd), F32), pltpu.VMEM((rows, hd), F32), pltpu.VMEM((rows, hd), F32),
            pltpu.VMEM((gnc, 2, hd, hd), F32), pltpu.VMEM((gnc, 2, hd, hd), F32),
            pltpu.VMEM((gnc, 2, hd, hd), F32), pltpu.VMEM((gnc, 2, hd, hd), F32), pltpu.VMEM((gnc, 2, hd, hd), F32),
            pltpu.VMEM((gnc, 2, 8, hd), F32),
            pltpu.VMEM((n_seq, 2, hd, hd), F32), pltpu.VMEM((n_seq, 2, 1, hd), F32), pltpu.VMEM((n_seq, 2, 1, hd), F32),
        ],
        compiler_params=_cparams(("arbitrary", "arbitrary")),
        name="mixer_mlstm_dec" if has_state else "mixer_mlstm_ctx",
    )(*args)


KV_SUB = 256


ATT_SUB = 256
DIFF_QB = 1024
GQA_QB = 512


def _softmax_parts(s):
    e = jnp.exp(s - jnp.max(s, axis=1, keepdims=True))
    return e.astype(BF16), jnp.sum(e, axis=1, keepdims=True)


def _diff_lambda(lam_ref, lam_init):
    lp = lam_ref[...]
    return (jnp.exp(jnp.sum(lp[0:1, :] * lp[1:2, :], axis=1, keepdims=True))
            - jnp.exp(jnp.sum(lp[2:3, :] * lp[3:4, :], axis=1, keepdims=True)) + lam_init)


def _exp_scores_t(s_t):
    e = jnp.exp(s_t - jnp.max(s_t, axis=0, keepdims=True))
    return e.astype(BF16), jnp.sum(e, axis=0, keepdims=True)


def _pv_t(vt, el):
    e, l = el
    return jnp.dot(vt, e, preferred_element_type=F32) / l


def _diff_dec_kernel(q_ref, k_ref, v_ref, lam_ref, ck_ref, cv_ref, cosq_ref, sinq_ref, cosk_ref, sink_ref,
                     o_ref, k1_ref, k2_ref, vt_ref, *, seq, lam_init):
    past = PAST_LEN

    @pl.when(pl.program_id(2) == 0)
    def _():
        ck = ck_ref[...]
        k1_ref[0:past, :] = ck[:, 0:64].astype(BF16)
        k2_ref[0:past, :] = ck[:, 64:128].astype(BF16)
        vt_ref[:, 0:past] = cv_ref[...].T.astype(BF16)
        for s in range(seq // KV_SUB):
            r = s * KV_SUB
            kk = _rope(k_ref[r:r + KV_SUB, :], cosk_ref[r:r + KV_SUB, :], sink_ref[r:r + KV_SUB, :])
            k1_ref[past + r:past + r + KV_SUB, :] = kk[:, 0:64].astype(BF16)
            k2_ref[past + r:past + r + KV_SUB, :] = kk[:, 64:128].astype(BF16)
            vt_ref[:, past + r:past + r + KV_SUB] = v_ref[r:r + KV_SUB, :].T.astype(BF16)

    lam = _diff_lambda(lam_ref, lam_init)
    q = _rope(q_ref[...], cosq_ref[...], sinq_ref[...]) * (64 ** -0.5)
    n_sub = q.shape[0] // ATT_SUB
    k_maps = (k1_ref[...], k2_ref[...])
    q_t = [q[j * ATT_SUB:(j + 1) * ATT_SUB, m * 64:(m + 1) * 64].T.astype(BF16)
           for j in range(n_sub) for m in range(2)]
    scores = [jnp.dot(k_maps[n % 2], q_t[n], preferred_element_type=F32) for n in range(2 * n_sub)]
    probs = [_exp_scores_t(s) for s in scores]
    vt = vt_ref[...]
    pv = [_pv_t(vt, el) for el in probs]
    for j in range(n_sub):
        o = (pv[2 * j] - lam * pv[2 * j + 1]).T
        o_ref[j * ATT_SUB:(j + 1) * ATT_SUB, :] = (_rms(o) * (1.0 - lam_init)).astype(BF16)


def _diff_ctx_kernel(*refs, lam_init, n_carry=0):
    q_ref, k_ref, v_ref, lam_ref = refs[0:4]
    o_ref, ko_ref, vo_ref = refs[4 + n_carry:]
    ko_ref[...] = k_ref[...]
    vo_ref[...] = v_ref[...]
    lam = _diff_lambda(lam_ref, lam_init)
    q = q_ref[...] * (64 ** -0.5)
    k = k_ref[...].astype(BF16)
    v = v_ref[...].astype(BF16)
    cols = [h * 128 + m * 64 for h in range(DIFF_HEADS) for m in range(2)]
    scores = [_bdot_t(q[:, c0:c0 + 64], k[:, c0:c0 + 64]) for c0 in cols]
    parts = [_softmax_parts(s) for s in scores]
    pv = [jnp.dot(parts[n][0], v[:, (n // 2) * 128:(n // 2 + 1) * 128], preferred_element_type=F32)
          for n in range(2 * DIFF_HEADS)]
    outs = []
    for h in range(DIFF_HEADS):
        o = pv[2 * h] / parts[2 * h][1] - lam * (pv[2 * h + 1] / parts[2 * h + 1][1])
        outs.append(_rms(o) * (1.0 - lam_init))
    o_ref[...] = jnp.concatenate(outs, axis=1).astype(BF16)


def _mixer_diff(u, row0, nb, seq, lam_rows, lam_init, ctx, layer=0, carry=None):
    roff = row0 // seq
    o_shape = jax.ShapeDtypeStruct((nb * seq, GROUP_W), BF16)
    lam_spec3 = pl.BlockSpec((8, 128), lambda b, h, i: (0, 0))
    if ctx is None:
        col = lambda base: pl.BlockSpec((seq, GROUP_W), lambda b: (b + roff, base // GROUP_W))
        c_specs, c_args = _carried(carry)
        slab = pl.BlockSpec((None, None, seq, GROUP_W), lambda b: (b, layer, 0, 0))
        slab_shape = jax.ShapeDtypeStruct((nb, DEPTH, seq, GROUP_W), F32)
        return pl.pallas_call(
            functools.partial(_diff_ctx_kernel, lam_init=lam_init, n_carry=len(c_args)),
            grid=(nb,),
            in_specs=[col(B_Q), col(B_K), col(B_V), pl.BlockSpec((8, 128), lambda b: (0, 0))] + c_specs,
            out_specs=[pl.BlockSpec((seq, GROUP_W), lambda b: (b, 0)), slab, slab],
            out_shape=[o_shape, slab_shape, slab_shape],
            input_output_aliases={4 + k: 1 + k for k in range(len(c_args))},
            compiler_params=_cparams(("arbitrary",)),
            name="mixer_diff_ctx",
        )(u, u, u, lam_rows, *c_args)
    ck, cv, cos, sin = ctx
    qb = min(DIFF_QB, seq)
    qoff = row0 // qb
    nq = seq // qb
    lk = seq + PAST_LEN
    in_specs = [
        pl.BlockSpec((qb, 128), lambda b, h, i: (qoff + b * nq + i, B_Q // 128 + h)),
        pl.BlockSpec((seq, 128), lambda b, h, i: (b + roff, B_K // 128 + h)),
        pl.BlockSpec((seq, 128), lambda b, h, i: (b + roff, B_V // 128 + h)),
        lam_spec3,
        pl.BlockSpec((None, PAST_LEN, 128), lambda b, h, i: (b, 0, h)),
        pl.BlockSpec((None, PAST_LEN, 128), lambda b, h, i: (b, 0, h)),
        pl.BlockSpec((qb, 128), lambda b, h, i: (i, 0)),
        pl.BlockSpec((qb, 128), lambda b, h, i: (i, 0)),
        pl.BlockSpec((seq, 128), lambda b, h, i: (0, 0)),
        pl.BlockSpec((seq, 128), lambda b, h, i: (0, 0)),
    ]
    return pl.pallas_call(
        functools.partial(_diff_dec_kernel, seq=seq, lam_init=lam_init),
        grid=(nb, DIFF_HEADS, nq),
        in_specs=in_specs,
        out_specs=pl.BlockSpec((qb, 128), lambda b, h, i: (b * nq + i, h)),
        out_shape=o_shape,
        scratch_shapes=[pltpu.VMEM((lk, 64), BF16), pltpu.VMEM((lk, 64), BF16), pltpu.VMEM((128, lk), BF16)],
        compiler_params=_cparams(("arbitrary", "arbitrary", "arbitrary")),
        name="mixer_diff_dec",
    )(u, u, u, lam_rows, ck, cv, cos, sin, cos, sin)


def _seg_rms_pair(x, gain):
    low = _iota(x.shape, 1) < 64
    sq = x * x
    ms_lo = jnp.sum(jnp.where(low, sq, 0.0), axis=1, keepdims=True) * (1.0 / 64)
    ms_hi = jnp.sum(jnp.where(low, 0.0, sq), axis=1, keepdims=True) * (1.0 / 64)
    return x * lax.rsqrt(jnp.where(low, ms_lo, ms_hi) + EPS) * gain


def _gqa_attend(q_heads, kf, vf):
    rows = q_heads[0].shape[0]
    s = _bdot_t(jnp.concatenate(q_heads, axis=0), kf)
    e, l = _softmax_parts(s)
    o = jnp.dot(e, vf, preferred_element_type=F32) / l
    return [o[g * rows:(g + 1) * rows, :] for g in range(len(q_heads))]


def _gqa_dec_kernel(q_ref, kv_ref, qg_ref, kg_ref, ck_ref, cv_ref, cosq_ref, sinq_ref, cosk_ref, sink_ref,
                    o_ref, kf_ref, vt_ref, *, seq):
    past = PAST_LEN

    @pl.when(pl.program_id(2) == 0)
    def _():
        kf_ref[0:past, :] = ck_ref[...].astype(BF16)
        vt_ref[:, 0:past] = cv_ref[...].T.astype(BF16)
        for s in range(seq // KV_SUB):
            r = s * KV_SUB
            blk = kv_ref[r:r + KV_SUB, :]
            kn = _rope(_seg_rms_pair(blk, kg_ref[...]), cosk_ref[r:r + KV_SUB, :], sink_ref[r:r + KV_SUB, :])
            kf_ref[past + r:past + r + KV_SUB, :] = kn[:, 0:64].astype(BF16)
            vt_ref[:, past + r:past + r + KV_SUB] = blk[:, 64:128].T.astype(BF16)

    n_sub = q_ref.shape[0] // ATT_SUB
    heads_t = []
    for j in range(n_sub):
        rows = slice(j * ATT_SUB, (j + 1) * ATT_SUB)
        for half in range(2):
            qh = _seg_rms_pair(q_ref[rows, half * 128:(half + 1) * 128], qg_ref[...])
            qh = _rope(qh, cosq_ref[rows, :], sinq_ref[rows, :]) * (64 ** -0.5)
            heads_t += [qh[:, 0:64].T.astype(BF16), qh[:, 64:128].T.astype(BF16)]
    kf = kf_ref[...]
    scores = [jnp.dot(kf, q_t, preferred_element_type=F32) for q_t in heads_t]
    probs = [_exp_scores_t(s) for s in scores]
    vt = vt_ref[...]
    outs = [_pv_t(vt, el).T for el in probs]
    for j in range(n_sub):
        o_ref[j * ATT_SUB:(j + 1) * ATT_SUB, :] = jnp.concatenate(outs[4 * j:4 * j + 4], axis=1).astype(BF16)


def _gqa_ctx_kernel(*refs, n_carry=0):
    q_ref, kv_ref, qg_ref, kg_ref = refs[0:4]
    o_ref, kn_ref, vo_ref = refs[4 + n_carry:]
    outs = []
    for kvh in range(GQA_KV_HEADS):
        blk = kv_ref[:, kvh * 128:(kvh + 1) * 128]
        kn = _seg_rms_pair(blk, kg_ref[...])[:, 0:64]
        kn_ref[:, kvh * 64:(kvh + 1) * 64] = kn
        vo_ref[:, kvh * 64:(kvh + 1) * 64] = blk[:, 64:128]
        heads = []
        for half in range(2):
            c0 = kvh * 256 + half * 128
            qh = _seg_rms_pair(q_ref[:, c0:c0 + 128], qg_ref[...]) * (64 ** -0.5)
            heads += [qh[:, 0:64], qh[:, 64:128]]
        outs += _gqa_attend(heads, kn.astype(BF16), blk[:, 64:128].astype(BF16))
    o_ref[...] = jnp.concatenate(outs, axis=1).astype(BF16)


def _mixer_gqa(u, row0, nb, seq, q_gain, k_gain, ctx, layer=0, carry=None):
    roff = row0 // seq
    o_shape = jax.ShapeDtypeStruct((nb * seq, GROUP_W), BF16)
    if ctx is None:
        gain = pl.BlockSpec((1, 128), lambda b: (0, 0))
        c_specs, c_args = _carried(carry)
        kvw = GQA_KV_HEADS * GQA_HEAD_DIM
        slab = pl.BlockSpec((None, None, seq, kvw), lambda b: (b, layer, 0, 0))
        slab_shape = jax.ShapeDtypeStruct((nb, DEPTH, seq, kvw), F32)
        return pl.pallas_call(
            functools.partial(_gqa_ctx_kernel, n_carry=len(c_args)),
            grid=(nb,),
            in_specs=[pl.BlockSpec((seq, GROUP_W), lambda b: (b + roff, D_Q // GROUP_W)),
                      pl.BlockSpec((seq, 256), lambda b: (b + roff, D_KV // 256)), gain, gain] + c_specs,
            out_specs=[pl.BlockSpec((seq, GROUP_W), lambda b: (b, 0)), slab, slab],
            out_shape=[o_shape, slab_shape, slab_shape],
            input_output_aliases={4 + k: 1 + k for k in range(len(c_args))},
            compiler_params=_cparams(("arbitrary",)),
            name="mixer_gqa_ctx",
        )(u, u, q_gain, k_gain, *c_args)
    ck, cv, cos, sin = ctx
    qb = GQA_QB
    qoff = row0 // qb
    nq = seq // qb
    lk = seq + PAST_LEN
    in_specs = [
        pl.BlockSpec((qb, 256), lambda b, h, i: (qoff + b * nq + i, D_Q // 256 + h)),
        pl.BlockSpec((seq, 128), lambda b, h, i: (b + roff, D_KV // 128 + h)),
        pl.BlockSpec((1, 128), lambda b, h, i: (0, 0)),
        pl.BlockSpec((1, 128), lambda b, h, i: (0, 0)),
        pl.BlockSpec((None, None, PAST_LEN, 64), lambda b, h, i: (b, h, 0, 0)),
        pl.BlockSpec((None, None, PAST_LEN, 64), lambda b, h, i: (b, h, 0, 0)),
        pl.BlockSpec((qb, 128), lambda b, h, i: (i, 0)),
        pl.BlockSpec((qb, 128), lambda b, h, i: (i, 0)),
        pl.BlockSpec((seq, 128), lambda b, h, i: (0, 0)),
        pl.BlockSpec((seq, 128), lambda b, h, i: (0, 0)),
    ]
    return pl.pallas_call(
        functools.partial(_gqa_dec_kernel, seq=seq),
        grid=(nb, GQA_KV_HEADS, nq),
        in_specs=in_specs,
        out_specs=pl.BlockSpec((qb, 256), lambda b, h, i: (b * nq + i, h)),
        out_shape=o_shape,
        scratch_shapes=[pltpu.VMEM((lk, 64), BF16), pltpu.VMEM((64, lk), BF16)],
        compiler_params=_cparams(("arbitrary", "arbitrary", "arbitrary")),
        name="mixer_gqa_dec",
    )(u, u, q_gain, k_gain, ck, cv, cos, sin, cos, sin)


def _rope_tables(n_tok):
    rows = n_tok // GRID_W
    r, c = jnp.meshgrid(jnp.arange(rows, dtype=F32), jnp.arange(GRID_W, dtype=F32), indexing='ij')
    nf = 16
    freqs = ROPE_THETA ** (-jnp.arange(nf, dtype=F32) / nf)
    ang = jnp.stack([r.reshape(-1)[:, None] * freqs, c.reshape(-1)[:, None] * freqs], axis=1)
    cos, sin = jnp.cos(ang), jnp.sin(ang)
    cos64 = jnp.concatenate([cos, cos], axis=2).reshape(n_tok, 64)
    sin64 = jnp.concatenate([-sin, sin], axis=2).reshape(n_tok, 64)
    return jnp.tile(cos64, (1, 2)), jnp.tile(sin64, (1, 2))


def _pad_lanes(v, n=128):
    v = v.reshape(1, -1)
    return jnp.pad(v, ((0, 0), (0, n - v.shape[1])))


def kernel(x_prompt, x_sample, c, cache_diff_k, cache_diff_v, cache_gqa_k, cache_gqa_v, state_ssm, state_mlstm_c, state_mlstm_n, state_mlstm_m, c_ctx, w_ada, b_ada, norm1, norm2, w_in, w_out, conv_ssd_w, conv_ssd_b, ssd_a_log, ssd_dt_bias, ssd_d, ssd_norm, diff_lq1, diff_lk1, diff_lq2, diff_lk2, conv_mlstm_w, conv_mlstm_b, mlstm_gate_b, mlstm_norm, gqa_q_norm, gqa_k_norm, w_ffn_in, w_ffn_out, norm_f):
    w_in_p = _prep_w_in(w_in)
    w_out_b = w_out.astype(BF16)
    gain1 = norm1.reshape(DEPTH, 1, D_MODEL)
    gain2 = norm2.reshape(DEPTH, 1, D_MODEL)
    w_ffn_in_b = w_ffn_in.astype(BF16)
    w_ffn_out_b = w_ffn_out.astype(BF16)
    cvec = jnp.concatenate([c_ctx[None, :], c, jnp.zeros((5, D_MODEL), F32)], axis=0)
    cos_t, sin_t = _rope_tables(DEC_SEQ)
    pad_taps = lambda w: jnp.pad(w, ((0, 8 - D_CONV), (0, 0)))

    mod = _modulation(cvec, w_ada, b_ada).reshape(DEPTH, 8, 1, 6 * D_MODEL)
    x = jnp.concatenate([x_prompt.reshape(N_CTX, D_MODEL), x_sample.reshape(N_DEC, D_MODEL)], axis=0)

    zeros = lambda *shape: jnp.zeros((BATCH, DEPTH) + shape, F32)
    ssm_all = (zeros(2, SSM_HEADS, SSM_HEAD_DIM, SSM_STATE),)
    diff_kv_all = [zeros(SEQ, GROUP_W), zeros(SEQ, GROUP_W)]
    mlstm_all = [zeros(2, MLSTM_HEADS, MLSTM_HEAD_DIM, MLSTM_HEAD_DIM), zeros(2, MLSTM_HEADS, 1, MLSTM_HEAD_DIM),
                 zeros(2, MLSTM_HEADS, 1, MLSTM_HEAD_DIM)]
    gqa_kv_all = [zeros(SEQ, GQA_KV_HEADS * GQA_HEAD_DIM), zeros(SEQ, GQA_KV_HEADS * GQA_HEAD_DIM)]
    for l in range(DEPTH):
        u = _inproj(x, mod, gain1, w_in_p, l)

        ssd_args = (pad_taps(conv_ssd_w[l]), conv_ssd_b[l].reshape(1, -1), _pad_lanes(ssd_a_log[l]),
                    _pad_lanes(ssd_dt_bias[l]), jnp.repeat(ssd_d[l], SSM_HEAD_DIM).reshape(1, GROUP_W),
                    ssd_norm[l].reshape(1, GROUP_W))
        ya_c, ssm_new = _mixer_ssd(u, 0, BATCH, SEQ, *ssd_args, None, l, ssm_all)
        ssm_all = (ssm_new,)
        ya_d = _mixer_ssd(u, N_CTX, DEC_BATCH, DEC_SEQ, *ssd_args, state_ssm[:, l])

        lam_init = 0.8 - 0.6 * math.exp(-0.3 * l)
        lam_rows = jnp.pad(jnp.stack([diff_lq1[l], diff_lk1[l], diff_lq2[l], diff_lk2[l]]), ((0, 4), (0, 64)))
        yb_c, *diff_kv_all = _mixer_diff(u, 0, BATCH, SEQ, lam_rows, lam_init, None, l, diff_kv_all)
        yb_d = _mixer_diff(u, N_CTX, DEC_BATCH, DEC_SEQ, lam_rows, lam_init,
                           (cache_diff_k[:, l].reshape(DEC_BATCH, PAST_LEN, GROUP_W),
                            cache_diff_v[:, l].reshape(DEC_BATCH, PAST_LEN, GROUP_W), cos_t, sin_t))

        ml_args = (pad_taps(conv_mlstm_w[l]), conv_mlstm_b[l].reshape(1, -1), _pad_lanes(mlstm_gate_b[l]),
                   mlstm_norm[l].reshape(1, GROUP_W))
        yc_c, *mlstm_all = _mixer_mlstm(u, 0, BATCH, SEQ, *ml_args, None, l, mlstm_all)
        m0 = jnp.broadcast_to(state_mlstm_m[:, l][..., None, None], (DEC_BATCH, 2, MLSTM_HEADS, 1, MLSTM_HEAD_DIM))
        yc_d = _mixer_mlstm(u, N_CTX, DEC_BATCH, DEC_SEQ, *ml_args,
                            (state_mlstm_c[:, l], state_mlstm_n[:, l][:, :, :, None, :], m0))

        q_gain = jnp.tile(gqa_q_norm[l], 2).reshape(1, 128)
        k_gain = jnp.tile(gqa_k_norm[l], 2).reshape(1, 128)
        yd_c, *gqa_kv_all = _mixer_gqa(u, 0, BATCH, SEQ, q_gain, k_gain, None, l, gqa_kv_all)
        yd_d = _mixer_gqa(u, N_CTX, DEC_BATCH, DEC_SEQ, q_gain, k_gain,
                          (cache_gqa_k[:, l].transpose(0, 2, 1, 3), cache_gqa_v[:, l].transpose(0, 2, 1, 3), cos_t, sin_t))

        x = _outproj((ya_c, yb_c, yc_c, yd_c), (ya_d, yb_d, yc_d, yd_d), x, mod, w_out_b, l)
        x = _ffn(x, mod, gain2, w_ffn_in_b, w_ffn_out_b, l)

    y_prompt = _final_norm(x, norm_f.reshape(1, D_MODEL), 0, N_CTX).reshape(BATCH, SEQ, D_MODEL)
    y_sample = _final_norm(x, norm_f.reshape(1, D_MODEL), N_CTX, N_DEC).reshape(DEC_BATCH, DEC_SEQ, D_MODEL)
    c_all, n_all, m_all = mlstm_all
    return (y_prompt, y_sample,
            diff_kv_all[0].reshape(BATCH, DEPTH, SEQ, DIFF_HEADS, 2, 64),
            diff_kv_all[1].reshape(BATCH, DEPTH, SEQ, DIFF_HEADS, 128),
            gqa_kv_all[0].reshape(BATCH, DEPTH, SEQ, GQA_KV_HEADS, GQA_HEAD_DIM),
            gqa_kv_all[1].reshape(BATCH, DEPTH, SEQ, GQA_KV_HEADS, GQA_HEAD_DIM),
            ssm_all[0], c_all, n_all[:, :, :, :, 0, :], m_all[:, :, :, :, 0, 0])
```

```python
import functools
import math

import numpy as np
import jax
import jax.numpy as jnp
from jax import lax
from jax.experimental import pallas as pl
from jax.experimental.pallas import tpu as pltpu

F32 = jnp.float32
BF16 = jnp.bfloat16

D_MODEL = 1024
BATCH = 16
SEQ = 256
DEPTH = 4
DEC_BATCH = 2
DEC_SEQ = 2048
PAST_LEN = 256
GRID_W = 64
GROUP_W = 512
D_MIX = 2048
CHUNK = 128
Q_BLOCK = 128
D_CONV = 5
ROPE_THETA = 10000.0
EPS = 1e-6
SSM_HEADS = 8
SSM_HEAD_DIM = 64
SSM_STATE = 64
SSM_CONV_CH = 768
DIFF_HEADS = 4
MLSTM_HEADS = 4
MLSTM_HEAD_DIM = 128
GQA_KV_HEADS = 2
GQA_HEAD_DIM = 64
IN_COLS = 5664
D_FF = 2816

N_CTX = BATCH * SEQ
N_DEC = DEC_BATCH * DEC_SEQ
N_TOK = N_CTX + N_DEC

U_COLS = 6144
C_QK, C_V, C_O = 0, 1024, 1536
A_Z = 2048
B_Q, B_K, B_V = 2560, 3072, 3584
D_Q = 4096
A_XBC = 4608
D_KV = 5376
A_DT = 5632
C_G = 5760

VMEM_LIMIT_BYTES = 56 * 1024 * 1024


def _in_col_permutation():
    idx = np.full((U_COLS,), IN_COLS, np.int32)
    a0, b0, c0, d0 = 0, 1296, 2832, 4896

    def put(dst, src, n):
        idx[dst:dst + n] = np.arange(src, src + n)

    put(A_Z, a0, 512)
    put(A_XBC, a0 + 512, 768)
    put(A_DT, a0 + 1280, 16)
    put(B_Q, b0, 512)
    put(B_K, b0 + 512, 512)
    put(B_V, b0 + 1024, 512)
    put(C_QK, c0, 1024)
    put(C_V, c0 + 1024, 512)
    put(C_O, c0 + 1536, 512)
    put(C_G, c0 + 2048, 16)
    put(D_Q, d0, 512)
    for kv in range(GQA_KV_HEADS):
        put(D_KV + kv * 128, d0 + 512 + kv * 64, 64)
        put(D_KV + kv * 128 + 64, d0 + 640 + kv * 64, 64)
    return idx


_IN_PERM = _in_col_permutation()


def _runs(idx):
    out, start = [], 0
    for p in range(1, len(idx) + 1):
        pad = idx[start] == IN_COLS
        if p == len(idx) or (idx[p] == IN_COLS) != pad or (not pad and idx[p] != idx[p - 1] + 1):
            out.append((None if pad else int(idx[start]), p - start))
            start = p
    return out


_IN_SEGMENTS = _runs(_IN_PERM)


def _cparams(sem):
    return pltpu.CompilerParams(dimension_semantics=sem, vmem_limit_bytes=VMEM_LIMIT_BYTES)


def _bdot(a, b):
    return jnp.dot(a.astype(BF16), b.astype(BF16), preferred_element_type=F32)


def _bdot_t(a, b):
    return lax.dot_general(a.astype(BF16), b.astype(BF16), (((1,), (1,)), ((), ())),
                           preferred_element_type=F32)


def _split3(v):
    hi = v.astype(BF16)
    r = v - hi.astype(F32)
    mid = r.astype(BF16)
    lo = (r - mid.astype(F32)).astype(BF16)
    return hi, mid, lo


def _xdot_l(e, v):
    hi, mid, lo = _split3(v)
    return (jnp.dot(e, hi, preferred_element_type=F32) + jnp.dot(e, mid, preferred_element_type=F32)
            + jnp.dot(e, lo, preferred_element_type=F32))


def _xdot_r(v, e):
    hi, mid, lo = _split3(v)
    return (jnp.dot(hi, e, preferred_element_type=F32) + jnp.dot(mid, e, preferred_element_type=F32)
            + jnp.dot(lo, e, preferred_element_type=F32))


def _sigmoid(x):
    return 1.0 / (1.0 + jnp.exp(-x))


def _silu(x):
    return x * _sigmoid(x)


def _softplus(x):
    return jnp.maximum(x, 0.0) + jnp.log1p(jnp.exp(-jnp.abs(x)))


def _log_sigmoid(x):
    return jnp.minimum(x, 0.0) - jnp.log1p(jnp.exp(-jnp.abs(x)))


def _rms(x):
    return x * lax.rsqrt(jnp.mean(x * x, axis=-1, keepdims=True) + EPS)


def _iota(shape, dim):
    return lax.broadcasted_iota(jnp.int32, shape, dim)


def _rope(x, cos, sin_signed):
    lane = _iota(x.shape, 1)
    first = (lane % 32) < 16
    xr = jnp.where(first, pltpu.roll(x, 112, 1), pltpu.roll(x, 16, 1))
    return x * cos + xr * sin_signed


def _mod_index(tile_rows):
    n_ctx_tiles = N_CTX // tile_rows
    per_sample = DEC_SEQ // tile_rows

    def f(i):
        return jnp.where(i < n_ctx_tiles, 0, 1 + (i - n_ctx_tiles) // per_sample)

    return f


def _win_prep_kernel(w_ref, o_ref):
    row = 0
    for s, n in _IN_SEGMENTS:
        if s is None:
            o_ref[row:row + n, :] = jnp.zeros((n, o_ref.shape[1]), BF16)
        else:
            o_ref[row:row + n, :] = w_ref[s:s + n, :].astype(BF16)
        row += n


def _prep_w_in(w_in_t):
    tc = 256
    return pl.pallas_call(
        _win_prep_kernel,
        grid=(DEPTH, D_MODEL // tc),
        in_specs=[pl.BlockSpec((None, IN_COLS, tc), lambda l, i: (l, 0, i))],
        out_specs=pl.BlockSpec((None, U_COLS, tc), lambda l, i: (l, 0, i)),
        out_shape=jax.ShapeDtypeStruct((DEPTH, U_COLS, D_MODEL), BF16),
        compiler_params=_cparams(("arbitrary", "arbitrary")),
        name="w_in_layout",
    )(w_in_t)


def _mod_kernel(c_ref, w_ref, b_ref, o_ref):
    c = c_ref[...]
    s = _silu(c)
    s_hi = s.astype(BF16)
    s_lo = (s - s_hi.astype(F32)).astype(BF16)
    w = w_ref[...]
    w_hi = w.astype(BF16)
    w_lo = (w - w_hi.astype(F32)).astype(BF16)
    acc = jnp.dot(s_hi, w_hi, preferred_element_type=F32)
    acc = acc + jnp.dot(s_hi, w_lo, preferred_element_type=F32)
    acc = acc + jnp.dot(s_lo, w_hi, preferred_element_type=F32)
    o_ref[...] = acc + b_ref[...]


def _modulation(cvec, w_ada, b_ada):
    tn = 1536
    n = 6 * D_MODEL
    return pl.pallas_call(
        _mod_kernel,
        grid=(DEPTH, n // tn),
        in_specs=[
            pl.BlockSpec((8, D_MODEL), lambda l, j: (0, 0)),
            pl.BlockSpec((None, D_MODEL, tn), lambda l, j: (l, 0, j)),
            pl.BlockSpec((None, 1, tn), lambda l, j: (l, 0, j)),
        ],
        out_specs=pl.BlockSpec((None, 8, tn), lambda l, j: (l, 0, j)),
        out_shape=jax.ShapeDtypeStruct((DEPTH, 8, n), F32),
        compiler_params=_cparams(("arbitrary", "arbitrary")),
        name="adaln_mod",
    )(cvec, w_ada, b_ada.reshape(DEPTH, 1, n))


ROW_SUB = 256


def _norm_mod_to(h_ref, x_ref, gain_ref, shift, scale, rows):
    def body(s, carry):
        r = pl.multiple_of(s * ROW_SUB, ROW_SUB)
        x = x_ref[pl.ds(r, ROW_SUB), :]
        h = (_rms(x) * gain_ref[...]) * (1.0 + scale) + shift
        h_ref[pl.ds(r, ROW_SUB), :] = h.astype(BF16)
        return carry

    lax.fori_loop(0, rows // ROW_SUB, body, 0)


def _inproj_kernel(x_ref, mod_ref, gain_ref, w_ref, u_ref, h_ref, *, tm):
    @pl.when(pl.program_id(1) == 0)
    def _():
        shift = mod_ref[:, 0:D_MODEL]
        scale = mod_ref[:, D_MODEL:2 * D_MODEL]
        _norm_mod_to(h_ref, x_ref, gain_ref, shift, scale, tm)

    u_ref[...] = lax.dot_general(h_ref[...], w_ref[...], (((1,), (1,)), ((), ())), preferred_element_type=F32)


def _inproj(x, mod, gain, w, l):
    tm, tn = 2048, 768
    midx = _mod_index(tm)
    return pl.pallas_call(
        functools.partial(_inproj_kernel, tm=tm),
        grid=(N_TOK // tm, U_COLS // tn),
        in_specs=[
            pl.BlockSpec((tm, D_MODEL), lambda i, j: (i, 0)),
            pl.BlockSpec((None, None, 1, 6 * D_MODEL), lambda i, j: (l, midx(i), 0, 0)),
            pl.BlockSpec((None, 1, D_MODEL), lambda i, j: (l, 0, 0)),
            pl.BlockSpec((None, tn, D_MODEL), lambda i, j: (l, j, 0)),
        ],
        out_specs=pl.BlockSpec((tm, tn), lambda i, j: (i, j)),
        out_shape=jax.ShapeDtypeStruct((N_TOK, U_COLS), F32),
        scratch_shapes=[pltpu.VMEM((tm, D_MODEL), BF16)],
        compiler_params=_cparams(("arbitrary", "arbitrary")),
        name="in_proj",
    )(x, mod, gain, w)


def _outproj_kernel(*refs, n_ctx_tiles):
    ctx_refs = refs[0:4]
    dec_refs = refs[4:8]
    x_ref, mod_ref, w_ref, o_ref = refs[8:12]
    i = pl.program_id(0)

    def compute(ys):
        acc = jnp.dot(ys[0][...], w_ref[0:512, :], preferred_element_type=F32)
        for g in range(1, 4):
            acc = acc + jnp.dot(ys[g][...], w_ref[g * 512:(g + 1) * 512, :], preferred_element_type=F32)
        gate = mod_ref[:, 2 * D_MODEL:3 * D_MODEL]
        o_ref[...] = x_ref[...] + gate * acc

    @pl.when(i < n_ctx_tiles)
    def _():
        compute(ctx_refs)

    @pl.when(i >= n_ctx_tiles)
    def _():
        compute(dec_refs)


def _outproj(ys_ctx, ys_dec, x, mod, w, l):
    tm = 512
    nct = N_CTX // tm
    midx = _mod_index(tm)
    ctx_spec = pl.BlockSpec((tm, GROUP_W), lambda i: (jnp.minimum(i, nct - 1), 0))
    dec_spec = pl.BlockSpec((tm, GROUP_W), lambda i: (jnp.maximum(i - nct, 0), 0))
    return pl.pallas_call(
        functools.partial(_outproj_kernel, n_ctx_tiles=nct),
        grid=(N_TOK // tm,),
        in_specs=[ctx_spec] * 4 + [dec_spec] * 4 + [
            pl.BlockSpec((tm, D_MODEL), lambda i: (i, 0)),
            pl.BlockSpec((None, None, 1, 6 * D_MODEL), lambda i: (l, midx(i), 0, 0)),
            pl.BlockSpec((None, D_MIX, D_MODEL), lambda i: (l, 0, 0)),
        ],
        out_specs=pl.BlockSpec((tm, D_MODEL), lambda i: (i, 0)),
        out_shape=jax.ShapeDtypeStruct((N_TOK, D_MODEL), F32),
        compiler_params=_cparams(("arbitrary",)),
        name="out_proj",
    )(*ys_ctx, *ys_dec, x, mod, w)


def _ffn_kernel(x_ref, mod_ref, gain_ref, wg_ref, wu_ref, wo_ref, o_ref, *, tm):
    shift = mod_ref[:, 3 * D_MODEL:4 * D_MODEL]
    scale = mod_ref[:, 4 * D_MODEL:5 * D_MODEL]
    g2 = mod_ref[:, 5 * D_MODEL:6 * D_MODEL]

    def body(s, carry):
        r = pl.multiple_of(s * ROW_SUB, ROW_SUB)
        x = x_ref[pl.ds(r, ROW_SUB), :]
        h = ((_rms(x) * gain_ref[...]) * (1.0 + scale) + shift).astype(BF16)
        gate = jnp.dot(h, wg_ref[...], preferred_element_type=F32)
        up = jnp.dot(h, wu_ref[...], preferred_element_type=F32)
        act = (_silu(gate) * up).astype(BF16)
        o_ref[pl.ds(r, ROW_SUB), :] = x + g2 * jnp.dot(act, wo_ref[...], preferred_element_type=F32)
        return carry

    lax.fori_loop(0, tm // ROW_SUB, body, 0)


def _ffn(x, mod, gain, w_in, w_out, l):
    tm = 1024
    midx = _mod_index(tm)
    resident = dict(pipeline_mode=pl.Buffered(1))
    return pl.pallas_call(
        functools.partial(_ffn_kernel, tm=tm),
        grid=(N_TOK // tm,),
        in_specs=[
            pl.BlockSpec((tm, D_MODEL), lambda i: (i, 0)),
            pl.BlockSpec((None, None, 1, 6 * D_MODEL), lambda i: (l, midx(i), 0, 0)),
            pl.BlockSpec((None, 1, D_MODEL), lambda i: (l, 0, 0)),
            pl.BlockSpec((None, D_MODEL, D_FF), lambda i: (l, 0, 0), **resident),
            pl.BlockSpec((None, D_MODEL, D_FF), lambda i: (l, 0, 1), **resident),
            pl.BlockSpec((None, D_FF, D_MODEL), lambda i: (l, 0, 0), **resident),
        ],
        out_specs=pl.BlockSpec((tm, D_MODEL), lambda i: (i, 0)),
        out_shape=jax.ShapeDtypeStruct((N_TOK, D_MODEL), F32),
        compiler_params=_cparams(("arbitrary",)),
        name="ffn",
    )(x, mod, gain, w_in, w_in, w_out)


def _final_norm_kernel(x_ref, g_ref, o_ref):
    o_ref[...] = _rms(x_ref[...]) * g_ref[...]


def _final_norm(x, gain, row0, rows):
    tm = 512
    off = row0 // tm
    return pl.pallas_call(
        _final_norm_kernel,
        grid=(rows // tm,),
        in_specs=[pl.BlockSpec((tm, D_MODEL), lambda i: (i + off, 0)),
                  pl.BlockSpec((1, D_MODEL), lambda i: (0, 0))],
        out_specs=pl.BlockSpec((tm, D_MODEL), lambda i: (i, 0)),
        out_shape=jax.ShapeDtypeStruct((rows, D_MODEL), F32),
        compiler_params=_cparams(("arbitrary",)),
        name="final_norm",
    )(x, gain)


def _causal_masks():
    i = _iota((CHUNK, CHUNK), 0)
    j = _iota((CHUNK, CHUNK), 1)
    return j <= i, j >= i


def _conv_silu_chunk(pad_ref, w_ref, b_ref, r):
    win = pad_ref[pl.ds(r, CHUNK + 16), :]
    acc = b_ref[...] + w_ref[0:1, :] * win[6:6 + CHUNK, :]
    for tap in range(1, D_CONV):
        acc = acc + w_ref[tap:tap + 1, :] * win[6 + tap:6 + tap + CHUNK, :]
    return _silu(acc)


def _fill_padded(pad_ref, src_ref, seq, width, n_seq=1):
    zeros = jnp.zeros((8, width), F32)
    for i in range(n_seq):
        base = i * (seq + 16)
        pad_ref[base:base + 8, :] = zeros
        pad_ref[base + seq + 8:base + seq + 16, :] = zeros

        def body(c, carry, base=base, src0=i * seq):
            r = pl.multiple_of(c * CHUNK, CHUNK)
            pad_ref[pl.ds(base + 8 + r, CHUNK), :] = src_ref[pl.ds(src0 + r, CHUNK), :]
            return carry

        lax.fori_loop(0, seq // CHUNK, body, 0)


SSD_CPI = 2


def _ssd_kernel(*refs, seq, has_state, n_carry=0):
    (z_ref, xbc_ref, dt_ref, cw_ref, cb_ref, alog_ref, dtb_ref, dsk_ref, ng_ref) = refs[0:9]
    refs = refs[9 + n_carry:]
    if has_state:
        h0_ref, y_ref = refs[0:2]
        hfin_ref = None
    else:
        h0_ref = None
        y_ref, hfin_ref = refs[0:2]
    pad_ref, xc_ref, yf_ref, yb_ref, st_ref = refs[2:7]
    nc = seq // CHUNK
    hpg = SSM_HEADS // 2
    gw = hpg * SSM_HEAD_DIM

    _fill_padded(pad_ref, xbc_ref, seq, SSM_CONV_CH)

    def conv_body(c, carry):
        r = pl.multiple_of(c * CHUNK, CHUNK)
        xc_ref[pl.ds(r, CHUNK), :] = _conv_silu_chunk(pad_ref, cw_ref, cb_ref, r)
        return carry

    lax.fori_loop(0, nc, conv_body, 0)

    zero_blk = jnp.zeros((SSM_STATE, SSM_HEAD_DIM), F32)
    for d in range(2):
        for g in range(2):
            if has_state:
                rows = []
                for h4 in range(hpg):
                    blk = h0_ref[d, g * hpg + h4].T
                    rows.append(jnp.concatenate([blk if k == h4 else zero_blk for k in range(hpg)], axis=1))
                st_ref[d, g] = jnp.concatenate(rows, axis=0)
            else:
                st_ref[d, g] = jnp.zeros((gw, gw), F32)

    mask_f, mask_b = _causal_masks()
    masks = (mask_f, mask_b)
    tmats = (mask_f.astype(BF16), mask_b.astype(BF16))
    a_neg = -jnp.exp(alog_ref[...])
    lane_head = _iota((CHUNK, gw), 1) // SSM_HEAD_DIM
    blk_diag = (_iota((gw, gw), 0) // SSM_STATE) == (_iota((gw, gw), 1) // SSM_HEAD_DIM)

    def step(items):
        xcs = [xc_ref[pl.ds(r, CHUNK), :] for d, r in items]
        dts = [_softplus(dt_ref[pl.ds(r, CHUNK), :] + dtb_ref[...]) for d, r in items]
        cum_c = [_xdot_l(tmats[d], dts[n] * a_neg) for n, (d, r) in enumerate(items)]
        gms = [[_bdot_t(xcs[n][:, 640 + g * 64:640 + (g + 1) * 64], xcs[n][:, 512 + g * 64:512 + (g + 1) * 64])
                for g in range(2)] for n in range(len(items))]
        lhs_d, rhs_d, lhs_o, lhs_s, xgs, elcols = [], [], [], [], [], []
        for n, (d, r) in enumerate(items):
            cum_r = cum_c[n].T
            dt_r = dts[n].T
            last = cum_c[n][CHUNK - 1:CHUNK, :] if d == 0 else cum_c[n][0:1, :]
            e_last = jnp.exp(last)
            for g in range(2):
                xg = xcs[n][:, g * gw:(g + 1) * gw]
                bt = xcs[n][:, 512 + g * 64:512 + (g + 1) * 64].T
                cm = xcs[n][:, 640 + g * 64:640 + (g + 1) * 64]
                s_l, cec_l, btw_l, xm_l, el_l = [], [], [], [], []
                for h4 in range(hpg):
                    col = d * SSM_HEADS + g * hpg + h4
                    cc = jnp.broadcast_to(cum_c[n][:, col:col + 1], (CHUNK, CHUNK))
                    cr = cum_r[col:col + 1, :]
                    dtr = dt_r[col:col + 1, :]
                    decay = jnp.exp(jnp.where(masks[d], cc - cr, -jnp.inf))
                    s_l.append((decay * gms[n][g] * dtr).astype(BF16))
                    cec_l.append((cm * jnp.exp(cc[:, 0:SSM_STATE])).astype(BF16))
                    btw_l.append((bt * (jnp.exp(last[:, col:col + 1] - cr) * dtr)).astype(BF16))
                    xm_l.append(jnp.where(lane_head == h4, xg, 0.0).astype(BF16))
                    el_l.append(jnp.broadcast_to(e_last[:, col:col + 1], (SSM_STATE, gw)))
                lhs_d.append(jnp.concatenate(s_l, axis=1))
                rhs_d.append(jnp.concatenate(xm_l, axis=0))
                lhs_o.append(jnp.concatenate(cec_l, axis=1))
                lhs_s.append(jnp.concatenate(btw_l, axis=0))
                xgs.append(xg.astype(BF16))
                elcols.append(jnp.concatenate(el_l, axis=0))
        yds = [jnp.dot(lhs_d[k], rhs_d[k], preferred_element_type=F32) for k in range(len(lhs_d))]
        css = [jnp.dot(lhs_s[k], xgs[k], preferred_element_type=F32) for k in range(len(lhs_s))]
        outs = []
        for n, (d, r) in enumerate(items):
            ys = []
            for g in range(2):
                k = 2 * n + g
                st = st_ref[d, g]
                ys.append(yds[k] + jnp.dot(lhs_o[k], st.astype(BF16), preferred_element_type=F32))
                st_ref[d, g] = elcols[k] * st + jnp.where(blk_diag, css[k], 0.0)
            outs.append(jnp.concatenate(ys, axis=1))
        return outs

    def scan_body(s, carry):
        cf = SSD_CPI * s
        cb = nc - 1 - cf
        rows_f = [pl.multiple_of((cf + t) * CHUNK, CHUNK) for t in range(SSD_CPI)]
        rows_b = [pl.multiple_of((cb - t) * CHUNK, CHUNK) for t in range(SSD_CPI)]
        ys = step([(0, r) for r in rows_f] + [(1, r) for r in rows_b])
        for t in range(SSD_CPI):
            yf_ref[pl.ds(rows_f[t], CHUNK), :] = ys[t]
            yb_ref[pl.ds(rows_b[t], CHUNK), :] = ys[SSD_CPI + t]
        return carry

    lax.fori_loop(0, nc // SSD_CPI, scan_body, 0)

    def out_body(c, carry):
        r = pl.multiple_of(c * CHUNK, CHUNK)
        x = xc_ref[pl.ds(r, CHUNK), 0:GROUP_W]
        y = yf_ref[pl.ds(r, CHUNK), :] + yb_ref[pl.ds(r, CHUNK), :] + dsk_ref[...] * x
        y = y * _silu(z_ref[pl.ds(r, CHUNK), :])
        y_ref[pl.ds(r, CHUNK), :] = (_rms(y) * ng_ref[...]).astype(BF16)
        return carry

    lax.fori_loop(0, nc, out_body, 0)

    if hfin_ref is not None:
        for d in range(2):
            for g in range(2):
                st = st_ref[d, g]
                for h4 in range(hpg):
                    lo = h4 * SSM_STATE
                    hfin_ref[d, g * hpg + h4] = st[lo:lo + SSM_STATE, lo:lo + SSM_HEAD_DIM].T


def _carried(carry):
    if carry is None:
        return [], []
    return [pl.BlockSpec(memory_space=pl.ANY)] * len(carry), list(carry)


def _mixer_ssd(u, row0, nb, seq, conv_w, conv_b, a_log, dt_bias, d_skip, norm_g, h0, layer=0, carry=None):
    has_state = h0 is not None
    roff = row0 // seq
    vec = lambda n: pl.BlockSpec((1, n), lambda b: (0, 0))
    in_specs = [
        pl.BlockSpec((seq, GROUP_W), lambda b: (b + roff, A_Z // GROUP_W)),
        pl.BlockSpec((seq, SSM_CONV_CH), lambda b: (b + roff, A_XBC // SSM_CONV_CH)),
        pl.BlockSpec((seq, 128), lambda b: (b + roff, A_DT // 128)),
        pl.BlockSpec((8, SSM_CONV_CH), lambda b: (0, 0)),
        vec(SSM_CONV_CH), vec(128), vec(128), vec(GROUP_W), vec(GROUP_W),
    ]
    args = [u, u, u, conv_w, conv_b, a_log, dt_bias, d_skip, norm_g]
    st_block = (None, 2, SSM_HEADS, SSM_HEAD_DIM, SSM_STATE)
    y_spec = pl.BlockSpec((seq, GROUP_W), lambda b: (b, 0))
    y_shape = jax.ShapeDtypeStruct((nb * seq, GROUP_W), BF16)
    aliases = {}
    n_carry = 0
    if has_state:
        in_specs.append(pl.BlockSpec(st_block, lambda b: (b, 0, 0, 0, 0)))
        args.append(h0)
        out_specs, out_shape = y_spec, y_shape
    else:
        c_specs, c_args = _carried(carry)
        n_carry = len(c_args)
        aliases = {len(args) + k: 1 + k for k in range(n_carry)}
        in_specs += c_specs
        args += c_args
        out_specs = [y_spec, pl.BlockSpec((None,) + st_block, lambda b: (b, layer, 0, 0, 0, 0))]
        out_shape = [y_shape, jax.ShapeDtypeStruct((nb, DEPTH, 2, SSM_HEADS, SSM_HEAD_DIM, SSM_STATE), F32)]
    return pl.pallas_call(
        functools.partial(_ssd_kernel, seq=seq, has_state=has_state, n_carry=n_carry),
        grid=(nb,),
        in_specs=in_specs,
        out_specs=out_specs,
        out_shape=out_shape,
        input_output_aliases=aliases,
        scratch_shapes=[
            pltpu.VMEM((seq + 16, SSM_CONV_CH), F32),
            pltpu.VMEM((seq, SSM_CONV_CH), F32),
            pltpu.VMEM((seq, GROUP_W), F32),
            pltpu.VMEM((seq, GROUP_W), F32),
            pltpu.VMEM((2, 2, GROUP_W // 2, GROUP_W // 2), F32),
        ],
        compiler_params=_cparams(("arbitrary",)),
        name="mixer_ssd_dec" if has_state else "mixer_ssd_ctx",
    )(*args)


MLSTM_CPI = 4


def _mlstm_kernel(*refs, seq, n_seq, has_state, n_carry=0):
    (q_ref, k_ref, v_ref, o_ref, g_ref, cwq_ref, cwk_ref, cbq_ref, cbk_ref, gb_ref, ng_ref) = refs[0:11]
    refs = refs[11 + n_carry:]
    if has_state:
        c0_ref, n0_ref, m0_ref, y_ref = refs[0:4]
        outs = None
    else:
        y_ref = refs[0]
        outs = refs[1:4]
    (qpad_ref, kpad_ref, qc_ref, hf_ref, hb_ref, a_ref, cl_ref, rs_ref, rm_ref, cu_ref, rows_ref,
     cst_ref, nst_ref, mst_ref) = refs[4:]
    nc = seq // CHUNK
    head = pl.program_id(1)

    _fill_padded(qpad_ref, q_ref, seq, MLSTM_HEAD_DIM, n_seq)
    _fill_padded(kpad_ref, k_ref, seq, MLSTM_HEAD_DIM, n_seq)

    mask_f, mask_b = _causal_masks()
    row_id = _iota((CHUNK, CHUNK), 0)
    full = (CHUNK, CHUNK)

    sel = jnp.concatenate(
        [(row_id == col).astype(BF16) for col in (head, 4 + head, 8 + head, 12 + head)], axis=1)
    tmats = (mask_f.astype(BF16), mask_b.astype(BF16))
    masks = (mask_f, mask_b)
    forget_col = (_iota((CHUNK, CHUNK), 1) % 8) >= MLSTM_HEADS

    def local_body(s, carry):
        chunks = tuple(MLSTM_CPI * s + t for t in range(MLSTM_CPI))
        qs, ks, vs, qks, gsel = [], [], [], [], []
        for c in chunks:
            r = c * CHUNK if isinstance(c, int) else pl.multiple_of(c * CHUNK, CHUNK)
            rp = r if n_seq == 1 else c * CHUNK + 16 * (c // nc)
            q = _conv_silu_chunk(qpad_ref, cwq_ref, cbq_ref, rp) * (MLSTM_HEAD_DIM ** -0.5)
            k = _conv_silu_chunk(kpad_ref, cwk_ref, cbk_ref, rp)
            qc_ref[pl.ds(r, CHUNK), :] = q
            qs.append(q)
            ks.append(k)
            vs.append(v_ref[pl.ds(r, CHUNK), :])
            g = g_ref[pl.ds(r, CHUNK), :] + gb_ref[...]
            g = jnp.where(forget_col, _log_sigmoid(g), g)
            gsel.append([jnp.dot(p, sel, preferred_element_type=F32) for p in _split3(g)])
        for i in range(MLSTM_CPI):
            qks.append(_bdot_t(qs[i], ks[i]))
        items = [(i, d) for i in range(MLSTM_CPI) for d in range(2)]
        li_c = [gsel[i][0][:, 256 * d:256 * d + 128] + gsel[i][1][:, 256 * d:256 * d + 128]
                + gsel[i][2][:, 256 * d:256 * d + 128] for i, d in items]
        cum_c = []
        for i, d in items:
            lf = [gsel[i][p][:, 256 * d + 128:256 * d + 256].astype(BF16) for p in range(3)]
            cum_c.append(jnp.dot(tmats[d], lf[0], preferred_element_type=F32)
                         + jnp.dot(tmats[d], lf[1], preferred_element_type=F32)
                         + jnp.dot(tmats[d], lf[2], preferred_element_type=F32))
        cum_r = [x.T for x in cum_c]
        li_r = [x.T for x in li_c]
        last = [cum_c[n][CHUNK - 1:CHUNK, :] if d == 0 else cum_c[n][0:1, :]
                for n, (i, d) in enumerate(items)]
        dmat = [jnp.where(masks[d], cum_c[n] - cum_r[n] + li_r[n], -jnp.inf) for n, (i, d) in enumerate(items)]
        rowmax = [jnp.max(x, axis=1, keepdims=True) for x in dmat]
        sp = [qks[i] * jnp.exp(dmat[n] - rowmax[n]) for n, (i, d) in enumerate(items)]
        m_loc = [jnp.max(last[n] - cum_r[n][0:1, :] + li_r[n][0:1, :], axis=1, keepdims=True)
                 for n in range(len(items))]
        kw = [ks[i] * jnp.exp(last[n] - cum_c[n] + li_c[n] - m_loc[n]) for n, (i, d) in enumerate(items)]
        kwt = [x.T for x in kw]
        a_loc = [_bdot(sp[n], vs[i]) for n, (i, d) in enumerate(items)]
        c_loc = [_bdot(kwt[n], vs[i]) for n, (i, d) in enumerate(items)]
        for n, (i, d) in enumerate(items):
            c = chunks[i]
            a_ref[c, d] = a_loc[n]
            cl_ref[c, d] = c_loc[n]
            rs_ref[c, d] = jnp.broadcast_to(jnp.sum(sp[n], axis=1, keepdims=True), full)
            rm_ref[c, d] = jnp.broadcast_to(rowmax[n], full)
            cu_ref[c, d] = cum_c[n]
            rows_ref[c, d, 0:1, :] = jnp.sum(kw[n], axis=0, keepdims=True)
            rows_ref[c, d, 1:2, :] = jnp.broadcast_to(m_loc[n], (1, CHUNK))
            rows_ref[c, d, 2:3, :] = last[n]
        return carry

    n_local = n_seq * nc // MLSTM_CPI
    if n_local == 1:
        local_body(0, 0)
    else:
        lax.fori_loop(0, n_local, local_body, 0)

    if has_state:
        cst_ref[...] = c0_ref[...]
        nst_ref[...] = n0_ref[...]
        mst_ref[...] = m0_ref[...]
    else:
        cst_ref[...] = jnp.zeros(cst_ref.shape, F32)
        nst_ref[...] = jnp.zeros(nst_ref.shape, F32)
        mst_ref[...] = jnp.zeros(mst_ref.shape, F32)

    def state_step(i, d, lc):
        c = i * nc + lc
        r = pl.multiple_of(c * CHUNK, CHUNK)
        q = qc_ref[pl.ds(r, CHUNK), :]
        c_p = cst_ref[i, d]
        n_p = nst_ref[i, d]
        m_p = mst_ref[i, d]
        n_loc = rows_ref[c, d, 0:1, :]
        m_loc = rows_ref[c, d, 1:2, :]
        last = rows_ref[c, d, 2:3, :]
        rowmax = rm_ref[c, d]
        inter = cu_ref[c, d] + m_p
        m_t = jnp.maximum(inter, rowmax)
        f_intra = jnp.exp(rowmax - m_t)
        w_inter = jnp.exp(inter - m_t)
        num = a_ref[c, d] * f_intra + w_inter * _bdot(q, c_p)
        den = rs_ref[c, d] * f_intra + w_inter * jnp.sum(q * n_p, axis=1, keepdims=True)
        hh = num / jnp.maximum(jnp.abs(den), jnp.exp(-m_t))
        m_new = jnp.maximum(last + m_p, m_loc)
        s_p = jnp.exp(last + m_p - m_new)
        s_l = jnp.exp(m_loc - m_new)
        cst_ref[i, d] = s_p[:, 0:1] * c_p + s_l[:, 0:1] * cl_ref[c, d]
        nst_ref[i, d] = s_p * n_p + s_l * n_loc
        mst_ref[i, d] = m_new
        return r, hh

    def state_body(s, carry):
        for i in range(n_seq):
            r, hh = state_step(i, 0, s)
            hf_ref[pl.ds(r, CHUNK), :] = hh
            r, hh = state_step(i, 1, nc - 1 - s)
            hb_ref[pl.ds(r, CHUNK), :] = hh
        return carry

    lax.fori_loop(0, nc, state_body, 0, unroll=2)

    def out_body(c, carry):
        r = pl.multiple_of(c * CHUNK, CHUNK)
        hsum = hf_ref[pl.ds(r, CHUNK), :] + hb_ref[pl.ds(r, CHUNK), :]
        y = _sigmoid(o_ref[pl.ds(r, CHUNK), :]) * (_rms(hsum) * ng_ref[...])
        y_ref[pl.ds(r, CHUNK), :] = y.astype(BF16)
        return carry

    lax.fori_loop(0, n_seq * nc, out_body, 0, unroll=2)

    if outs is not None:
        outs[0][...] = cst_ref[...]
        outs[1][...] = nst_ref[...]
        outs[2][...] = mst_ref[...]


def _mixer_mlstm(u, row0, nb, seq, conv_w, conv_b, gate_b, norm_g, state, layer=0, carry=None):
    has_state = state is not None
    hd = MLSTM_HEAD_DIM
    nc = seq // CHUNK
    n_seq = max(1, MLSTM_CPI // nc)
    rows = n_seq * seq
    gnc = n_seq * nc
    roff = row0 // rows
    col = lambda base: (lambda b, h: (b + roff, base // hd + h))
    in_specs = [
        pl.BlockSpec((rows, hd), col(C_QK)),
        pl.BlockSpec((rows, hd), col(C_QK + GROUP_W)),
        pl.BlockSpec((rows, hd), col(C_V)),
        pl.BlockSpec((rows, hd), col(C_O)),
        pl.BlockSpec((rows, 128), lambda b, h: (b + roff, C_G // 128)),
        pl.BlockSpec((8, hd), lambda b, h: (0, h)),
        pl.BlockSpec((8, hd), lambda b, h: (0, MLSTM_HEADS + h)),
        pl.BlockSpec((1, hd), lambda b, h: (0, h)),
        pl.BlockSpec((1, hd), lambda b, h: (0, MLSTM_HEADS + h)),
        pl.BlockSpec((1, 128), lambda b, h: (0, 0)),
        pl.BlockSpec((1, hd), lambda b, h: (0, h)),
    ]
    args = [u, u, u, u, u, conv_w, conv_w, conv_b, conv_b, gate_b, norm_g]
    c_spec = pl.BlockSpec((n_seq, 2, None, hd, hd), lambda b, h: (b, 0, h, 0, 0))
    n_spec = pl.BlockSpec((n_seq, 2, None, 1, hd), lambda b, h: (b, 0, h, 0, 0))
    y_spec = pl.BlockSpec((rows, hd), lambda b, h: (b, h))
    y_shape = jax.ShapeDtypeStruct((nb * seq, GROUP_W), BF16)
    aliases = {}
    n_carry = 0
    if has_state:
        in_specs += [c_spec, n_spec, n_spec]
        args += list(state)
        out_specs, out_shape = y_spec, y_shape
    else:
        c_specs, c_args = _carried(carry)
        n_carry = len(c_args)
        aliases = {len(args) + k: 1 + k for k in range(n_carry)}
        in_specs += c_specs
        args += c_args
        c_all = pl.BlockSpec((n_seq, None, 2, None, hd, hd), lambda b, h: (b, layer, 0, h, 0, 0))
        n_all = pl.BlockSpec((n_seq, None, 2, None, 1, hd), lambda b, h: (b, layer, 0, h, 0, 0))
        out_specs = [y_spec, c_all, n_all, n_all]
        out_shape = [y_shape,
                     jax.ShapeDtypeStruct((nb, DEPTH, 2, MLSTM_HEADS, hd, hd), F32),
                     jax.ShapeDtypeStruct((nb, DEPTH, 2, MLSTM_HEADS, 1, hd), F32),
                     jax.ShapeDtypeStruct((nb, DEPTH, 2, MLSTM_HEADS, 1, hd), F32)]
    return pl.pallas_call(
        functools.partial(_mlstm_kernel, seq=seq, n_seq=n_seq, has_state=has_state, n_carry=n_carry),
        grid=(nb // n_seq, MLSTM_HEADS),
        in_specs=in_specs,
        out_specs=out_specs,
        out_shape=out_shape,
        input_output_aliases=aliases,
        scratch_shapes=[
            pltpu.VMEM((rows + 16 * n_seq, hd), F32), pltpu.VMEM((rows + 16 * n_seq, hd), F32),
            pltpu.VMEM((rows, hd), F32), pltpu.VMEM((rows, hd), F32), pltpu.VMEM((rows, hd), F32),
            pltpu.VMEM((gnc, 2, hd, hd), F32), pltpu.VMEM((gnc, 2, hd, hd), F32),
            pltpu.VMEM((gnc, 2, hd, hd), F32), pltpu.VMEM((gnc, 2, hd, hd), F32), pltpu.VMEM((gnc, 2, hd, hd), F32),
            pltpu.VMEM((gnc, 2, 8, hd), F32),
            pltpu.VMEM((n_seq, 2, hd, hd), F32), pltpu.VMEM((n_seq, 2, 1, hd), F32), pltpu.VMEM((n_seq, 2, 1, hd), F32),
        ],
        compiler_params=_cparams(("arbitrary", "arbitrary")),
        name="mixer_mlstm_dec" if has_state else "mixer_mlstm_ctx",
    )(*args)


KV_SUB = 256


ATT_SUB = 256
DIFF_QB = 1024
GQA_QB = 512


def _softmax_parts(s):
    e = jnp.exp(s - jnp.max(s, axis=1, keepdims=True))
    return e.astype(BF16), jnp.sum(e, axis=1, keepdims=True)


def _diff_lambda(lam_ref, lam_init):
    lp = lam_ref[...]
    return (jnp.exp(jnp.sum(lp[0:1, :] * lp[1:2, :], axis=1, keepdims=True))
            - jnp.exp(jnp.sum(lp[2:3, :] * lp[3:4, :], axis=1, keepdims=True)) + lam_init)


def _exp_scores_t(s_t):
    e = jnp.exp(s_t - jnp.max(s_t, axis=0, keepdims=True))
    return e.astype(BF16), jnp.sum(e, axis=0, keepdims=True)


def _pv_t(vt, el):
    e, l = el
    return jnp.dot(vt, e, preferred_element_type=F32) / l


def _diff_dec_kernel(q_ref, k_ref, v_ref, lam_ref, ck_ref, cv_ref, cosq_ref, sinq_ref, cosk_ref, sink_ref,
                     o_ref, k1_ref, k2_ref, vt_ref, *, seq, lam_init):
    past = PAST_LEN

    @pl.when(pl.program_id(2) == 0)
    def _():
        ck = ck_ref[...]
        k1_ref[0:past, :] = ck[:, 0:64].astype(BF16)
        k2_ref[0:past, :] = ck[:, 64:128].astype(BF16)
        vt_ref[:, 0:past] = cv_ref[...].T.astype(BF16)
        for s in range(seq // KV_SUB):
            r = s * KV_SUB
            kk = _rope(k_ref[r:r + KV_SUB, :], cosk_ref[r:r + KV_SUB, :], sink_ref[r:r + KV_SUB, :])
            k1_ref[past + r:past + r + KV_SUB, :] = kk[:, 0:64].astype(BF16)
            k2_ref[past + r:past + r + KV_SUB, :] = kk[:, 64:128].astype(BF16)
            vt_ref[:, past + r:past + r + KV_SUB] = v_ref[r:r + KV_SUB, :].T.astype(BF16)

    lam = _diff_lambda(lam_ref, lam_init)
    q = _rope(q_ref[...], cosq_ref[...], sinq_ref[...]) * (64 ** -0.5)
    n_sub = q.shape[0] // ATT_SUB
    k_maps = (k1_ref[...], k2_ref[...])
    q_t = [q[j * ATT_SUB:(j + 1) * ATT_SUB, m * 64:(m + 1) * 64].T.astype(BF16)
           for j in range(n_sub) for m in range(2)]
    scores = [jnp.dot(k_maps[n % 2], q_t[n], preferred_element_type=F32) for n in range(2 * n_sub)]
    probs = [_exp_scores_t(s) for s in scores]
    vt = vt_ref[...]
    pv = [_pv_t(vt, el) for el in probs]
    for j in range(n_sub):
        o = (pv[2 * j] - lam * pv[2 * j + 1]).T
        o_ref[j * ATT_SUB:(j + 1) * ATT_SUB, :] = (_rms(o) * (1.0 - lam_init)).astype(BF16)


def _diff_ctx_kernel(*refs, lam_init, n_carry=0):
    q_ref, k_ref, v_ref, lam_ref = refs[0:4]
    o_ref, ko_ref, vo_ref = refs[4 + n_carry:]
    ko_ref[...] = k_ref[...]
    vo_ref[...] = v_ref[...]
    lam = _diff_lambda(lam_ref, lam_init)
    q = q_ref[...] * (64 ** -0.5)
    k = k_ref[...].astype(BF16)
    v = v_ref[...].astype(BF16)
    cols = [h * 128 + m * 64 for h in range(DIFF_HEADS) for m in range(2)]
    scores = [_bdot_t(q[:, c0:c0 + 64], k[:, c0:c0 + 64]) for c0 in cols]
    parts = [_softmax_parts(s) for s in scores]
    pv = [jnp.dot(parts[n][0], v[:, (n // 2) * 128:(n // 2 + 1) * 128], preferred_element_type=F32)
          for n in range(2 * DIFF_HEADS)]
    outs = []
    for h in range(DIFF_HEADS):
        o = pv[2 * h] / parts[2 * h][1] - lam * (pv[2 * h + 1] / parts[2 * h + 1][1])
        outs.append(_rms(o) * (1.0 - lam_init))
    o_ref[...] = jnp.concatenate(outs, axis=1).astype(BF16)


def _mixer_diff(u, row0, nb, seq, lam_rows, lam_init, ctx, layer=0, carry=None):
    roff = row0 // seq
    o_shape = jax.ShapeDtypeStruct((nb * seq, GROUP_W), BF16)
    lam_spec3 = pl.BlockSpec((8, 128), lambda b, h, i: (0, 0))
    if ctx is None:
        col = lambda base: pl.BlockSpec((seq, GROUP_W), lambda b: (b + roff, base // GROUP_W))
        c_specs, c_args = _carried(carry)
        slab = pl.BlockSpec((None, None, seq, GROUP_W), lambda b: (b, layer, 0, 0))
        slab_shape = jax.ShapeDtypeStruct((nb, DEPTH, seq, GROUP_W), F32)
        return pl.pallas_call(
            functools.partial(_diff_ctx_kernel, lam_init=lam_init, n_carry=len(c_args)),
            grid=(nb,),
            in_specs=[col(B_Q), col(B_K), col(B_V), pl.BlockSpec((8, 128), lambda b: (0, 0))] + c_specs,
            out_specs=[pl.BlockSpec((seq, GROUP_W), lambda b: (b, 0)), slab, slab],
            out_shape=[o_shape, slab_shape, slab_shape],
            input_output_aliases={4 + k: 1 + k for k in range(len(c_args))},
            compiler_params=_cparams(("arbitrary",)),
            name="mixer_diff_ctx",
        )(u, u, u, lam_rows, *c_args)
    ck, cv, cos, sin = ctx
    qb = min(DIFF_QB, seq)
    qoff = row0 // qb
    nq = seq // qb
    lk = seq + PAST_LEN
    in_specs = [
        pl.BlockSpec((qb, 128), lambda b, h, i: (qoff + b * nq + i, B_Q // 128 + h)),
        pl.BlockSpec((seq, 128), lambda b, h, i: (b + roff, B_K // 128 + h)),
        pl.BlockSpec((seq, 128), lambda b, h, i: (b + roff, B_V // 128 + h)),
        lam_spec3,
        pl.BlockSpec((None, PAST_LEN, 128), lambda b, h, i: (b, 0, h)),
        pl.BlockSpec((None, PAST_LEN, 128), lambda b, h, i: (b, 0, h)),
        pl.BlockSpec((qb, 128), lambda b, h, i: (i, 0)),
        pl.BlockSpec((qb, 128), lambda b, h, i: (i, 0)),
        pl.BlockSpec((seq, 128), lambda b, h, i: (0, 0)),
        pl.BlockSpec((seq, 128), lambda b, h, i: (0, 0)),
    ]
    return pl.pallas_call(
        functools.partial(_diff_dec_kernel, seq=seq, lam_init=lam_init),
        grid=(nb, DIFF_HEADS, nq),
        in_specs=in_specs,
        out_specs=pl.BlockSpec((qb, 128), lambda b, h, i: (b * nq + i, h)),
        out_shape=o_shape,
        scratch_shapes=[pltpu.VMEM((lk, 64), BF16), pltpu.VMEM((lk, 64), BF16), pltpu.VMEM((128, lk), BF16)],
        compiler_params=_cparams(("arbitrary", "arbitrary", "arbitrary")),
        name="mixer_diff_dec",
    )(u, u, u, lam_rows, ck, cv, cos, sin, cos, sin)


def _seg_rms_pair(x, gain):
    low = _iota(x.shape, 1) < 64
    sq = x * x
    ms_lo = jnp.sum(jnp.where(low, sq, 0.0), axis=1, keepdims=True) * (1.0 / 64)
    ms_hi = jnp.sum(jnp.where(low, 0.0, sq), axis=1, keepdims=True) * (1.0 / 64)
    return x * lax.rsqrt(jnp.where(low, ms_lo, ms_hi) + EPS) * gain


def _gqa_attend(q_heads, kf, vf):
    rows = q_heads[0].shape[0]
    s = _bdot_t(jnp.concatenate(q_heads, axis=0), kf)
    e, l = _softmax_parts(s)
    o = jnp.dot(e, vf, preferred_element_type=F32) / l
    return [o[g * rows:(g + 1) * rows, :] for g in range(len(q_heads))]


def _gqa_dec_kernel(q_ref, kv_ref, qg_ref, kg_ref, ck_ref, cv_ref, cosq_ref, sinq_ref, cosk_ref, sink_ref,
                    o_ref, kf_ref, vt_ref, *, seq):
    past = PAST_LEN

    @pl.when(pl.program_id(2) == 0)
    def _():
        kf_ref[0:past, :] = ck_ref[...].astype(BF16)
        vt_ref[:, 0:past] = cv_ref[...].T.astype(BF16)
        for s in range(seq // KV_SUB):
            r = s * KV_SUB
            blk = kv_ref[r:r + KV_SUB, :]
            kn = _rope(_seg_rms_pair(blk, kg_ref[...]), cosk_ref[r:r + KV_SUB, :], sink_ref[r:r + KV_SUB, :])
            kf_ref[past + r:past + r + KV_SUB, :] = kn[:, 0:64].astype(BF16)
            vt_ref[:, past + r:past + r + KV_SUB] = blk[:, 64:128].T.astype(BF16)

    n_sub = q_ref.shape[0] // ATT_SUB
    heads_t = []
    for j in range(n_sub):
        rows = slice(j * ATT_SUB, (j + 1) * ATT_SUB)
        for half in range(2):
            qh = _seg_rms_pair(q_ref[rows, half * 128:(half + 1) * 128], qg_ref[...])
            qh = _rope(qh, cosq_ref[rows, :], sinq_ref[rows, :]) * (64 ** -0.5)
            heads_t += [qh[:, 0:64].T.astype(BF16), qh[:, 64:128].T.astype(BF16)]
    kf = kf_ref[...]
    scores = [jnp.dot(kf, q_t, preferred_element_type=F32) for q_t in heads_t]
    probs = [_exp_scores_t(s) for s in scores]
    vt = vt_ref[...]
    outs = [_pv_t(vt, el).T for el in probs]
    for j in range(n_sub):
        o_ref[j * ATT_SUB:(j + 1) * ATT_SUB, :] = jnp.concatenate(outs[4 * j:4 * j + 4], axis=1).astype(BF16)


def _gqa_ctx_kernel(*refs, n_carry=0):
    q_ref, kv_ref, qg_ref, kg_ref = refs[0:4]
    o_ref, kn_ref, vo_ref = refs[4 + n_carry:]
    outs = []
    for kvh in range(GQA_KV_HEADS):
        blk = kv_ref[:, kvh * 128:(kvh + 1) * 128]
        kn = _seg_rms_pair(blk, kg_ref[...])[:, 0:64]
        kn_ref[:, kvh * 64:(kvh + 1) * 64] = kn
        vo_ref[:, kvh * 64:(kvh + 1) * 64] = blk[:, 64:128]
        heads = []
        for half in range(2):
            c0 = kvh * 256 + half * 128
            qh = _seg_rms_pair(q_ref[:, c0:c0 + 128], qg_ref[...]) * (64 ** -0.5)
            heads += [qh[:, 0:64], qh[:, 64:128]]
        outs += _gqa_attend(heads, kn.astype(BF16), blk[:, 64:128].astype(BF16))
    o_ref[...] = jnp.concatenate(outs, axis=1).astype(BF16)


def _mixer_gqa(u, row0, nb, seq, q_gain, k_gain, ctx, layer=0, carry=None):
    roff = row0 // seq
    o_shape = jax.ShapeDtypeStruct((nb * seq, GROUP_W), BF16)
    if ctx is None:
        gain = pl.BlockSpec((1, 128), lambda b: (0, 0))
        c_specs, c_args = _carried(carry)
        kvw = GQA_KV_HEADS * GQA_HEAD_DIM
        slab = pl.BlockSpec((None, None, seq, kvw), lambda b: (b, layer, 0, 0))
        slab_shape = jax.ShapeDtypeStruct((nb, DEPTH, seq, kvw), F32)
        return pl.pallas_call(
            functools.partial(_gqa_ctx_kernel, n_carry=len(c_args)),
            grid=(nb,),
            in_specs=[pl.BlockSpec((seq, GROUP_W), lambda b: (b + roff, D_Q // GROUP_W)),
                      pl.BlockSpec((seq, 256), lambda b: (b + roff, D_KV // 256)), gain, gain] + c_specs,
            out_specs=[pl.BlockSpec((seq, GROUP_W), lambda b: (b, 0)), slab, slab],
            out_shape=[o_shape, slab_shape, slab_shape],
            input_output_aliases={4 + k: 1 + k for k in range(len(c_args))},
            compiler_params=_cparams(("arbitrary",)),
            name="mixer_gqa_ctx",
        )(u, u, q_gain, k_gain, *c_args)
    ck, cv, cos, sin = ctx
    qb = GQA_QB
    qoff = row0 // qb
    nq = seq // qb
    lk = seq + PAST_LEN
    in_specs = [
        pl.BlockSpec((qb, 256), lambda b, h, i: (qoff + b * nq + i, D_Q // 256 + h)),
        pl.BlockSpec((seq, 128), lambda b, h, i: (b + roff, D_KV // 128 + h)),
        pl.BlockSpec((1, 128), lambda b, h, i: (0, 0)),
        pl.BlockSpec((1, 128), lambda b, h, i: (0, 0)),
        pl.BlockSpec((None, None, PAST_LEN, 64), lambda b, h, i: (b, h, 0, 0)),
        pl.BlockSpec((None, None, PAST_LEN, 64), lambda b, h, i: (b, h, 0, 0)),
        pl.BlockSpec((qb, 128), lambda b, h, i: (i, 0)),
        pl.BlockSpec((qb, 128), lambda b, h, i: (i, 0)),
        pl.BlockSpec((seq, 128), lambda b, h, i: (0, 0)),
        pl.BlockSpec((seq, 128), lambda b, h, i: (0, 0)),
    ]
    return pl.pallas_call(
        functools.partial(_gqa_dec_kernel, seq=seq),
        grid=(nb, GQA_KV_HEADS, nq),
        in_specs=in_specs,
        out_specs=pl.BlockSpec((qb, 256), lambda b, h, i: (b * nq + i, h)),
        out_shape=o_shape,
        scratch_shapes=[pltpu.VMEM((lk, 64), BF16), pltpu.VMEM((64, lk), BF16)],
        compiler_params=_cparams(("arbitrary", "arbitrary", "arbitrary")),
        name="mixer_gqa_dec",
    )(u, u, q_gain, k_gain, ck, cv, cos, sin, cos, sin)


def _rope_tables(n_tok):
    rows = n_tok // GRID_W
    r, c = jnp.meshgrid(jnp.arange(rows, dtype=F32), jnp.arange(GRID_W, dtype=F32), indexing='ij')
    nf = 16
    freqs = ROPE_THETA ** (-jnp.arange(nf, dtype=F32) / nf)
    ang = jnp.stack([r.reshape(-1)[:, None] * freqs, c.reshape(-1)[:, None] * freqs], axis=1)
    cos, sin = jnp.cos(ang), jnp.sin(ang)
    cos64 = jnp.concatenate([cos, cos], axis=2).reshape(n_tok, 64)
    sin64 = jnp.concatenate([-sin, sin], axis=2).reshape(n_tok, 64)
    return jnp.tile(cos64, (1, 2)), jnp.tile(sin64, (1, 2))


def _pad_lanes(v, n=128):
    v = v.reshape(1, -1)
    return jnp.pad(v, ((0, 0), (0, n - v.shape[1])))


def kernel(x_prompt, x_sample, c, cache_diff_k, cache_diff_v, cache_gqa_k, cache_gqa_v, state_ssm, state_mlstm_c, state_mlstm_n, state_mlstm_m, c_ctx, w_ada, b_ada, norm1, norm2, w_in, w_out, conv_ssd_w, conv_ssd_b, ssd_a_log, ssd_dt_bias, ssd_d, ssd_norm, diff_lq1, diff_lk1, diff_lq2, diff_lk2, conv_mlstm_w, conv_mlstm_b, mlstm_gate_b, mlstm_norm, gqa_q_norm, gqa_k_norm, w_ffn_in, w_ffn_out, norm_f):
    w_in_p = _prep_w_in(jnp.swapaxes(w_in, 1, 2))
    w_out_b = w_out.astype(BF16)
    gain1 = norm1.reshape(DEPTH, 1, D_MODEL)
    gain2 = norm2.reshape(DEPTH, 1, D_MODEL)
    w_ffn_in_b = w_ffn_in.astype(BF16)
    w_ffn_out_b = w_ffn_out.astype(BF16)
    cvec = jnp.concatenate([c_ctx[None, :], c, jnp.zeros((5, D_MODEL), F32)], axis=0)
    cos_t, sin_t = _rope_tables(DEC_SEQ)
    pad_taps = lambda w: jnp.pad(w, ((0, 8 - D_CONV), (0, 0)))

    mod = _modulation(cvec, w_ada, b_ada).reshape(DEPTH, 8, 1, 6 * D_MODEL)
    x = jnp.concatenate([x_prompt.reshape(N_CTX, D_MODEL), x_sample.reshape(N_DEC, D_MODEL)], axis=0)

    zeros = lambda *shape: jnp.zeros((BATCH, DEPTH) + shape, F32)
    ssm_all = (zeros(2, SSM_HEADS, SSM_HEAD_DIM, SSM_STATE),)
    diff_kv_all = [zeros(SEQ, GROUP_W), zeros(SEQ, GROUP_W)]
    mlstm_all = [zeros(2, MLSTM_HEADS, MLSTM_HEAD_DIM, MLSTM_HEAD_DIM), zeros(2, MLSTM_HEADS, 1, MLSTM_HEAD_DIM),
                 zeros(2, MLSTM_HEADS, 1, MLSTM_HEAD_DIM)]
    gqa_kv_all = [zeros(SEQ, GQA_KV_HEADS * GQA_HEAD_DIM), zeros(SEQ, GQA_KV_HEADS * GQA_HEAD_DIM)]
    for l in range(DEPTH):
        u = _inproj(x, mod, gain1, w_in_p, l)

        ssd_args = (pad_taps(conv_ssd_w[l]), conv_ssd_b[l].reshape(1, -1), _pad_lanes(ssd_a_log[l]),
                    _pad_lanes(ssd_dt_bias[l]), jnp.repeat(ssd_d[l], SSM_HEAD_DIM).reshape(1, GROUP_W),
                    ssd_norm[l].reshape(1, GROUP_W))
        ya_c, ssm_new = _mixer_ssd(u, 0, BATCH, SEQ, *ssd_args, None, l, ssm_all)
        ssm_all = (ssm_new,)
        ya_d = _mixer_ssd(u, N_CTX, DEC_BATCH, DEC_SEQ, *ssd_args, state_ssm[:, l])

        lam_init = 0.8 - 0.6 * math.exp(-0.3 * l)
        lam_rows = jnp.pad(jnp.stack([diff_lq1[l], diff_lk1[l], diff_lq2[l], diff_lk2[l]]), ((0, 4), (0, 64)))
        yb_c, *diff_kv_all = _mixer_diff(u, 0, BATCH, SEQ, lam_rows, lam_init, None, l, diff_kv_all)
        yb_d = _mixer_diff(u, N_CTX, DEC_BATCH, DEC_SEQ, lam_rows, lam_init,
                           (cache_diff_k[:, l].reshape(DEC_BATCH, PAST_LEN, GROUP_W),
                            cache_diff_v[:, l].reshape(DEC_BATCH, PAST_LEN, GROUP_W), cos_t, sin_t))

        ml_args = (pad_taps(conv_mlstm_w[l]), conv_mlstm_b[l].reshape(1, -1), _pad_lanes(mlstm_gate_b[l]),
                   mlstm_norm[l].reshape(1, GROUP_W))
        yc_c, *mlstm_all = _mixer_mlstm(u, 0, BATCH, SEQ, *ml_args, None, l, mlstm_all)
        m0 = jnp.broadcast_to(state_mlstm_m[:, l][..., None, None], (DEC_BATCH, 2, MLSTM_HEADS, 1, MLSTM_HEAD_DIM))
        yc_d = _mixer_mlstm(u, N_CTX, DEC_BATCH, DEC_SEQ, *ml_args,
                            (state_mlstm_c[:, l], state_mlstm_n[:, l][:, :, :, None, :], m0))

        q_gain = jnp.tile(gqa_q_norm[l], 2).reshape(1, 128)
        k_gain = jnp.tile(gqa_k_norm[l], 2).reshape(1, 128)
        yd_c, *gqa_kv_all = _mixer_gqa(u, 0, BATCH, SEQ, q_gain, k_gain, None, l, gqa_kv_all)
        yd_d = _mixer_gqa(u, N_CTX, DEC_BATCH, DEC_SEQ, q_gain, k_gain,
                          (cache_gqa_k[:, l].transpose(0, 2, 1, 3), cache_gqa_v[:, l].transpose(0, 2, 1, 3), cos_t, sin_t))

        x = _outproj((ya_c, yb_c, yc_c, yd_c), (ya_d, yb_d, yc_d, yd_d), x, mod, w_out_b, l)
        x = _ffn(x, mod, gain2, w_ffn_in_b, w_ffn_out_b, l)

    y_prompt = _final_norm(x, norm_f.reshape(1, D_MODEL), 0, N_CTX).reshape(BATCH, SEQ, D_MODEL)
    y_sample = _final_norm(x, norm_f.reshape(1, D_MODEL), N_CTX, N_DEC).reshape(DEC_BATCH, DEC_SEQ, D_MODEL)
    c_all, n_all, m_all = mlstm_all
    return (y_prompt, y_sample,
            diff_kv_all[0].reshape(BATCH, DEPTH, SEQ, DIFF_HEADS, 2, 64),
            diff_kv_all[1].reshape(BATCH, DEPTH, SEQ, DIFF_HEADS, 128),
            gqa_kv_all[0].reshape(BATCH, DEPTH, SEQ, GQA_KV_HEADS, GQA_HEAD_DIM),
            gqa_kv_all[1].reshape(BATCH, DEPTH, SEQ, GQA_KV_HEADS, GQA_HEAD_DIM),
            ssm_all[0], c_all, n_all[:, :, :, :, 0, :], m_all[:, :, :, :, 0, 0])
```

```python
import functools
import math

import numpy as np
import jax
import jax.numpy as jnp
from jax import lax
from jax.experimental import pallas as pl
from jax.experimental.pallas import tpu as pltpu

F32 = jnp.float32
BF16 = jnp.bfloat16

D_MODEL = 1024
BATCH = 16
SEQ = 256
DEPTH = 4
DEC_BATCH = 2
DEC_SEQ = 2048
PAST_LEN = 256
GRID_W = 64
GROUP_W = 512
D_MIX = 2048
CHUNK = 128
Q_BLOCK = 128
D_CONV = 5
ROPE_THETA = 10000.0
EPS = 1e-6
SSM_HEADS = 8
SSM_HEAD_DIM = 64
SSM_STATE = 64
SSM_CONV_CH = 768
DIFF_HEADS = 4
MLSTM_HEADS = 4
MLSTM_HEAD_DIM = 128
GQA_KV_HEADS = 2
GQA_HEAD_DIM = 64
IN_COLS = 5664
D_FF = 2816

N_CTX = BATCH * SEQ
N_DEC = DEC_BATCH * DEC_SEQ
N_TOK = N_CTX + N_DEC

U_COLS = 6144
C_QK, C_V, C_O = 0, 1024, 1536
A_Z = 2048
B_Q, B_K, B_V = 2560, 3072, 3584
D_Q = 4096
A_XBC = 4608
D_KV = 5376
A_DT = 5632
C_G = 5760

VMEM_LIMIT_BYTES = 56 * 1024 * 1024


def _in_col_permutation():
    idx = np.full((U_COLS,), IN_COLS, np.int32)
    a0, b0, c0, d0 = 0, 1296, 2832, 4896

    def put(dst, src, n):
        idx[dst:dst + n] = np.arange(src, src + n)

    put(A_Z, a0, 512)
    put(A_XBC, a0 + 512, 768)
    put(A_DT, a0 + 1280, 16)
    put(B_Q, b0, 512)
    put(B_K, b0 + 512, 512)
    put(B_V, b0 + 1024, 512)
    put(C_QK, c0, 1024)
    put(C_V, c0 + 1024, 512)
    put(C_O, c0 + 1536, 512)
    put(C_G, c0 + 2048, 16)
    put(D_Q, d0, 512)
    for kv in range(GQA_KV_HEADS):
        put(D_KV + kv * 128, d0 + 512 + kv * 64, 64)
        put(D_KV + kv * 128 + 64, d0 + 640 + kv * 64, 64)
    return idx


_IN_PERM = _in_col_permutation()


def _runs(idx):
    out, start = [], 0
    for p in range(1, len(idx) + 1):
        pad = idx[start] == IN_COLS
        if p == len(idx) or (idx[p] == IN_COLS) != pad or (not pad and idx[p] != idx[p - 1] + 1):
            out.append((None if pad else int(idx[start]), p - start))
            start = p
    return out


_IN_SEGMENTS = _runs(_IN_PERM)


def _cparams(sem):
    return pltpu.CompilerParams(dimension_semantics=sem, vmem_limit_bytes=VMEM_LIMIT_BYTES)


def _bdot(a, b):
    return jnp.dot(a.astype(BF16), b.astype(BF16), preferred_element_type=F32)


def _bdot_t(a, b):
    return lax.dot_general(a.astype(BF16), b.astype(BF16), (((1,), (1,)), ((), ())),
                           preferred_element_type=F32)


def _split3(v):
    hi = v.astype(BF16)
    r = v - hi.astype(F32)
    mid = r.astype(BF16)
    lo = (r - mid.astype(F32)).astype(BF16)
    return hi, mid, lo


def _xdot_l(e, v):
    hi, mid, lo = _split3(v)
    return (jnp.dot(e, hi, preferred_element_type=F32) + jnp.dot(e, mid, preferred_element_type=F32)
            + jnp.dot(e, lo, preferred_element_type=F32))


def _xdot_r(v, e):
    hi, mid, lo = _split3(v)
    return (jnp.dot(hi, e, preferred_element_type=F32) + jnp.dot(mid, e, preferred_element_type=F32)
            + jnp.dot(lo, e, preferred_element_type=F32))


def _sigmoid(x):
    return 0.5 * jnp.tanh(0.5 * x) + 0.5


def _silu(x):
    h = 0.5 * x
    return h * jnp.tanh(h) + h


def _softplus(x):
    return jnp.maximum(x, 0.0) + jnp.log1p(jnp.exp(-jnp.abs(x)))


def _log_sigmoid(x):
    return jnp.minimum(x, 0.0) - jnp.log(1.0 + jnp.exp(-jnp.abs(x)))


def _rms(x):
    return x * lax.rsqrt(jnp.mean(x * x, axis=-1, keepdims=True) + EPS)


def _iota(shape, dim):
    return lax.broadcasted_iota(jnp.int32, shape, dim)


def _rope(x, cos, sin_signed):
    lane = _iota(x.shape, 1)
    first = (lane % 32) < 16
    xr = jnp.where(first, pltpu.roll(x, 112, 1), pltpu.roll(x, 16, 1))
    return x * cos + xr * sin_signed


def _mod_index(tile_rows):
    n_ctx_tiles = N_CTX // tile_rows
    per_sample = DEC_SEQ // tile_rows

    def f(i):
        return jnp.where(i < n_ctx_tiles, 0, 1 + (i - n_ctx_tiles) // per_sample)

    return f


def _win_prep_kernel(w_ref, o_ref):
    row = 0
    for s, n in _IN_SEGMENTS:
        if s is None:
            o_ref[row:row + n, :] = jnp.zeros((n, o_ref.shape[1]), BF16)
        else:
            o_ref[row:row + n, :] = w_ref[s:s + n, :].astype(BF16)
        row += n


def _prep_w_in(w_in_t):
    tc = 256
    return pl.pallas_call(
        _win_prep_kernel,
        grid=(DEPTH, D_MODEL // tc),
        in_specs=[pl.BlockSpec((None, IN_COLS, tc), lambda l, i: (l, 0, i))],
        out_specs=pl.BlockSpec((None, U_COLS, tc), lambda l, i: (l, 0, i)),
        out_shape=jax.ShapeDtypeStruct((DEPTH, U_COLS, D_MODEL), BF16),
        compiler_params=_cparams(("arbitrary", "arbitrary")),
        name="w_in_layout",
    )(w_in_t)


def _mod_kernel(c_ref, w_ref, b_ref, o_ref):
    c = c_ref[...]
    s = _silu(c)
    s_hi = s.astype(BF16)
    s_lo = (s - s_hi.astype(F32)).astype(BF16)
    w = w_ref[...]
    w_hi = w.astype(BF16)
    w_lo = (w - w_hi.astype(F32)).astype(BF16)
    acc = jnp.dot(s_hi, w_hi, preferred_element_type=F32)
    acc = acc + jnp.dot(s_hi, w_lo, preferred_element_type=F32)
    acc = acc + jnp.dot(s_lo, w_hi, preferred_element_type=F32)
    o_ref[...] = acc + b_ref[...]


def _modulation(cvec, w_ada, b_ada):
    tn = 1536
    n = 6 * D_MODEL
    return pl.pallas_call(
        _mod_kernel,
        grid=(DEPTH, n // tn),
        in_specs=[
            pl.BlockSpec((8, D_MODEL), lambda l, j: (0, 0)),
            pl.BlockSpec((None, D_MODEL, tn), lambda l, j: (l, 0, j)),
            pl.BlockSpec((None, 1, tn), lambda l, j: (l, 0, j)),
        ],
        out_specs=pl.BlockSpec((None, 8, tn), lambda l, j: (l, 0, j)),
        out_shape=jax.ShapeDtypeStruct((DEPTH, 8, n), F32),
        compiler_params=_cparams(("arbitrary", "arbitrary")),
        name="adaln_mod",
    )(cvec, w_ada, b_ada.reshape(DEPTH, 1, n))


ROW_SUB = 256


def _norm_mod_to(h_ref, x_ref, gain_ref, shift, scale, rows):
    def body(s, carry):
        r = pl.multiple_of(s * ROW_SUB, ROW_SUB)
        x = x_ref[pl.ds(r, ROW_SUB), :]
        h = (_rms(x) * gain_ref[...]) * (1.0 + scale) + shift
        h_ref[pl.ds(r, ROW_SUB), :] = h.astype(BF16)
        return carry

    lax.fori_loop(0, rows // ROW_SUB, body, 0)


def _inproj_kernel(x_ref, mod_ref, gain_ref, w_ref, u_ref, h_ref, *, tm):
    @pl.when(pl.program_id(1) == 0)
    def _():
        shift = mod_ref[:, 0:D_MODEL]
        scale = mod_ref[:, D_MODEL:2 * D_MODEL]
        _norm_mod_to(h_ref, x_ref, gain_ref, shift, scale, tm)

    u_ref[...] = lax.dot_general(h_ref[...], w_ref[...], (((1,), (1,)), ((), ())), preferred_element_type=F32)


def _inproj(x, mod, gain, w, l):
    tm, tn = 2048, 768
    midx = _mod_index(tm)
    return pl.pallas_call(
        functools.partial(_inproj_kernel, tm=tm),
        grid=(N_TOK // tm, U_COLS // tn),
        in_specs=[
            pl.BlockSpec((tm, D_MODEL), lambda i, j: (i, 0)),
            pl.BlockSpec((None, None, 1, 6 * D_MODEL), lambda i, j: (l, midx(i), 0, 0)),
            pl.BlockSpec((None, 1, D_MODEL), lambda i, j: (l, 0, 0)),
            pl.BlockSpec((None, tn, D_MODEL), lambda i, j: (l, j, 0)),
        ],
        out_specs=pl.BlockSpec((tm, tn), lambda i, j: (i, j)),
        out_shape=jax.ShapeDtypeStruct((N_TOK, U_COLS), F32),
        scratch_shapes=[pltpu.VMEM((tm, D_MODEL), BF16)],
        compiler_params=_cparams(("arbitrary", "arbitrary")),
        name="in_proj",
    )(x, mod, gain, w)


def _outproj_kernel(*refs, n_ctx_tiles):
    ctx_refs = refs[0:4]
    dec_refs = refs[4:8]
    x_ref, mod_ref, w_ref, o_ref = refs[8:12]
    i = pl.program_id(0)

    def compute(ys):
        acc = jnp.dot(ys[0][...], w_ref[0:512, :], preferred_element_type=F32)
        for g in range(1, 4):
            acc = acc + jnp.dot(ys[g][...], w_ref[g * 512:(g + 1) * 512, :], preferred_element_type=F32)
        gate = mod_ref[:, 2 * D_MODEL:3 * D_MODEL]
        o_ref[...] = x_ref[...] + gate * acc

    @pl.when(i < n_ctx_tiles)
    def _():
        compute(ctx_refs)

    @pl.when(i >= n_ctx_tiles)
    def _():
        compute(dec_refs)


def _outproj(ys_ctx, ys_dec, x, mod, w, l):
    tm = 512
    nct = N_CTX // tm
    midx = _mod_index(tm)
    ctx_spec = pl.BlockSpec((tm, GROUP_W), lambda i: (jnp.minimum(i, nct - 1), 0))
    dec_spec = pl.BlockSpec((tm, GROUP_W), lambda i: (jnp.maximum(i - nct, 0), 0))
    return pl.pallas_call(
        functools.partial(_outproj_kernel, n_ctx_tiles=nct),
        grid=(N_TOK // tm,),
        in_specs=[ctx_spec] * 4 + [dec_spec] * 4 + [
            pl.BlockSpec((tm, D_MODEL), lambda i: (i, 0)),
            pl.BlockSpec((None, None, 1, 6 * D_MODEL), lambda i: (l, midx(i), 0, 0)),
            pl.BlockSpec((None, D_MIX, D_MODEL), lambda i: (l, 0, 0)),
        ],
        out_specs=pl.BlockSpec((tm, D_MODEL), lambda i: (i, 0)),
        out_shape=jax.ShapeDtypeStruct((N_TOK, D_MODEL), F32),
        compiler_params=_cparams(("arbitrary",)),
        name="out_proj",
    )(*ys_ctx, *ys_dec, x, mod, w)


def _ffn_kernel(x_ref, mod_ref, gain_ref, wg_ref, wu_ref, wo_ref, o_ref, *, tm):
    shift = mod_ref[:, 3 * D_MODEL:4 * D_MODEL]
    scale = mod_ref[:, 4 * D_MODEL:5 * D_MODEL]
    g2 = mod_ref[:, 5 * D_MODEL:6 * D_MODEL]

    def body(s, carry):
        r = pl.multiple_of(s * ROW_SUB, ROW_SUB)
        x = x_ref[pl.ds(r, ROW_SUB), :]
        h = ((_rms(x) * gain_ref[...]) * (1.0 + scale) + shift).astype(BF16)
        gate = jnp.dot(h, wg_ref[...], preferred_element_type=F32)
        up = jnp.dot(h, wu_ref[...], preferred_element_type=F32)
        act = (_silu(gate) * up).astype(BF16)
        o_ref[pl.ds(r, ROW_SUB), :] = x + g2 * jnp.dot(act, wo_ref[...], preferred_element_type=F32)
        return carry

    lax.fori_loop(0, tm // ROW_SUB, body, 0)


def _ffn(x, mod, gain, w_in, w_out, l):
    tm = 1024
    midx = _mod_index(tm)
    resident = dict(pipeline_mode=pl.Buffered(1))
    return pl.pallas_call(
        functools.partial(_ffn_kernel, tm=tm),
        grid=(N_TOK // tm,),
        in_specs=[
            pl.BlockSpec((tm, D_MODEL), lambda i: (i, 0)),
            pl.BlockSpec((None, None, 1, 6 * D_MODEL), lambda i: (l, midx(i), 0, 0)),
            pl.BlockSpec((None, 1, D_MODEL), lambda i: (l, 0, 0)),
            pl.BlockSpec((None, D_MODEL, D_FF), lambda i: (l, 0, 0), **resident),
            pl.BlockSpec((None, D_MODEL, D_FF), lambda i: (l, 0, 1), **resident),
            pl.BlockSpec((None, D_FF, D_MODEL), lambda i: (l, 0, 0), **resident),
        ],
        out_specs=pl.BlockSpec((tm, D_MODEL), lambda i: (i, 0)),
        out_shape=jax.ShapeDtypeStruct((N_TOK, D_MODEL), F32),
        compiler_params=_cparams(("arbitrary",)),
        name="ffn",
    )(x, mod, gain, w_in, w_in, w_out)


def _final_norm_kernel(x_ref, g_ref, o_ref):
    o_ref[...] = _rms(x_ref[...]) * g_ref[...]


def _final_norm(x, gain, row0, rows):
    tm = 512
    off = row0 // tm
    return pl.pallas_call(
        _final_norm_kernel,
        grid=(rows // tm,),
        in_specs=[pl.BlockSpec((tm, D_MODEL), lambda i: (i + off, 0)),
                  pl.BlockSpec((1, D_MODEL), lambda i: (0, 0))],
        out_specs=pl.BlockSpec((tm, D_MODEL), lambda i: (i, 0)),
        out_shape=jax.ShapeDtypeStruct((rows, D_MODEL), F32),
        compiler_params=_cparams(("arbitrary",)),
        name="final_norm",
    )(x, gain)


def _causal_masks():
    i = _iota((CHUNK, CHUNK), 0)
    j = _iota((CHUNK, CHUNK), 1)
    return j <= i, j >= i


def _conv_silu_chunk(pad_ref, w_ref, b_ref, r):
    win = pad_ref[pl.ds(r, CHUNK + 16), :]
    acc = b_ref[...] + w_ref[0:1, :] * win[6:6 + CHUNK, :]
    for tap in range(1, D_CONV):
        acc = acc + w_ref[tap:tap + 1, :] * win[6 + tap:6 + tap + CHUNK, :]
    return _silu(acc)


def _fill_padded(pad_ref, src_ref, seq, width, n_seq=1):
    zeros = jnp.zeros((8, width), F32)
    for i in range(n_seq):
        base = i * (seq + 16)
        pad_ref[base:base + 8, :] = zeros
        pad_ref[base + seq + 8:base + seq + 16, :] = zeros

        def body(c, carry, base=base, src0=i * seq):
            r = pl.multiple_of(c * CHUNK, CHUNK)
            pad_ref[pl.ds(base + 8 + r, CHUNK), :] = src_ref[pl.ds(src0 + r, CHUNK), :]
            return carry

        lax.fori_loop(0, seq // CHUNK, body, 0)


SSD_CPI = 2


def _ssd_kernel(*refs, seq, has_state, n_carry=0):
    (z_ref, xbc_ref, dt_ref, cw_ref, cb_ref, alog_ref, dtb_ref, dsk_ref, ng_ref) = refs[0:9]
    refs = refs[9 + n_carry:]
    if has_state:
        h0_ref, y_ref = refs[0:2]
        hfin_ref = None
    else:
        h0_ref = None
        y_ref, hfin_ref = refs[0:2]
    pad_ref, xc_ref, yf_ref, yb_ref, st_ref = refs[2:7]
    nc = seq // CHUNK
    hpg = SSM_HEADS // 2
    gw = hpg * SSM_HEAD_DIM

    _fill_padded(pad_ref, xbc_ref, seq, SSM_CONV_CH)

    def conv_body(c, carry):
        r = pl.multiple_of(c * CHUNK, CHUNK)
        xc_ref[pl.ds(r, CHUNK), :] = _conv_silu_chunk(pad_ref, cw_ref, cb_ref, r)
        return carry

    lax.fori_loop(0, nc, conv_body, 0)

    zero_blk = jnp.zeros((SSM_STATE, SSM_HEAD_DIM), F32)
    for d in range(2):
        for g in range(2):
            if has_state:
                rows = []
                for h4 in range(hpg):
                    blk = h0_ref[d, g * hpg + h4].T
                    rows.append(jnp.concatenate([blk if k == h4 else zero_blk for k in range(hpg)], axis=1))
                st_ref[d, g] = jnp.concatenate(rows, axis=0)
            else:
                st_ref[d, g] = jnp.zeros((gw, gw), F32)

    mask_f, mask_b = _causal_masks()
    masks = (mask_f, mask_b)
    tmats = (mask_f.astype(BF16), mask_b.astype(BF16))
    a_neg = -jnp.exp(alog_ref[...])
    lane_head = _iota((CHUNK, gw), 1) // SSM_HEAD_DIM
    blk_diag = (_iota((gw, gw), 0) // SSM_STATE) == (_iota((gw, gw), 1) // SSM_HEAD_DIM)

    def step(items):
        xcs = [xc_ref[pl.ds(r, CHUNK), :] for d, r in items]
        dts = [_softplus(dt_ref[pl.ds(r, CHUNK), :] + dtb_ref[...]) for d, r in items]
        cum_c = [_xdot_l(tmats[d], dts[n] * a_neg) for n, (d, r) in enumerate(items)]
        gms = [[_bdot_t(xcs[n][:, 640 + g * 64:640 + (g + 1) * 64], xcs[n][:, 512 + g * 64:512 + (g + 1) * 64])
                for g in range(2)] for n in range(len(items))]
        lhs_d, rhs_d, lhs_o, lhs_s, xgs, elcols = [], [], [], [], [], []
        for n, (d, r) in enumerate(items):
            cum_r = cum_c[n].T
            dt_r = dts[n].T
            last = cum_c[n][CHUNK - 1:CHUNK, :] if d == 0 else cum_c[n][0:1, :]
            e_last = jnp.exp(last)
            for g in range(2):
                xg = xcs[n][:, g * gw:(g + 1) * gw]
                bt = xcs[n][:, 512 + g * 64:512 + (g + 1) * 64].T
                cm = xcs[n][:, 640 + g * 64:640 + (g + 1) * 64]
                s_l, cec_l, btw_l, xm_l, el_l = [], [], [], [], []
                for h4 in range(hpg):
                    col = d * SSM_HEADS + g * hpg + h4
                    cc = jnp.broadcast_to(cum_c[n][:, col:col + 1], (CHUNK, CHUNK))
                    cr = cum_r[col:col + 1, :]
                    dtr = dt_r[col:col + 1, :]
                    decay = jnp.exp(jnp.where(masks[d], cc - cr, -jnp.inf))
                    s_l.append((decay * gms[n][g] * dtr).astype(BF16))
                    cec_l.append((cm * jnp.exp(cc[:, 0:SSM_STATE])).astype(BF16))
                    btw_l.append((bt * (jnp.exp(last[:, col:col + 1] - cr) * dtr)).astype(BF16))
                    xm_l.append(jnp.where(lane_head == h4, xg, 0.0).astype(BF16))
                    el_l.append(jnp.broadcast_to(e_last[:, col:col + 1], (SSM_STATE, gw)))
                lhs_d.append(jnp.concatenate(s_l, axis=1))
                rhs_d.append(jnp.concatenate(xm_l, axis=0))
                lhs_o.append(jnp.concatenate(cec_l, axis=1))
                lhs_s.append(jnp.concatenate(btw_l, axis=0))
                xgs.append(xg.astype(BF16))
                elcols.append(jnp.concatenate(el_l, axis=0))
        yds = [jnp.dot(lhs_d[k], rhs_d[k], preferred_element_type=F32) for k in range(len(lhs_d))]
        css = [jnp.dot(lhs_s[k], xgs[k], preferred_element_type=F32) for k in range(len(lhs_s))]
        outs = []
        for n, (d, r) in enumerate(items):
            ys = []
            for g in range(2):
                k = 2 * n + g
                st = st_ref[d, g]
                ys.append(yds[k] + jnp.dot(lhs_o[k], st.astype(BF16), preferred_element_type=F32))
                st_ref[d, g] = elcols[k] * st + jnp.where(blk_diag, css[k], 0.0)
            outs.append(jnp.concatenate(ys, axis=1))
        return outs

    def scan_body(s, carry):
        cf = SSD_CPI * s
        cb = nc - 1 - cf
        rows_f = [pl.multiple_of((cf + t) * CHUNK, CHUNK) for t in range(SSD_CPI)]
        rows_b = [pl.multiple_of((cb - t) * CHUNK, CHUNK) for t in range(SSD_CPI)]
        ys = step([(0, r) for r in rows_f] + [(1, r) for r in rows_b])
        for t in range(SSD_CPI):
            yf_ref[pl.ds(rows_f[t], CHUNK), :] = ys[t]
            yb_ref[pl.ds(rows_b[t], CHUNK), :] = ys[SSD_CPI + t]
        return carry

    lax.fori_loop(0, nc // SSD_CPI, scan_body, 0)

    def out_body(c, carry):
        r = pl.multiple_of(c * CHUNK, CHUNK)
        x = xc_ref[pl.ds(r, CHUNK), 0:GROUP_W]
        y = yf_ref[pl.ds(r, CHUNK), :] + yb_ref[pl.ds(r, CHUNK), :] + dsk_ref[...] * x
        y = y * _silu(z_ref[pl.ds(r, CHUNK), :])
        y_ref[pl.ds(r, CHUNK), :] = (_rms(y) * ng_ref[...]).astype(BF16)
        return carry

    lax.fori_loop(0, nc, out_body, 0)

    if hfin_ref is not None:
        for d in range(2):
            for g in range(2):
                st = st_ref[d, g]
                for h4 in range(hpg):
                    lo = h4 * SSM_STATE
                    hfin_ref[d, g * hpg + h4] = st[lo:lo + SSM_STATE, lo:lo + SSM_HEAD_DIM].T


def _carried(carry):
    if carry is None:
        return [], []
    return [pl.BlockSpec(memory_space=pl.ANY)] * len(carry), list(carry)


def _mixer_ssd(u, row0, nb, seq, conv_w, conv_b, a_log, dt_bias, d_skip, norm_g, h0, layer=0, carry=None):
    has_state = h0 is not None
    roff = row0 // seq
    vec = lambda n: pl.BlockSpec((1, n), lambda b: (0, 0))
    in_specs = [
        pl.BlockSpec((seq, GROUP_W), lambda b: (b + roff, A_Z // GROUP_W)),
        pl.BlockSpec((seq, SSM_CONV_CH), lambda b: (b + roff, A_XBC // SSM_CONV_CH)),
        pl.BlockSpec((seq, 128), lambda b: (b + roff, A_DT // 128)),
        pl.BlockSpec((8, SSM_CONV_CH), lambda b: (0, 0)),
        vec(SSM_CONV_CH), vec(128), vec(128), vec(GROUP_W), vec(GROUP_W),
    ]
    args = [u, u, u, conv_w, conv_b, a_log, dt_bias, d_skip, norm_g]
    st_block = (None, 2, SSM_HEADS, SSM_HEAD_DIM, SSM_STATE)
    y_spec = pl.BlockSpec((seq, GROUP_W), lambda b: (b, 0))
    y_shape = jax.ShapeDtypeStruct((nb * seq, GROUP_W), BF16)
    aliases = {}
    n_carry = 0
    if has_state:
        in_specs.append(pl.BlockSpec(st_block, lambda b: (b, 0, 0, 0, 0)))
        args.append(h0)
        out_specs, out_shape = y_spec, y_shape
    else:
        c_specs, c_args = _carried(carry)
        n_carry = len(c_args)
        aliases = {len(args) + k: 1 + k for k in range(n_carry)}
        in_specs += c_specs
        args += c_args
        out_specs = [y_spec, pl.BlockSpec((None,) + st_block, lambda b: (b, layer, 0, 0, 0, 0))]
        out_shape = [y_shape, jax.ShapeDtypeStruct((nb, DEPTH, 2, SSM_HEADS, SSM_HEAD_DIM, SSM_STATE), F32)]
    return pl.pallas_call(
        functools.partial(_ssd_kernel, seq=seq, has_state=has_state, n_carry=n_carry),
        grid=(nb,),
        in_specs=in_specs,
        out_specs=out_specs,
        out_shape=out_shape,
        input_output_aliases=aliases,
        scratch_shapes=[
            pltpu.VMEM((seq + 16, SSM_CONV_CH), F32),
            pltpu.VMEM((seq, SSM_CONV_CH), F32),
            pltpu.VMEM((seq, GROUP_W), F32),
            pltpu.VMEM((seq, GROUP_W), F32),
            pltpu.VMEM((2, 2, GROUP_W // 2, GROUP_W // 2), F32),
        ],
        compiler_params=_cparams(("arbitrary",)),
        name="mixer_ssd_dec" if has_state else "mixer_ssd_ctx",
    )(*args)


MLSTM_CPI = 4


def _mlstm_kernel(*refs, seq, n_seq, has_state, n_carry=0):
    (q_ref, k_ref, v_ref, o_ref, g_ref, cwq_ref, cwk_ref, cbq_ref, cbk_ref, gb_ref, ng_ref) = refs[0:11]
    refs = refs[11 + n_carry:]
    if has_state:
        c0_ref, n0_ref, m0_ref, y_ref = refs[0:4]
        outs = None
    else:
        y_ref = refs[0]
        outs = refs[1:4]
    (qpad_ref, kpad_ref, qc_ref, hf_ref, hb_ref, a_ref, cl_ref, rs_ref, rm_ref, cu_ref, rows_ref,
     cst_ref, nst_ref, mst_ref) = refs[4:]
    nc = seq // CHUNK
    head = pl.program_id(1)

    _fill_padded(qpad_ref, q_ref, seq, MLSTM_HEAD_DIM, n_seq)
    _fill_padded(kpad_ref, k_ref, seq, MLSTM_HEAD_DIM, n_seq)

    mask_f, mask_b = _causal_masks()
    row_id = _iota((CHUNK, CHUNK), 0)
    full = (CHUNK, CHUNK)

    sel = jnp.concatenate(
        [(row_id == col).astype(BF16) for col in (head, 4 + head, 8 + head, 12 + head)], axis=1)
    tmats = (mask_f.astype(BF16), mask_b.astype(BF16))
    masks = (mask_f, mask_b)
    forget_col = (_iota((CHUNK, CHUNK), 1) % 8) >= MLSTM_HEADS

    def local_body(s, carry):
        chunks = tuple(MLSTM_CPI * s + t for t in range(MLSTM_CPI))
        qs, ks, vs, qks, gsel = [], [], [], [], []
        for c in chunks:
            r = c * CHUNK if isinstance(c, int) else pl.multiple_of(c * CHUNK, CHUNK)
            rp = r if n_seq == 1 else c * CHUNK + 16 * (c // nc)
            q = _conv_silu_chunk(qpad_ref, cwq_ref, cbq_ref, rp) * (MLSTM_HEAD_DIM ** -0.5)
            k = _conv_silu_chunk(kpad_ref, cwk_ref, cbk_ref, rp)
            qc_ref[pl.ds(r, CHUNK), :] = q
            qs.append(q)
            ks.append(k)
            vs.append(v_ref[pl.ds(r, CHUNK), :])
            g = g_ref[pl.ds(r, CHUNK), :] + gb_ref[...]
            g = jnp.where(forget_col, _log_sigmoid(g), g)
            gsel.append([jnp.dot(p, sel, preferred_element_type=F32) for p in _split3(g)])
        for i in range(MLSTM_CPI):
            qks.append(_bdot_t(qs[i], ks[i]))
        items = [(i, d) for i in range(MLSTM_CPI) for d in range(2)]
        li_c = [gsel[i][0][:, 256 * d:256 * d + 128] + gsel[i][1][:, 256 * d:256 * d + 128]
                + gsel[i][2][:, 256 * d:256 * d + 128] for i, d in items]
        cum_c = []
        for i, d in items:
            lf = [gsel[i][p][:, 256 * d + 128:256 * d + 256].astype(BF16) for p in range(3)]
            cum_c.append(jnp.dot(tmats[d], lf[0], preferred_element_type=F32)
                         + jnp.dot(tmats[d], lf[1], preferred_element_type=F32)
                         + jnp.dot(tmats[d], lf[2], preferred_element_type=F32))
        cum_r = [x.T for x in cum_c]
        li_r = [x.T for x in li_c]
        last = [cum_c[n][CHUNK - 1:CHUNK, :] if d == 0 else cum_c[n][0:1, :]
                for n, (i, d) in enumerate(items)]
        dmat = [jnp.where(masks[d], cum_c[n] - cum_r[n] + li_r[n], -jnp.inf) for n, (i, d) in enumerate(items)]
        rowmax = [jnp.max(x, axis=1, keepdims=True) for x in dmat]
        sp = [qks[i] * jnp.exp(dmat[n] - rowmax[n]) for n, (i, d) in enumerate(items)]
        m_loc = [jnp.max(last[n] - cum_r[n][0:1, :] + li_r[n][0:1, :], axis=1, keepdims=True)
                 for n in range(len(items))]
        kw = [ks[i] * jnp.exp(last[n] - cum_c[n] + li_c[n] - m_loc[n]) for n, (i, d) in enumerate(items)]
        kwt = [x.T for x in kw]
        a_loc = [_bdot(sp[n], vs[i]) for n, (i, d) in enumerate(items)]
        c_loc = [_bdot(kwt[n], vs[i]) for n, (i, d) in enumerate(items)]
        for n, (i, d) in enumerate(items):
            c = chunks[i]
            a_ref[c, d] = a_loc[n]
            cl_ref[c, d] = c_loc[n]
            rs_ref[c, d] = jnp.broadcast_to(jnp.sum(sp[n], axis=1, keepdims=True), full)
            rm_ref[c, d] = jnp.broadcast_to(rowmax[n], full)
            cu_ref[c, d] = cum_c[n]
            rows_ref[c, d, 0:1, :] = jnp.sum(kw[n], axis=0, keepdims=True)
            rows_ref[c, d, 1:2, :] = jnp.broadcast_to(m_loc[n], (1, CHUNK))
            rows_ref[c, d, 2:3, :] = last[n]
        return carry

    n_local = n_seq * nc // MLSTM_CPI
    if n_local == 1:
        local_body(0, 0)
    else:
        lax.fori_loop(0, n_local, local_body, 0)

    if has_state:
        cst_ref[...] = c0_ref[...]
        nst_ref[...] = n0_ref[...]
        mst_ref[...] = m0_ref[...]
    else:
        cst_ref[...] = jnp.zeros(cst_ref.shape, F32)
        nst_ref[...] = jnp.zeros(nst_ref.shape, F32)
        mst_ref[...] = jnp.zeros(mst_ref.shape, F32)

    def state_step(i, d, lc):
        c = i * nc + lc
        r = pl.multiple_of(c * CHUNK, CHUNK)
        q = qc_ref[pl.ds(r, CHUNK), :]
        c_p = cst_ref[i, d]
        n_p = nst_ref[i, d]
        m_p = mst_ref[i, d]
        n_loc = rows_ref[c, d, 0:1, :]
        m_loc = rows_ref[c, d, 1:2, :]
        last = rows_ref[c, d, 2:3, :]
        rowmax = rm_ref[c, d]
        inter = cu_ref[c, d] + m_p
        m_t = jnp.maximum(inter, rowmax)
        f_intra = jnp.exp(rowmax - m_t)
        w_inter = jnp.exp(inter - m_t)
        num = a_ref[c, d] * f_intra + w_inter * _bdot(q, c_p)
        den = rs_ref[c, d] * f_intra + w_inter * jnp.sum(q * n_p, axis=1, keepdims=True)
        hh = num / jnp.maximum(jnp.abs(den), jnp.exp(-m_t))
        m_new = jnp.maximum(last + m_p, m_loc)
        s_p = jnp.exp(last + m_p - m_new)
        s_l = jnp.exp(m_loc - m_new)
        cst_ref[i, d] = s_p[:, 0:1] * c_p + s_l[:, 0:1] * cl_ref[c, d]
        nst_ref[i, d] = s_p * n_p + s_l * n_loc
        mst_ref[i, d] = m_new
        return r, hh

    def state_body(s, carry):
        for i in range(n_seq):
            r, hh = state_step(i, 0, s)
            hf_ref[pl.ds(r, CHUNK), :] = hh
            r, hh = state_step(i, 1, nc - 1 - s)
            hb_ref[pl.ds(r, CHUNK), :] = hh
        return carry

    lax.fori_loop(0, nc, state_body, 0, unroll=2)

    def out_body(c, carry):
        r = pl.multiple_of(c * CHUNK, CHUNK)
        hsum = hf_ref[pl.ds(r, CHUNK), :] + hb_ref[pl.ds(r, CHUNK), :]
        y = _sigmoid(o_ref[pl.ds(r, CHUNK), :]) * (_rms(hsum) * ng_ref[...])
        y_ref[pl.ds(r, CHUNK), :] = y.astype(BF16)
        return carry

    lax.fori_loop(0, n_seq * nc, out_body, 0, unroll=2)

    if outs is not None:
        outs[0][...] = cst_ref[...]
        outs[1][...] = nst_ref[...]
        outs[2][...] = mst_ref[...]


def _mixer_mlstm(u, row0, nb, seq, conv_w, conv_b, gate_b, norm_g, state, layer=0, carry=None):
    has_state = state is not None
    hd = MLSTM_HEAD_DIM
    nc = seq // CHUNK
    n_seq = max(1, MLSTM_CPI // nc)
    rows = n_seq * seq
    gnc = n_seq * nc
    roff = row0 // rows
    col = lambda base: (lambda b, h: (b + roff, base // hd + h))
    in_specs = [
        pl.BlockSpec((rows, hd), col(C_QK)),
        pl.BlockSpec((rows, hd), col(C_QK + GROUP_W)),
        pl.BlockSpec((rows, hd), col(C_V)),
        pl.BlockSpec((rows, hd), col(C_O)),
        pl.BlockSpec((rows, 128), lambda b, h: (b + roff, C_G // 128)),
        pl.BlockSpec((8, hd), lambda b, h: (0, h)),
        pl.BlockSpec((8, hd), lambda b, h: (0, MLSTM_HEADS + h)),
        pl.BlockSpec((1, hd), lambda b, h: (0, h)),
        pl.BlockSpec((1, hd), lambda b, h: (0, MLSTM_HEADS + h)),
        pl.BlockSpec((1, 128), lambda b, h: (0, 0)),
        pl.BlockSpec((1, hd), lambda b, h: (0, h)),
    ]
    args = [u, u, u, u, u, conv_w, conv_w, conv_b, conv_b, gate_b, norm_g]
    c_spec = pl.BlockSpec((n_seq, 2, None, hd, hd), lambda b, h: (b, 0, h, 0, 0))
    n_spec = pl.BlockSpec((n_seq, 2, None, 1, hd), lambda b, h: (b, 0, h, 0, 0))
    y_spec = pl.BlockSpec((rows, hd), lambda b, h: (b, h))
    y_shape = jax.ShapeDtypeStruct((nb * seq, GROUP_W), BF16)
    aliases = {}
    n_carry = 0
    if has_state:
        in_specs += [c_spec, n_spec, n_spec]
        args += list(state)
        out_specs, out_shape = y_spec, y_shape
    else:
        c_specs, c_args = _carried(carry)
        n_carry = len(c_args)
        aliases = {len(args) + k: 1 + k for k in range(n_carry)}
        in_specs += c_specs
        args += c_args
        c_all = pl.BlockSpec((n_seq, None, 2, None, hd, hd), lambda b, h: (b, layer, 0, h, 0, 0))
        n_all = pl.BlockSpec((n_seq, None, 2, None, 1, hd), lambda b, h: (b, layer, 0, h, 0, 0))
        out_specs = [y_spec, c_all, n_all, n_all]
        out_shape = [y_shape,
                     jax.ShapeDtypeStruct((nb, DEPTH, 2, MLSTM_HEADS, hd, hd), F32),
                     jax.ShapeDtypeStruct((nb, DEPTH, 2, MLSTM_HEADS, 1, hd), F32),
                     jax.ShapeDtypeStruct((nb, DEPTH, 2, MLSTM_HEADS, 1, hd), F32)]
    return pl.pallas_call(
        functools.partial(_mlstm_kernel, seq=seq, n_seq=n_seq, has_state=has_state, n_carry=n_carry),
        grid=(nb // n_seq, MLSTM_HEADS),
        in_specs=in_specs,
        out_specs=out_specs,
        out_shape=out_shape,
        input_output_aliases=aliases,
        scratch_shapes=[
            pltpu.VMEM((rows + 16 * n_seq, hd), F32), pltpu.VMEM((rows + 16 * n_seq, hd), F32),
            pltpu.VMEM((rows, hd), F32), pltpu.VMEM((rows, hd), F32), pltpu.VMEM((rows, hd), F32),
            pltpu.VMEM((gnc, 2, hd, hd), F32), pltpu.VMEM((gnc, 2, hd, hd), F32),
            pltpu.VMEM((gnc, 2, hd, hd), F32), pltpu.VMEM((gnc, 2, hd, hd), F32), pltpu.VMEM((gnc, 2, hd, hd), F32),
            pltpu.VMEM((gnc, 2, 8, hd), F32),
            pltpu.VMEM((n_seq, 2, hd, hd), F32), pltpu.VMEM((n_seq, 2, 1, hd), F32), pltpu.VMEM((n_seq, 2, 1, hd), F32),
        ],
        compiler_params=_cparams(("arbitrary", "arbitrary")),
        name="mixer_mlstm_dec" if has_state else "mixer_mlstm_ctx",
    )(*args)


KV_SUB = 256


ATT_SUB = 256
DIFF_QB = 1024
GQA_QB = 512


QK_SCALE_LOG2 = (64 ** -0.5) * math.log2(math.e)


def _softmax_parts(s):
    e = jnp.exp2(s - jnp.max(s, axis=1, keepdims=True))
    return e.astype(BF16), jnp.sum(e, axis=1, keepdims=True)


def _diff_lambda(lam_ref, lam_init):
    lp = lam_ref[...]
    return (jnp.exp(jnp.sum(lp[0:1, :] * lp[1:2, :], axis=1, keepdims=True))
            - jnp.exp(jnp.sum(lp[2:3, :] * lp[3:4, :], axis=1, keepdims=True)) + lam_init)


def _exp_scores_t(s_t):
    e = jnp.exp2(s_t - jnp.max(s_t, axis=0, keepdims=True))
    return e.astype(BF16), jnp.sum(e, axis=0, keepdims=True)


def _pv_t(vt, el):
    e, l = el
    return jnp.dot(vt, e, preferred_element_type=F32) / l


def _diff_dec_kernel(q_ref, k_ref, v_ref, lam_ref, ck_ref, cv_ref, cosq_ref, sinq_ref, cosk_ref, sink_ref,
                     o_ref, k1_ref, k2_ref, vt_ref, *, seq, lam_init):
    past = PAST_LEN

    @pl.when(pl.program_id(2) == 0)
    def _():
        ck = ck_ref[...]
        k1_ref[0:past, :] = ck[:, 0:64].astype(BF16)
        k2_ref[0:past, :] = ck[:, 64:128].astype(BF16)
        vt_ref[:, 0:past] = cv_ref[...].T.astype(BF16)
        for s in range(seq // KV_SUB):
            r = s * KV_SUB
            kk = _rope(k_ref[r:r + KV_SUB, :], cosk_ref[r:r + KV_SUB, :], sink_ref[r:r + KV_SUB, :])
            k1_ref[past + r:past + r + KV_SUB, :] = kk[:, 0:64].astype(BF16)
            k2_ref[past + r:past + r + KV_SUB, :] = kk[:, 64:128].astype(BF16)
            vt_ref[:, past + r:past + r + KV_SUB] = v_ref[r:r + KV_SUB, :].T.astype(BF16)

    lam = _diff_lambda(lam_ref, lam_init)
    q = _rope(q_ref[...], cosq_ref[...], sinq_ref[...]) * QK_SCALE_LOG2
    n_sub = q.shape[0] // ATT_SUB
    k_maps = (k1_ref[...], k2_ref[...])
    q_t = [q[j * ATT_SUB:(j + 1) * ATT_SUB, m * 64:(m + 1) * 64].T.astype(BF16)
           for j in range(n_sub) for m in range(2)]
    scores = [jnp.dot(k_maps[n % 2], q_t[n], preferred_element_type=F32) for n in range(2 * n_sub)]
    probs = [_exp_scores_t(s) for s in scores]
    vt = vt_ref[...]
    pv = [_pv_t(vt, el) for el in probs]
    for j in range(n_sub):
        o = (pv[2 * j] - lam * pv[2 * j + 1]).T
        o_ref[j * ATT_SUB:(j + 1) * ATT_SUB, :] = (_rms(o) * (1.0 - lam_init)).astype(BF16)


def _diff_ctx_kernel(*refs, lam_init, n_carry=0):
    q_ref, k_ref, v_ref, lam_ref = refs[0:4]
    o_ref, ko_ref, vo_ref = refs[4 + n_carry:]
    ko_ref[...] = k_ref[...]
    vo_ref[...] = v_ref[...]
    lam = _diff_lambda(lam_ref, lam_init)
    q = q_ref[...] * QK_SCALE_LOG2
    k = k_ref[...].astype(BF16)
    v = v_ref[...].astype(BF16)
    cols = [h * 128 + m * 64 for h in range(DIFF_HEADS) for m in range(2)]
    scores = [_bdot_t(q[:, c0:c0 + 64], k[:, c0:c0 + 64]) for c0 in cols]
    parts = [_softmax_parts(s) for s in scores]
    pv = [jnp.dot(parts[n][0], v[:, (n // 2) * 128:(n // 2 + 1) * 128], preferred_element_type=F32)
          for n in range(2 * DIFF_HEADS)]
    outs = []
    for h in range(DIFF_HEADS):
        o = pv[2 * h] / parts[2 * h][1] - lam * (pv[2 * h + 1] / parts[2 * h + 1][1])
        outs.append(_rms(o) * (1.0 - lam_init))
    o_ref[...] = jnp.concatenate(outs, axis=1).astype(BF16)


def _mixer_diff(u, row0, nb, seq, lam_rows, lam_init, ctx, layer=0, carry=None):
    roff = row0 // seq
    o_shape = jax.ShapeDtypeStruct((nb * seq, GROUP_W), BF16)
    lam_spec3 = pl.BlockSpec((8, 128), lambda b, h, i: (0, 0))
    if ctx is None:
        col = lambda base: pl.BlockSpec((seq, GROUP_W), lambda b: (b + roff, base // GROUP_W))
        c_specs, c_args = _carried(carry)
        slab = pl.BlockSpec((None, None, seq, GROUP_W), lambda b: (b, layer, 0, 0))
        slab_shape = jax.ShapeDtypeStruct((nb, DEPTH, seq, GROUP_W), F32)
        return pl.pallas_call(
            functools.partial(_diff_ctx_kernel, lam_init=lam_init, n_carry=len(c_args)),
            grid=(nb,),
            in_specs=[col(B_Q), col(B_K), col(B_V), pl.BlockSpec((8, 128), lambda b: (0, 0))] + c_specs,
            out_specs=[pl.BlockSpec((seq, GROUP_W), lambda b: (b, 0)), slab, slab],
            out_shape=[o_shape, slab_shape, slab_shape],
            input_output_aliases={4 + k: 1 + k for k in range(len(c_args))},
            compiler_params=_cparams(("arbitrary",)),
            name="mixer_diff_ctx",
        )(u, u, u, lam_rows, *c_args)
    ck, cv, cos, sin = ctx
    qb = min(DIFF_QB, seq)
    qoff = row0 // qb
    nq = seq // qb
    lk = seq + PAST_LEN
    in_specs = [
        pl.BlockSpec((qb, 128), lambda b, h, i: (qoff + b * nq + i, B_Q // 128 + h)),
        pl.BlockSpec((seq, 128), lambda b, h, i: (b + roff, B_K // 128 + h)),
        pl.BlockSpec((seq, 128), lambda b, h, i: (b + roff, B_V // 128 + h)),
        lam_spec3,
        pl.BlockSpec((None, PAST_LEN, 128), lambda b, h, i: (b, 0, h)),
        pl.BlockSpec((None, PAST_LEN, 128), lambda b, h, i: (b, 0, h)),
        pl.BlockSpec((qb, 128), lambda b, h, i: (i, 0)),
        pl.BlockSpec((qb, 128), lambda b, h, i: (i, 0)),
        pl.BlockSpec((seq, 128), lambda b, h, i: (0, 0)),
        pl.BlockSpec((seq, 128), lambda b, h, i: (0, 0)),
    ]
    return pl.pallas_call(
        functools.partial(_diff_dec_kernel, seq=seq, lam_init=lam_init),
        grid=(nb, DIFF_HEADS, nq),
        in_specs=in_specs,
        out_specs=pl.BlockSpec((qb, 128), lambda b, h, i: (b * nq + i, h)),
        out_shape=o_shape,
        scratch_shapes=[pltpu.VMEM((lk, 64), BF16), pltpu.VMEM((lk, 64), BF16), pltpu.VMEM((128, lk), BF16)],
        compiler_params=_cparams(("arbitrary", "arbitrary", "arbitrary")),
        name="mixer_diff_dec",
    )(u, u, u, lam_rows, ck, cv, cos, sin, cos, sin)


def _seg_rms_pair(x, gain):
    low = _iota(x.shape, 1) < 64
    sq = x * x
    ms_lo = jnp.sum(jnp.where(low, sq, 0.0), axis=1, keepdims=True) * (1.0 / 64)
    ms_hi = jnp.sum(jnp.where(low, 0.0, sq), axis=1, keepdims=True) * (1.0 / 64)
    return x * lax.rsqrt(jnp.where(low, ms_lo, ms_hi) + EPS) * gain


def _gqa_attend(q_heads, kf, vf):
    rows = q_heads[0].shape[0]
    s = _bdot_t(jnp.concatenate(q_heads, axis=0), kf)
    e, l = _softmax_parts(s)
    o = jnp.dot(e, vf, preferred_element_type=F32) / l
    return [o[g * rows:(g + 1) * rows, :] for g in range(len(q_heads))]


def _gqa_dec_kernel(q_ref, kv_ref, qg_ref, kg_ref, ck_ref, cv_ref, cosq_ref, sinq_ref, cosk_ref, sink_ref,
                    o_ref, kf_ref, vt_ref, *, seq):
    past = PAST_LEN

    @pl.when(pl.program_id(2) == 0)
    def _():
        kf_ref[0:past, :] = ck_ref[...].astype(BF16)
        vt_ref[:, 0:past] = cv_ref[...].T.astype(BF16)
        for s in range(seq // KV_SUB):
            r = s * KV_SUB
            blk = kv_ref[r:r + KV_SUB, :]
            kn = _rope(_seg_rms_pair(blk, kg_ref[...]), cosk_ref[r:r + KV_SUB, :], sink_ref[r:r + KV_SUB, :])
            kf_ref[past + r:past + r + KV_SUB, :] = kn[:, 0:64].astype(BF16)
            vt_ref[:, past + r:past + r + KV_SUB] = blk[:, 64:128].T.astype(BF16)

    n_sub = q_ref.shape[0] // ATT_SUB
    heads_t = []
    for j in range(n_sub):
        rows = slice(j * ATT_SUB, (j + 1) * ATT_SUB)
        for half in range(2):
            qh = _seg_rms_pair(q_ref[rows, half * 128:(half + 1) * 128], qg_ref[...])
            qh = _rope(qh, cosq_ref[rows, :], sinq_ref[rows, :]) * QK_SCALE_LOG2
            heads_t += [qh[:, 0:64].T.astype(BF16), qh[:, 64:128].T.astype(BF16)]
    kf = kf_ref[...]
    scores = [jnp.dot(kf, q_t, preferred_element_type=F32) for q_t in heads_t]
    probs = [_exp_scores_t(s) for s in scores]
    vt = vt_ref[...]
    outs = [_pv_t(vt, el).T for el in probs]
    for j in range(n_sub):
        o_ref[j * ATT_SUB:(j + 1) * ATT_SUB, :] = jnp.concatenate(outs[4 * j:4 * j + 4], axis=1).astype(BF16)


def _gqa_ctx_kernel(*refs, n_carry=0):
    q_ref, kv_ref, qg_ref, kg_ref = refs[0:4]
    o_ref, kn_ref, vo_ref = refs[4 + n_carry:]
    outs = []
    for kvh in range(GQA_KV_HEADS):
        blk = kv_ref[:, kvh * 128:(kvh + 1) * 128]
        kn = _seg_rms_pair(blk, kg_ref[...])[:, 0:64]
        kn_ref[:, kvh * 64:(kvh + 1) * 64] = kn
        vo_ref[:, kvh * 64:(kvh + 1) * 64] = blk[:, 64:128]
        heads = []
        for half in range(2):
            c0 = kvh * 256 + half * 128
            qh = _seg_rms_pair(q_ref[:, c0:c0 + 128], qg_ref[...]) * QK_SCALE_LOG2
            heads += [qh[:, 0:64], qh[:, 64:128]]
        outs += _gqa_attend(heads, kn.astype(BF16), blk[:, 64:128].astype(BF16))
    o_ref[...] = jnp.concatenate(outs, axis=1).astype(BF16)


def _mixer_gqa(u, row0, nb, seq, q_gain, k_gain, ctx, layer=0, carry=None):
    roff = row0 // seq
    o_shape = jax.ShapeDtypeStruct((nb * seq, GROUP_W), BF16)
    if ctx is None:
        gain = pl.BlockSpec((1, 128), lambda b: (0, 0))
        c_specs, c_args = _carried(carry)
        kvw = GQA_KV_HEADS * GQA_HEAD_DIM
        slab = pl.BlockSpec((None, None, seq, kvw), lambda b: (b, layer, 0, 0))
        slab_shape = jax.ShapeDtypeStruct((nb, DEPTH, seq, kvw), F32)
        return pl.pallas_call(
            functools.partial(_gqa_ctx_kernel, n_carry=len(c_args)),
            grid=(nb,),
            in_specs=[pl.BlockSpec((seq, GROUP_W), lambda b: (b + roff, D_Q // GROUP_W)),
                      pl.BlockSpec((seq, 256), lambda b: (b + roff, D_KV // 256)), gain, gain] + c_specs,
            out_specs=[pl.BlockSpec((seq, GROUP_W), lambda b: (b, 0)), slab, slab],
            out_shape=[o_shape, slab_shape, slab_shape],
            input_output_aliases={4 + k: 1 + k for k in range(len(c_args))},
            compiler_params=_cparams(("arbitrary",)),
            name="mixer_gqa_ctx",
        )(u, u, q_gain, k_gain, *c_args)
    ck, cv, cos, sin = ctx
    qb = GQA_QB
    qoff = row0 // qb
    nq = seq // qb
    lk = seq + PAST_LEN
    in_specs = [
        pl.BlockSpec((qb, 256), lambda b, h, i: (qoff + b * nq + i, D_Q // 256 + h)),
        pl.BlockSpec((seq, 128), lambda b, h, i: (b + roff, D_KV // 128 + h)),
        pl.BlockSpec((1, 128), lambda b, h, i: (0, 0)),
        pl.BlockSpec((1, 128), lambda b, h, i: (0, 0)),
        pl.BlockSpec((None, None, PAST_LEN, 64), lambda b, h, i: (b, h, 0, 0)),
        pl.BlockSpec((None, None, PAST_LEN, 64), lambda b, h, i: (b, h, 0, 0)),
        pl.BlockSpec((qb, 128), lambda b, h, i: (i, 0)),
        pl.BlockSpec((qb, 128), lambda b, h, i: (i, 0)),
        pl.BlockSpec((seq, 128), lambda b, h, i: (0, 0)),
        pl.BlockSpec((seq, 128), lambda b, h, i: (0, 0)),
    ]
    return pl.pallas_call(
        functools.partial(_gqa_dec_kernel, seq=seq),
        grid=(nb, GQA_KV_HEADS, nq),
        in_specs=in_specs,
        out_specs=pl.BlockSpec((qb, 256), lambda b, h, i: (b * nq + i, h)),
        out_shape=o_shape,
        scratch_shapes=[pltpu.VMEM((lk, 64), BF16), pltpu.VMEM((64, lk), BF16)],
        compiler_params=_cparams(("arbitrary", "arbitrary", "arbitrary")),
        name="mixer_gqa_dec",
    )(u, u, q_gain, k_gain, ck, cv, cos, sin, cos, sin)


def _rope_tables(n_tok):
    rows = n_tok // GRID_W
    r, c = jnp.meshgrid(jnp.arange(rows, dtype=F32), jnp.arange(GRID_W, dtype=F32), indexing='ij')
    nf = 16
    freqs = ROPE_THETA ** (-jnp.arange(nf, dtype=F32) / nf)
    ang = jnp.stack([r.reshape(-1)[:, None] * freqs, c.reshape(-1)[:, None] * freqs], axis=1)
    cos, sin = jnp.cos(ang), jnp.sin(ang)
    cos64 = jnp.concatenate([cos, cos], axis=2).reshape(n_tok, 64)
    sin64 = jnp.concatenate([-sin, sin], axis=2).reshape(n_tok, 64)
    return jnp.tile(cos64, (1, 2)), jnp.tile(sin64, (1, 2))


def _pad_lanes(v, n=128):
    v = v.reshape(1, -1)
    return jnp.pad(v, ((0, 0), (0, n - v.shape[1])))


def kernel(x_prompt, x_sample, c, cache_diff_k, cache_diff_v, cache_gqa_k, cache_gqa_v, state_ssm, state_mlstm_c, state_mlstm_n, state_mlstm_m, c_ctx, w_ada, b_ada, norm1, norm2, w_in, w_out, conv_ssd_w, conv_ssd_b, ssd_a_log, ssd_dt_bias, ssd_d, ssd_norm, diff_lq1, diff_lk1, diff_lq2, diff_lk2, conv_mlstm_w, conv_mlstm_b, mlstm_gate_b, mlstm_norm, gqa_q_norm, gqa_k_norm, w_ffn_in, w_ffn_out, norm_f):
    w_in_p = _prep_w_in(jnp.swapaxes(w_in, 1, 2))
    w_out_b = w_out.astype(BF16)
    gain1 = norm1.reshape(DEPTH, 1, D_MODEL)
    gain2 = norm2.reshape(DEPTH, 1, D_MODEL)
    w_ffn_in_b = w_ffn_in.astype(BF16)
    w_ffn_out_b = w_ffn_out.astype(BF16)
    cvec = jnp.concatenate([c_ctx[None, :], c, jnp.zeros((5, D_MODEL), F32)], axis=0)
    cos_t, sin_t = _rope_tables(DEC_SEQ)
    pad_taps = lambda w: jnp.pad(w, ((0, 8 - D_CONV), (0, 0)))

    mod = _modulation(cvec, w_ada, b_ada).reshape(DEPTH, 8, 1, 6 * D_MODEL)
    x = jnp.concatenate([x_prompt.reshape(N_CTX, D_MODEL), x_sample.reshape(N_DEC, D_MODEL)], axis=0)

    zeros = lambda *shape: jnp.zeros((BATCH, DEPTH) + shape, F32)
    ssm_all = (zeros(2, SSM_HEADS, SSM_HEAD_DIM, SSM_STATE),)
    diff_kv_all = [zeros(SEQ, GROUP_W), zeros(SEQ, GROUP_W)]
    mlstm_all = [zeros(2, MLSTM_HEADS, MLSTM_HEAD_DIM, MLSTM_HEAD_DIM), zeros(2, MLSTM_HEADS, 1, MLSTM_HEAD_DIM),
                 zeros(2, MLSTM_HEADS, 1, MLSTM_HEAD_DIM)]
    gqa_kv_all = [zeros(SEQ, GQA_KV_HEADS * GQA_HEAD_DIM), zeros(SEQ, GQA_KV_HEADS * GQA_HEAD_DIM)]
    for l in range(DEPTH):
        u = _inproj(x, mod, gain1, w_in_p, l)

        ssd_args = (pad_taps(conv_ssd_w[l]), conv_ssd_b[l].reshape(1, -1), _pad_lanes(ssd_a_log[l]),
                    _pad_lanes(ssd_dt_bias[l]), jnp.repeat(ssd_d[l], SSM_HEAD_DIM).reshape(1, GROUP_W),
                    ssd_norm[l].reshape(1, GROUP_W))
        ya_c, ssm_new = _mixer_ssd(u, 0, BATCH, SEQ, *ssd_args, None, l, ssm_all)
        ssm_all = (ssm_new,)
        ya_d = _mixer_ssd(u, N_CTX, DEC_BATCH, DEC_SEQ, *ssd_args, state_ssm[:, l])

        lam_init = 0.8 - 0.6 * math.exp(-0.3 * l)
        lam_rows = jnp.pad(jnp.stack([diff_lq1[l], diff_lk1[l], diff_lq2[l], diff_lk2[l]]), ((0, 4), (0, 64)))
        yb_c, *diff_kv_all = _mixer_diff(u, 0, BATCH, SEQ, lam_rows, lam_init, None, l, diff_kv_all)
        yb_d = _mixer_diff(u, N_CTX, DEC_BATCH, DEC_SEQ, lam_rows, lam_init,
                           (cache_diff_k[:, l].reshape(DEC_BATCH, PAST_LEN, GROUP_W),
                            cache_diff_v[:, l].reshape(DEC_BATCH, PAST_LEN, GROUP_W), cos_t, sin_t))

        ml_args = (pad_taps(conv_mlstm_w[l]), conv_mlstm_b[l].reshape(1, -1), _pad_lanes(mlstm_gate_b[l]),
                   mlstm_norm[l].reshape(1, GROUP_W))
        yc_c, *mlstm_all = _mixer_mlstm(u, 0, BATCH, SEQ, *ml_args, None, l, mlstm_all)
        m0 = jnp.broadcast_to(state_mlstm_m[:, l][..., None, None], (DEC_BATCH, 2, MLSTM_HEADS, 1, MLSTM_HEAD_DIM))
        yc_d = _mixer_mlstm(u, N_CTX, DEC_BATCH, DEC_SEQ, *ml_args,
                            (state_mlstm_c[:, l], state_mlstm_n[:, l][:, :, :, None, :], m0))

        q_gain = jnp.tile(gqa_q_norm[l], 2).reshape(1, 128)
        k_gain = jnp.tile(gqa_k_norm[l], 2).reshape(1, 128)
        yd_c, *gqa_kv_all = _mixer_gqa(u, 0, BATCH, SEQ, q_gain, k_gain, None, l, gqa_kv_all)
        yd_d = _mixer_gqa(u, N_CTX, DEC_BATCH, DEC_SEQ, q_gain, k_gain,
                          (cache_gqa_k[:, l].transpose(0, 2, 1, 3), cache_gqa_v[:, l].transpose(0, 2, 1, 3), cos_t, sin_t))

        x = _outproj((ya_c, yb_c, yc_c, yd_c), (ya_d, yb_d, yc_d, yd_d), x, mod, w_out_b, l)
        x = _ffn(x, mod, gain2, w_ffn_in_b, w_ffn_out_b, l)

    y_prompt = _final_norm(x, norm_f.reshape(1, D_MODEL), 0, N_CTX).reshape(BATCH, SEQ, D_MODEL)
    y_sample = _final_norm(x, norm_f.reshape(1, D_MODEL), N_CTX, N_DEC).reshape(DEC_BATCH, DEC_SEQ, D_MODEL)
    c_all, n_all, m_all = mlstm_all
    return (y_prompt, y_sample,
            diff_kv_all[0].reshape(BATCH, DEPTH, SEQ, DIFF_HEADS, 2, 64),
            diff_kv_all[1].reshape(BATCH, DEPTH, SEQ, DIFF_HEADS, 128),
            gqa_kv_all[0].reshape(BATCH, DEPTH, SEQ, GQA_KV_HEADS, GQA_HEAD_DIM),
            gqa_kv_all[1].reshape(BATCH, DEPTH, SEQ, GQA_KV_HEADS, GQA_HEAD_DIM),
            ssm_all[0], c_all, n_all[:, :, :, :, 0, :], m_all[:, :, :, :, 0, 0])
```
